```python
import jax, jax.numpy as jnp
from jax import lax
import numpy as np

D_MODEL = 1024
BATCH = 2
SEQ = 8192
DEPTH = 2

CHUNK = 64
D_MIX = D_MODEL
HEAD_DIM = 64
ATT_WIDTH = 3 * D_MIX // 8
ATT_HEADS = ATT_WIDTH // HEAD_DIM
ATT_PAST_CHUNKS = 8
ATT_BAND = (ATT_PAST_CHUNKS + 1) * CHUNK
REL_CLIP = 128
ML_WIDTH = 3 * D_MIX // 8
ML_HEADS = ML_WIDTH // HEAD_DIM
CONV_WIDTH = 4
GLA_WIDTH = D_MIX - ATT_WIDTH - ML_WIDTH
GLA_HEADS = 4
GLA_DV = GLA_WIDTH // GLA_HEADS
GLA_DK = GLA_DV // 2
GLA_KW = GLA_HEADS * GLA_DK
GLA_RANK = 16
GLA_TAU = 16.0
IN_SPLITS = (ATT_WIDTH, ATT_WIDTH, ATT_WIDTH, ATT_WIDTH,
             ML_WIDTH, ML_WIDTH, ML_WIDTH, ML_WIDTH, ML_HEADS, ML_HEADS, ML_WIDTH,
             GLA_KW, GLA_KW, GLA_WIDTH, GLA_RANK, GLA_WIDTH)
D_IN = sum(IN_SPLITS)
EPS = 1e-6
NEG = -1e30

kernel_name = "hymba_chunk_attn_mlstm_gla_trunk"


def rmsnorm(x, g):
    xf = x.astype(jnp.float32)
    y = xf * lax.rsqrt(jnp.mean(xf * xf, axis=-1, keepdims=True) + EPS)
    return (y * g.astype(jnp.float32)).astype(x.dtype)


def head_rmsnorm(x, g, n_heads):
    b, s, w = x.shape
    y = rmsnorm(x.reshape(b, s, n_heads, w // n_heads), g.reshape(n_heads, w // n_heads))
    return y.reshape(b, s, w)


def causal_conv(x, w):
    k_w = w.shape[0]
    s = x.shape[1]
    xp = jnp.pad(x, ((0, 0), (k_w - 1, 0), (0, 0)))
    return sum(xp[:, j:j + s] * w[j] for j in range(k_w))


def band_chunk_attention(q, k, v, rel_bias):
    bsz, s, h, d = q.shape
    nc = s // CHUNK
    pad = ATT_PAST_CHUNKS * CHUNK
    qc = q.reshape(bsz, nc, CHUNK, h, d)
    kp = jnp.pad(k, ((0, 0), (pad, 0), (0, 0), (0, 0))).reshape(bsz, nc + ATT_PAST_CHUNKS, CHUNK, h, d)
    vp = jnp.pad(v, ((0, 0), (pad, 0), (0, 0), (0, 0))).reshape(bsz, nc + ATT_PAST_CHUNKS, CHUNK, h, d)
    kb = jnp.concatenate([kp[:, m:m + nc] for m in range(ATT_PAST_CHUNKS + 1)], axis=2)
    vb = jnp.concatenate([vp[:, m:m + nc] for m in range(ATT_PAST_CHUNKS + 1)], axis=2)
    scores = jnp.einsum('bnqhd,bnkhd->bnhqk', qc, kb).astype(jnp.float32) * (d ** -0.5)
    rel = jnp.arange(CHUNK)[:, None] + pad - jnp.arange(ATT_BAND)[None, :]
    bias = rel_bias[:, jnp.clip(rel, -REL_CLIP, REL_CLIP) + REL_CLIP].astype(jnp.float32)
    kpos = (jnp.arange(nc)[:, None] - ATT_PAST_CHUNKS) * CHUNK + jnp.arange(ATT_BAND)[None, :]
    valid = kpos >= 0
    scores = jnp.where(valid[None, :, None, None, :], scores + bias[None, None], NEG)
    p = jax.nn.softmax(scores, axis=-1).astype(v.dtype)
    o = jnp.einsum('bnhqk,bnkhd->bnqhd', p, vb)
    return o.reshape(bsz, s, h * d)


def mlstm_chunkwise(q, k, v, i_pre, f_pre):
    bsz, s, h, d = q.shape
    nc = s // CHUNK
    L = CHUNK
    dt = q.dtype
    qc = q.astype(jnp.float32).reshape(bsz, nc, L, h, d)
    kc = (k.astype(jnp.float32) * (d ** -0.5)).reshape(bsz, nc, L, h, d)
    vc = v.astype(jnp.float32).reshape(bsz, nc, L, h, d)
    ig = i_pre.astype(jnp.float32).reshape(bsz, nc, L, h).transpose(0, 1, 3, 2)
    lf = jax.nn.log_sigmoid(f_pre.astype(jnp.float32)).reshape(bsz, nc, L, h).transpose(0, 1, 3, 2)
    bcum = jnp.cumsum(lf, axis=-1)
    g = bcum[..., -1]
    a = g[..., None] - bcum + ig
    m_loc = jnp.max(a, axis=-1)
    w = jnp.exp(a - m_loc[..., None])
    c_loc = jnp.einsum('bnhj,bnjhv,bnjhk->bnhvk', w, vc, kc)
    n_loc = jnp.einsum('bnhj,bnjhk->bnhk', w, kc)

    def step(carry, xs):
        c_st, n_st, m_st = carry
        g_c, m_l, c_l, n_l = xs
        m_new = jnp.maximum(g_c + m_st, m_l)
        s_old = jnp.exp(g_c + m_st - m_new)
        s_new = jnp.exp(m_l - m_new)
        c_new = s_old[..., None, None] * c_st + s_new[..., None, None] * c_l
        n_new = s_old[..., None] * n_st + s_new[..., None] * n_l
        return (c_new, n_new, m_new), (c_st, n_st, m_st)

    init = (jnp.zeros((bsz, h, d, d), jnp.float32), jnp.zeros((bsz, h, d), jnp.float32),
            jnp.zeros((bsz, h), jnp.float32))
    xs = (jnp.moveaxis(g, 1, 0), jnp.moveaxis(m_loc, 1, 0), jnp.moveaxis(c_loc, 1, 0), jnp.moveaxis(n_loc, 1, 0))
    _, (c_prev, n_prev, m_prev) = lax.scan(step, init, xs)
    c_prev = jnp.moveaxis(c_prev, 0, 1)
    n_prev = jnp.moveaxis(n_prev, 0, 1)
    m_prev = jnp.moveaxis(m_prev, 0, 1)

    causal = jnp.tril(jnp.ones((L, L), dtype=bool))
    log_d = jnp.where(causal, bcum[..., :, None] - bcum[..., None, :] + ig[..., None, :], NEG)
    inter_log = bcum + m_prev[..., None]
    m_row = jnp.maximum(inter_log, jnp.max(log_d, axis=-1))
    d_mat = jnp.exp(log_d - m_row[..., None])
    inter_w = jnp.exp(inter_log - m_row)
    sc = jnp.einsum('bnthd,bnjhd->bnhtj', qc, kc) * d_mat
    num = jnp.einsum('bnhtj,bnjhv->bnthv', sc, vc) + \
        inter_w.transpose(0, 1, 3, 2)[..., None] * jnp.einsum('bnhvk,bnthk->bnthv', c_prev, qc)
    den = jnp.sum(sc, axis=-1) + inter_w * jnp.einsum('bnhk,bnthk->bnht', n_prev, qc)
    den = jnp.maximum(jnp.abs(den), jnp.exp(-m_row))
    out = num / den.transpose(0, 1, 3, 2)[..., None]
    return out.reshape(bsz, s, h * d).astype(dt)


def gla_chunkwise(q, k, v, log_alpha):
    bsz, s, h, dk = q.shape
    dv = v.shape[-1]
    nc = s // CHUNK
    L = CHUNK
    dt = q.dtype

    def to_chunks(t):
        return jnp.moveaxis(t.astype(jnp.float32).reshape(bsz, nc, L, h, t.shape[-1]), 1, 0)

    xs = (to_chunks(q * (dk ** -0.5)), to_chunks(k), to_chunks(v), to_chunks(log_alpha))
    causal = jnp.tril(jnp.ones((L, L), dtype=bool))

    def step(state, chunk):
        qx, kx, vx, la = chunk
        bcum = jnp.cumsum(la, axis=1)
        diff = bcum[:, :, None] - bcum[:, None, :]
        decay = jnp.where(causal[None, :, :, None, None], jnp.exp(jnp.minimum(diff, 0.0)), 0.0)
        att = jnp.einsum('bthd,bjhd,btjhd->bhtj', qx, kx, decay)
        o_intra = jnp.einsum('bhtj,bjhv->bthv', att, vx)
        o_inter = jnp.einsum('bthd,bhdv->bthv', qx * jnp.exp(bcum), state)
        btot = bcum[:, -1]
        k_dec = kx * jnp.exp(btot[:, None] - bcum)
        state_new = jnp.exp(btot)[..., None] * state + jnp.einsum('bjhd,bjhv->bhdv', k_dec, vx)
        return state_new, o_intra + o_inter

    _, o = lax.scan(step, jnp.zeros((bsz, h, dk, dv), jnp.float32), xs)
    return jnp.moveaxis(o, 0, 1).reshape(bsz, s, h * dv).astype(dt)


def hybrid_layer(x, norm_g, w_in, b_gates, conv_w, w_alpha, b_alpha, rel_bias, ml_norm_g, gla_norm_g, w_out):
    bsz, s, _ = x.shape
    hn = rmsnorm(x, norm_g)
    z = jnp.einsum('bsd,de->bse', hn, w_in)
    split_idx = [int(i) for i in np.cumsum(IN_SPLITS)[:-1]]
    (aq, ak, av, ag, mq, mk, mv, mo, mi, mf, mg, gq, gk, gv, ga, gg) = jnp.split(z, split_idx, axis=-1)

    att = band_chunk_attention(aq.reshape(bsz, s, ATT_HEADS, HEAD_DIM), ak.reshape(bsz, s, ATT_HEADS, HEAD_DIM),
                               av.reshape(bsz, s, ATT_HEADS, HEAD_DIM), rel_bias)
    y_a = att * jax.nn.silu(ag)

    qk = jax.nn.silu(causal_conv(jnp.concatenate([mq, mk], axis=-1), conv_w))
    mq_c, mk_c = qk[..., :ML_WIDTH], qk[..., ML_WIDTH:]
    h_ml = mlstm_chunkwise(mq_c.reshape(bsz, s, ML_HEADS, HEAD_DIM), mk_c.reshape(bsz, s, ML_HEADS, HEAD_DIM),
                           mv.reshape(bsz, s, ML_HEADS, HEAD_DIM),
                           mi + b_gates[:ML_HEADS], mf + b_gates[ML_HEADS:])
    y_b = jax.nn.sigmoid(mo) * head_rmsnorm(h_ml, ml_norm_g, ML_HEADS) * jax.nn.silu(mg)

    log_alpha = jax.nn.log_sigmoid((jnp.einsum('bsr,rk->bsk', ga, w_alpha) + b_alpha).astype(jnp.float32)) / GLA_TAU
    h_gla = gla_chunkwise(gq.reshape(bsz, s, GLA_HEADS, GLA_DK), gk.reshape(bsz, s, GLA_HEADS, GLA_DK),
                          gv.reshape(bsz, s, GLA_HEADS, GLA_DV), log_alpha.reshape(bsz, s, GLA_HEADS, GLA_DK))
    y_c = head_rmsnorm(h_gla, gla_norm_g, GLA_HEADS) * jax.nn.silu(gg)

    y = jnp.concatenate([y_a, y_b, y_c], axis=-1)
    return x + jnp.einsum('bse,ed->bsd', y, w_out)


def setup_inputs(seed: int = 0) -> dict:
    key = jax.random.key(seed)
    ks = jax.random.split(key, 12)
    f32 = jnp.float32
    x = jax.random.normal(ks[0], (BATCH, SEQ, D_MODEL), f32)
    norm_g = 1.0 + 0.1 * jax.random.normal(ks[1], (DEPTH, D_MODEL), f32)
    w_in = jax.random.normal(ks[2], (DEPTH, D_MODEL, D_IN), f32) * D_MODEL ** -0.5
    f_bias = jnp.broadcast_to(jnp.linspace(3.0, 6.0, ML_HEADS, dtype=f32), (DEPTH, ML_HEADS))
    i_bias = 0.1 * jax.random.normal(ks[3], (DEPTH, ML_HEADS), f32)
    b_gates = jnp.concatenate([i_bias, f_bias + 0.1 * jax.random.normal(ks[4], (DEPTH, ML_HEADS), f32)], axis=-1)
    conv_w = jax.random.normal(ks[5], (DEPTH, CONV_WIDTH, 2 * ML_WIDTH), f32) * CONV_WIDTH ** -0.5
    w_alpha = jax.random.normal(ks[6], (DEPTH, GLA_RANK, GLA_KW), f32) * GLA_RANK ** -0.5
    b_alpha = 0.1 * jax.random.normal(ks[7], (DEPTH, GLA_KW), f32)
    rel_bias = 0.1 * jax.random.normal(ks[8], (DEPTH, ATT_HEADS, 2 * REL_CLIP + 1), f32)
    ml_norm_g = 1.0 + 0.1 * jax.random.normal(ks[9], (DEPTH, ML_WIDTH), f32)
    gla_norm_g = 1.0 + 0.1 * jax.random.normal(ks[10], (DEPTH, GLA_WIDTH), f32)
    k_out, k_fin = jax.random.split(ks[11])
    w_out = jax.random.normal(k_out, (DEPTH, D_MIX, D_MODEL), f32) * D_MIX ** -0.5
    final_g = 1.0 + 0.1 * jax.random.normal(k_fin, (D_MODEL,), f32)
    return {"x": x, "norm_g": norm_g, "w_in": w_in, "b_gates": b_gates, "conv_w": conv_w,
            "w_alpha": w_alpha, "b_alpha": b_alpha, "rel_bias": rel_bias, "ml_norm_g": ml_norm_g,
            "gla_norm_g": gla_norm_g, "w_out": w_out, "final_g": final_g}


def reference(x, norm_g, w_in, b_gates, conv_w, w_alpha, b_alpha, rel_bias, ml_norm_g, gla_norm_g, w_out, final_g):
    for l in range(DEPTH):
        x = hybrid_layer(x, norm_g[l], w_in[l], b_gates[l], conv_w[l], w_alpha[l], b_alpha[l], rel_bias[l],
                         ml_norm_g[l], gla_norm_g[l], w_out[l])
    return rmsnorm(x, final_g)
```

```python
import functools

import jax
import jax.numpy as jnp
from jax import lax
from jax.experimental import pallas as pl
from jax.experimental.pallas import tpu as pltpu

F32 = jnp.float32
BF16 = jnp.bfloat16

D_MODEL = 1024
CHUNK = 64
HEAD_DIM = 64
ATT_W = 384
ML_W = 384
GLA_KW = 128
GLA_VW = 256
GLA_DK = 32
GLA_RANK = 16
GLA_TAU = 16.0
ML_HEADS = 6
PAST_CHUNKS = 8
REL_CLIP = 128
CONV_W = 4
EPS = 1e-6
NEG = -1e30

LANES = 128
TILE = 256
NCH = TILE // CHUNK
HIST = PAST_CHUNKS * CHUNK
KEYS = HIST + TILE
BAND = (PAST_CHUNKS + 1) * CHUNK
ROLL_W = 1024

AQ, AK, AV, AG = 0, 384, 768, 1152
MQ, MK, MV, MO, MG = 1536, 1920, 2304, 2688, 3072
GQ, GK, GV, GG = 3456, 3584, 3712, 3968
SM = 4224
D_INP = 4352
SM_I = GLA_RANK
SM_F = GLA_RANK + ML_HEADS

VMEM_LIMIT = 56 * 1024 * 1024


def _log_sigmoid(x):
    return jnp.minimum(x, 0.0) - jnp.log1p(jnp.exp(-jnp.abs(x)))


def _silu(x):
    return x * jax.nn.sigmoid(x)


def _dot_nt(a, b):
    return lax.dot_general(a, b, (((1,), (1,)), ((), ())), preferred_element_type=F32)


def _dot_tn(a, b):
    return lax.dot_general(a, b, (((0,), (0,)), ((), ())), preferred_element_type=F32)


def _dot(a, b):
    return jnp.dot(a, b, preferred_element_type=F32)


def _layer_kernel(x_ref, ng_ref, win_ref, gb_ref, conv_ref, wal_ref, bal_ref, gp_ref,
                  mlg_ref, glg_ref, wout_ref, fg_ref,
                  o_ref,
                  z_ref, kbuf, vbuf, bias_ref, cbuf, ml_c, ml_n, ml_m, gla_s, y_ref,
                  *, final_norm):
    b = pl.program_id(0)
    i = pl.program_id(1)

    lane = lax.broadcasted_iota(jnp.int32, (1, LANES), 1)
    m0 = lane < HEAD_DIM

    @pl.when((b == 0) & (i == 0))
    def _build_bias():
        r = lax.broadcasted_iota(jnp.int32, (TILE, KEYS), 0)
        m = lax.broadcasted_iota(jnp.int32, (TILE, KEYS), 1)
        jj = m - ((r >> 6) << 6)
        in_band = (jj >= 0) & (jj < BAND)
        for h in range(ML_HEADS):
            row = jnp.broadcast_to(gp_ref[h:h + 1, :], (TILE, ROLL_W))
            rolled = pltpu.roll(row, KEYS, 1, stride=1, stride_axis=0)
            bias_ref[h] = jnp.where(in_band, rolled[:, :KEYS], NEG)

    @pl.when(i == 0)
    def _reset():
        kbuf[...] = jnp.zeros(kbuf.shape, kbuf.dtype)
        vbuf[...] = jnp.zeros(vbuf.shape, vbuf.dtype)
        cbuf[0:8, :] = jnp.zeros((8, cbuf.shape[1]), F32)
        ml_c[...] = jnp.zeros(ml_c.shape, F32)
        ml_n[...] = jnp.zeros(ml_n.shape, F32)
        ml_m[...] = jnp.zeros(ml_m.shape, F32)
        gla_s[...] = jnp.zeros(gla_s.shape, F32)

    x = x_ref[...]
    hn = x * lax.rsqrt(jnp.sum(x * x, axis=1, keepdims=True) * (1.0 / D_MODEL) + EPS) * ng_ref[...]
    hn_b = hn.astype(BF16)
    for n0 in range(0, D_INP, 256):
        z_ref[:, n0:n0 + 256] = _dot(hn_b, win_ref[:, n0:n0 + 256])

    kbuf[0:HIST, :] = kbuf[TILE:KEYS, :]
    vbuf[0:HIST, :] = vbuf[TILE:KEYS, :]
    kbuf[HIST:KEYS, :] = z_ref[:, AK:AK + ATT_W].astype(BF16)
    vbuf[HIST:KEYS, :] = z_ref[:, AV:AV + ATT_W].astype(BF16)

    kcol = lax.broadcasted_iota(jnp.int32, (1, KEYS), 1)
    pen = jnp.where(kcol + (i * TILE - HIST) >= 0, 0.0, NEG)

    for p in range(ATT_W // LANES):
        c0 = p * LANES
        q2 = z_ref[:, AQ + c0:AQ + c0 + LANES] * (HEAD_DIM ** -0.5)
        k2 = kbuf[:, c0:c0 + LANES]
        v2 = vbuf[:, c0:c0 + LANES]
        outs = []
        for e in range(2):
            msk = m0 if e == 0 else jnp.logical_not(m0)
            qm = jnp.where(msk, q2, 0.0).astype(BF16)
            s = _dot_nt(qm, k2) + bias_ref[2 * p + e] + pen
            mx = jnp.max(s, axis=1, keepdims=True)
            pe = jnp.exp(s - mx)
            l = jnp.sum(pe, axis=1, keepdims=True)
            outs.append(_dot(pe.astype(BF16), v2) / l)
        att = jnp.where(m0, outs[0], outs[1])
        y_ref[:, c0:c0 + LANES] = att * _silu(z_ref[:, AG + c0:AG + c0 + LANES])

    cbuf[8:8 + TILE, :] = z_ref[:, MQ:MQ + 2 * ML_W]
    for c0 in range(0, 2 * ML_W, LANES):
        acc = cbuf[8:8 + TILE, c0:c0 + LANES] * conv_ref[CONV_W - 1:CONV_W, c0:c0 + LANES]
        for sft in range(1, CONV_W):
            acc = acc + cbuf[8 - sft:8 - sft + TILE, c0:c0 + LANES] * \
                conv_ref[CONV_W - 1 - sft:CONV_W - sft, c0:c0 + LANES]
        act = _silu(acc)
        if c0 >= ML_W:
            act = act * (HEAD_DIM ** -0.5)
        z_ref[:, MQ + c0:MQ + c0 + LANES] = act
    cbuf[0:8, :] = cbuf[TILE:TILE + 8, :]

    small = z_ref[:, SM:SM + LANES]
    pre = small + gb_ref[...]
    lf = _log_sigmoid(pre)
    la = _log_sigmoid(_dot(small.astype(BF16), wal_ref[...]) + bal_ref[...]) * (1.0 / GLA_TAU)
    tr = lax.broadcasted_iota(jnp.int32, (TILE, TILE), 0)
    tc = lax.broadcasted_iota(jnp.int32, (TILE, TILE), 1)
    tri = jnp.where((tr >= tc) & ((tr >> 6) == (tc >> 6)), 1.0, 0.0).astype(F32)
    cum = jnp.dot(tri, jnp.concatenate([lf, la], axis=1), preferred_element_type=F32,
                  precision=lax.Precision.HIGHEST)
    bml = cum[:, :LANES]
    bgl = cum[:, LANES:]
    pre_t = pre.T
    bml_t = bml.T
    bgl_t = bgl.T

    cr = lax.broadcasted_iota(jnp.int32, (CHUNK, CHUNK), 0)
    cc = lax.broadcasted_iota(jnp.int32, (CHUNK, CHUNK), 1)
    causal = cr >= cc
    prow = lax.broadcasted_iota(jnp.int32, (LANES, LANES), 0)
    pcol = lax.broadcasted_iota(jnp.int32, (LANES, LANES), 1)
    pair_diag = (prow < HEAD_DIM) == (pcol < HEAD_DIM)
    prow_first = prow[:, 0:1] < HEAD_DIM

    for p in range(ML_W // LANES):
        c0 = p * LANES
        ct = ml_c[p]
        nvec = ml_n[p][0:1, :]
        mprev = [ml_m[2 * p + e][0:1, 0:CHUNK] for e in range(2)]
        for c in range(NCH):
            r0 = c * CHUNK
            q2 = z_ref[r0:r0 + CHUNK, MQ + c0:MQ + c0 + LANES]
            k2 = z_ref[r0:r0 + CHUNK, MK + c0:MK + c0 + LANES]
            v2b = z_ref[r0:r0 + CHUNK, MV + c0:MV + c0 + LANES].astype(BF16)
            k2b = k2.astype(BF16)
            num_inter = _dot(q2.astype(BF16), ct.astype(BF16))
            qn = q2 * nvec
            nums, dens, wcols, solds, mnews = [], [], [], [], []
            for e in range(2):
                h = 2 * p + e
                msk = m0 if e == 0 else jnp.logical_not(m0)
                bt = jnp.broadcast_to(bml[r0:r0 + CHUNK, SM_F + h:SM_F + h + 1], (CHUNK, CHUNK))
                it = jnp.broadcast_to(pre[r0:r0 + CHUNK, SM_I + h:SM_I + h + 1], (CHUNK, CHUNK))
                bj = bml_t[SM_F + h:SM_F + h + 1, r0:r0 + CHUNK]
                ij = pre_t[SM_I + h:SM_I + h + 1, r0:r0 + CHUNK]
                s_qk = _dot_nt(jnp.where(msk, q2, 0.0).astype(BF16), k2b)
                log_d = jnp.where(causal, bt - bj + ij, NEG)
                inter_log = bt + mprev[e]
                m_row = jnp.maximum(inter_log, jnp.max(log_d, axis=1, keepdims=True))
                d_mat = jnp.exp(log_d - m_row)
                inter_w = jnp.exp(inter_log - m_row)
                sc = s_qk * d_mat
                num_intra = _dot(sc.astype(BF16), v2b)
                den_inter = jnp.sum(jnp.where(msk, qn, 0.0), axis=1, keepdims=True)
                den = jnp.sum(sc, axis=1, keepdims=True) + inter_w[:, 0:1] * den_inter
                den = jnp.maximum(jnp.abs(den), jnp.exp(-m_row[:, 0:1]))
                iw128 = jnp.broadcast_to(inter_w[:, 0:1], (CHUNK, LANES))
                nums.append(num_intra + iw128 * num_inter)
                dens.append(den)
                g_row = bt[CHUNK - 1:CHUNK, :]
                a_row = g_row - bj + ij
                m_loc = jnp.max(a_row, axis=1, keepdims=True)
                m_new = jnp.maximum(g_row + mprev[e], m_loc)
                wcols.append(jnp.exp(g_row - bt + it - m_new))
                solds.append(jnp.exp(g_row + mprev[e] - m_new))
                mnews.append(m_new)
            out = jnp.where(m0, nums[0] / dens[0], nums[1] / dens[1])
            y_ref[r0:r0 + CHUNK, ATT_W + c0:ATT_W + c0 + LANES] = out
            wpair = jnp.where(m0, jnp.broadcast_to(wcols[0][:, 0:1], (CHUNK, LANES)),
                              jnp.broadcast_to(wcols[1][:, 0:1], (CHUNK, LANES)))
            kw = k2 * wpair
            upd = _dot_tn(kw.astype(BF16), v2b)
            sold_col = jnp.where(prow_first, solds[0][:, 0:1], solds[1][:, 0:1])
            ct = jnp.where(pair_diag, sold_col * ct + upd, 0.0)
            sold_row = jnp.where(m0, jnp.broadcast_to(solds[0][:, 0:1], (1, LANES)),
                                 jnp.broadcast_to(solds[1][:, 0:1], (1, LANES)))
            nvec = sold_row * nvec + jnp.sum(kw, axis=0, keepdims=True)
            mprev = mnews
        ml_c[p] = ct
        ml_n[p] = jnp.broadcast_to(nvec, (8, LANES))
        for e in range(2):
            ml_m[2 * p + e] = jnp.broadcast_to(mprev[e][:, 0:1], (8, LANES))

    for p in range(ML_W // LANES):
        c0 = p * LANES
        hcur = y_ref[:, ATT_W + c0:ATT_W + c0 + LANES]
        sq = hcur * hcur
        ms0 = jnp.sum(jnp.where(m0, sq, 0.0), axis=1, keepdims=True)
        ms1 = jnp.sum(jnp.where(m0, 0.0, sq), axis=1, keepdims=True)
        ms = jnp.where(m0, ms0, ms1) * (1.0 / HEAD_DIM)
        hnorm = hcur * lax.rsqrt(ms + EPS) * mlg_ref[:, c0:c0 + LANES]
        y_ref[:, ATT_W + c0:ATT_W + c0 + LANES] = (
            jax.nn.sigmoid(z_ref[:, MO + c0:MO + c0 + LANES]) * hnorm
            * _silu(z_ref[:, MG + c0:MG + c0 + LANES]))

    grow = lax.broadcasted_iota(jnp.int32, (LANES, GLA_VW), 0)
    gcol = lax.broadcasted_iota(jnp.int32, (LANES, GLA_VW), 1)
    gla_diag = (grow >> 5) == (gcol >> 6)
    s_all = gla_s[...]
    yc0 = ATT_W + ML_W
    for c in range(NCH):
        r0 = c * CHUNK
        bc = bgl[r0:r0 + CHUNK, :]
        bref = bc[CHUNK // 2 - 1:CHUNK // 2, :]
        btot = bc[CHUNK - 1:CHUNK, :]
        gq = z_ref[r0:r0 + CHUNK, GQ:GQ + GLA_KW] * (GLA_DK ** -0.5)
        gk = z_ref[r0:r0 + CHUNK, GK:GK + GLA_KW]
        gvb = z_ref[r0:r0 + CHUNK, GV:GV + GLA_VW].astype(BF16)
        qe = gq * jnp.exp(bc - bref)
        keb = (gk * jnp.exp(bref - bc)).astype(BF16)
        o = _dot((gq * jnp.exp(bc)).astype(BF16), s_all.astype(BF16))
        intra = []
        for hp in range(2):
            oh = []
            for e in range(2):
                h = 2 * hp + e
                hm = (lane >> 5) == h
                a = _dot_nt(jnp.where(hm, qe, 0.0).astype(BF16), keb)
                a = jnp.where(causal, a, 0.0)
                oh.append(_dot(a.astype(BF16), gvb[:, hp * LANES:(hp + 1) * LANES]))
            intra.append(jnp.where(m0, oh[0], oh[1]))
        y_ref[r0:r0 + CHUNK, yc0:yc0 + GLA_VW] = o + jnp.concatenate(intra, axis=1)
        kd = (gk * jnp.exp(btot - bc)).astype(BF16)
        upd = _dot_tn(kd, gvb)
        dec_col = jnp.exp(bgl_t[:, r0 + CHUNK - 1:r0 + CHUNK])
        s_all = jnp.where(gla_diag, dec_col * s_all + upd, 0.0)
    gla_s[...] = s_all

    for hp in range(GLA_VW // LANES):
        c0 = hp * LANES
        hcur = y_ref[:, yc0 + c0:yc0 + c0 + LANES]
        sq = hcur * hcur
        ms0 = jnp.sum(jnp.where(m0, sq, 0.0), axis=1, keepdims=True)
        ms1 = jnp.sum(jnp.where(m0, 0.0, sq), axis=1, keepdims=True)
        ms = jnp.where(m0, ms0, ms1) * (1.0 / HEAD_DIM)
        hnorm = hcur * lax.rsqrt(ms + EPS) * glg_ref[:, c0:c0 + LANES]
        y_ref[:, yc0 + c0:yc0 + c0 + LANES] = hnorm * _silu(z_ref[:, GG + c0:GG + c0 + LANES])

    xn = x_ref[...] + _dot(y_ref[...].astype(BF16), wout_ref[...])
    if final_norm:
        xn = xn * lax.rsqrt(jnp.sum(xn * xn, axis=1, keepdims=True) * (1.0 / D_MODEL) + EPS) * fg_ref[...]
    o_ref[...] = xn


def _full_spec(shape):
    nd = len(shape)
    return pl.BlockSpec(shape, lambda b, i, _nd=nd: (0,) * _nd)


def _layer_call(x, prm, final_g, final_norm):
    bsz, seq, _ = x.shape
    args = (x, prm["ng"], prm["win"], prm["gb"], prm["conv"], prm["wal"], prm["bal"], prm["gp"],
            prm["mlg"], prm["glg"], prm["wout"], final_g)
    x_spec = pl.BlockSpec((None, TILE, D_MODEL), lambda b, i: (b, i, 0))
    in_specs = [x_spec] + [_full_spec(a.shape) for a in args[1:]]
    scratch = [
        pltpu.VMEM((TILE, D_INP), F32),
        pltpu.VMEM((KEYS, ATT_W), BF16),
        pltpu.VMEM((KEYS, ATT_W), BF16),
        pltpu.VMEM((ML_HEADS, TILE, KEYS), F32),
        pltpu.VMEM((TILE + 16, 2 * ML_W), F32),
        pltpu.VMEM((ML_W // LANES, LANES, LANES), F32),
        pltpu.VMEM((ML_W // LANES, 8, LANES), F32),
        pltpu.VMEM((ML_HEADS, 8, LANES), F32),
        pltpu.VMEM((LANES, GLA_VW), F32),
        pltpu.VMEM((TILE, D_MODEL), F32),
    ]
    return pl.pallas_call(
        functools.partial(_layer_kernel, final_norm=final_norm),
        out_shape=jax.ShapeDtypeStruct(x.shape, x.dtype),
        grid=(bsz, seq // TILE),
        in_specs=in_specs,
        out_specs=x_spec,
        scratch_shapes=scratch,
        compiler_params=pltpu.CompilerParams(
            dimension_semantics=("arbitrary", "arbitrary"),
            vmem_limit_bytes=VMEM_LIMIT),
        name="hybrid_layer_final" if final_norm else "hybrid_layer",
    )(*args)


def _prep_layer(norm_g, w_in, b_gates, conv_w, w_alpha, b_alpha, rel_bias, ml_norm_g, gla_norm_g, w_out):
    pad = jnp.zeros((D_MODEL, D_INP - SM - GLA_RANK - 2 * ML_HEADS), w_in.dtype)
    win = jnp.concatenate([w_in[:, :3072], w_in[:, 3084:3980], w_in[:, 3996:4252],
                           w_in[:, 3980:3996], w_in[:, 3072:3084], pad], axis=1).astype(BF16)
    gb = jnp.zeros((1, LANES), F32).at[0, SM_I:SM_I + 2 * ML_HEADS].set(b_gates)
    wal = jnp.zeros((LANES, GLA_KW), F32).at[:GLA_RANK, :].set(w_alpha).astype(BF16)
    nh = rel_bias.shape[0]
    gp = jnp.concatenate([
        jnp.broadcast_to(rel_bias[:, 2 * REL_CLIP:], (nh, KEYS - REL_CLIP + 1)),
        rel_bias[:, 2 * REL_CLIP - 1:0:-1],
        jnp.broadcast_to(rel_bias[:, :1], (nh, ROLL_W - KEYS - REL_CLIP)),
    ], axis=1)
    return dict(ng=norm_g.reshape(1, D_MODEL), win=win, gb=gb, conv=conv_w, wal=wal,
                bal=b_alpha.reshape(1, GLA_KW), gp=gp, mlg=ml_norm_g.reshape(1, ML_W),
                glg=gla_norm_g.reshape(1, GLA_VW), wout=w_out.astype(BF16))


def kernel(x, norm_g, w_in, b_gates, conv_w, w_alpha, b_alpha, rel_bias, ml_norm_g, gla_norm_g, w_out, final_g):
    depth = norm_g.shape[0]
    fg = final_g.reshape(1, D_MODEL)
    for l in range(depth):
        prm = _prep_layer(norm_g[l], w_in[l], b_gates[l], conv_w[l], w_alpha[l], b_alpha[l], rel_bias[l],
                          ml_norm_g[l], gla_norm_g[l], w_out[l])
        x = _layer_call(x, prm, fg, final_norm=(l == depth - 1))
    return x
```

```python
import functools

import jax
import jax.numpy as jnp
from jax import lax
from jax.experimental import pallas as pl
from jax.experimental.pallas import tpu as pltpu

F32 = jnp.float32
BF16 = jnp.bfloat16

D_MODEL = 1024
CHUNK = 64
HEAD_DIM = 64
ATT_W = 384
ML_W = 384
GLA_KW = 128
GLA_VW = 256
GLA_DK = 32
GLA_RANK = 16
GLA_TAU = 16.0
ML_HEADS = 6
PAST_CHUNKS = 8
REL_CLIP = 128
CONV_W = 4
EPS = 1e-6
NEG = -1e30

LANES = 128
TILE = 256
NCH = TILE // CHUNK
HIST = PAST_CHUNKS * CHUNK
KEYS = HIST + TILE
BAND = (PAST_CHUNKS + 1) * CHUNK
ROLL_W = 1024

AQ, AK, AV, AG = 0, 384, 768, 1152
MQ, MK, MV, MO, MG = 1536, 1920, 2304, 2688, 3072
GQ, GK, GV, GG = 3456, 3584, 3712, 3968
SM = 4224
D_INP = 4352
SM_I = GLA_RANK
SM_F = GLA_RANK + ML_HEADS

VMEM_LIMIT = 56 * 1024 * 1024


def _log_sigmoid(x):
    return jnp.minimum(x, 0.0) - jnp.log1p(jnp.exp(-jnp.abs(x)))


def _silu(x):
    return x * jax.nn.sigmoid(x)


def _dot_nt(a, b):
    return lax.dot_general(a, b, (((1,), (1,)), ((), ())), preferred_element_type=F32)


def _dot_tn(a, b):
    return lax.dot_general(a, b, (((0,), (0,)), ((), ())), preferred_element_type=F32)


def _dot(a, b):
    return jnp.dot(a, b, preferred_element_type=F32)


def _layer_kernel(x_ref, ng_ref, win_ref, gb_ref, conv_ref, wal_ref, bal_ref, gp_ref,
                  mlg_ref, glg_ref, wout_ref, fg_ref,
                  o_ref,
                  z_ref, kbuf, vbuf, bias_ref, cbuf, ml_c, ml_n, ml_m, gla_s, y_ref,
                  *, final_norm):
    b = pl.program_id(0)
    i = pl.program_id(1)

    lane = lax.broadcasted_iota(jnp.int32, (1, LANES), 1)
    m0 = lane < HEAD_DIM

    @pl.when((b == 0) & (i == 0))
    def _build_bias():
        r = lax.broadcasted_iota(jnp.int32, (TILE, KEYS), 0)
        m = lax.broadcasted_iota(jnp.int32, (TILE, KEYS), 1)
        jj = m - ((r >> 6) << 6)
        in_band = (jj >= 0) & (jj < BAND)
        for h in range(ML_HEADS):
            row = jnp.broadcast_to(gp_ref[h:h + 1, :], (TILE, ROLL_W))
            rolled = pltpu.roll(row, KEYS, 1, stride=1, stride_axis=0)
            bias_ref[h] = jnp.where(in_band, rolled[:, :KEYS], NEG)

    @pl.when(i == 0)
    def _reset():
        kbuf[...] = jnp.zeros(kbuf.shape, kbuf.dtype)
        vbuf[...] = jnp.zeros(vbuf.shape, vbuf.dtype)
        cbuf[0:8, :] = jnp.zeros((8, cbuf.shape[1]), F32)
        ml_c[...] = jnp.zeros(ml_c.shape, F32)
        ml_n[...] = jnp.zeros(ml_n.shape, F32)
        ml_m[...] = jnp.zeros(ml_m.shape, F32)
        gla_s[...] = jnp.zeros(gla_s.shape, F32)

    x = x_ref[...]
    hn = x * lax.rsqrt(jnp.sum(x * x, axis=1, keepdims=True) * (1.0 / D_MODEL) + EPS) * ng_ref[...]
    hn_b = hn.astype(BF16)
    for n0 in range(0, D_INP, 256):
        z_ref[:, n0:n0 + 256] = _dot_nt(hn_b, win_ref[n0:n0 + 256, :])

    kbuf[0:HIST, :] = kbuf[TILE:KEYS, :]
    vbuf[0:HIST, :] = vbuf[TILE:KEYS, :]
    kbuf[HIST:KEYS, :] = z_ref[:, AK:AK + ATT_W].astype(BF16)
    vbuf[HIST:KEYS, :] = z_ref[:, AV:AV + ATT_W].astype(BF16)

    kcol = lax.broadcasted_iota(jnp.int32, (1, KEYS), 1)
    pen = jnp.where(kcol + (i * TILE - HIST) >= 0, 0.0, NEG)

    for p in range(ATT_W // LANES):
        c0 = p * LANES
        q2 = z_ref[:, AQ + c0:AQ + c0 + LANES] * (HEAD_DIM ** -0.5)
        k2 = kbuf[:, c0:c0 + LANES]
        v2 = vbuf[:, c0:c0 + LANES]
        outs = []
        for e in range(2):
            msk = m0 if e == 0 else jnp.logical_not(m0)
            qm = jnp.where(msk, q2, 0.0).astype(BF16)
            s = _dot_nt(qm, k2) + bias_ref[2 * p + e] + pen
            mx = jnp.max(s, axis=1, keepdims=True)
            pe = jnp.exp(s - mx)
            l = jnp.sum(pe, axis=1, keepdims=True)
            outs.append(_dot(pe.astype(BF16), v2) / l)
        att = jnp.where(m0, outs[0], outs[1])
        y_ref[:, c0:c0 + LANES] = att * _silu(z_ref[:, AG + c0:AG + c0 + LANES])

    cbuf[8:8 + TILE, :] = z_ref[:, MQ:MQ + 2 * ML_W]
    for c0 in range(0, 2 * ML_W, LANES):
        acc = cbuf[8:8 + TILE, c0:c0 + LANES] * conv_ref[CONV_W - 1:CONV_W, c0:c0 + LANES]
        for sft in range(1, CONV_W):
            acc = acc + cbuf[8 - sft:8 - sft + TILE, c0:c0 + LANES] * \
                conv_ref[CONV_W - 1 - sft:CONV_W - sft, c0:c0 + LANES]
        act = _silu(acc)
        if c0 >= ML_W:
            act = act * (HEAD_DIM ** -0.5)
        z_ref[:, MQ + c0:MQ + c0 + LANES] = act
    cbuf[0:8, :] = cbuf[TILE:TILE + 8, :]

    small = z_ref[:, SM:SM + LANES]
    pre = small + gb_ref[...]
    lf = _log_sigmoid(pre)
    la = _log_sigmoid(_dot(small.astype(BF16), wal_ref[...]) + bal_ref[...]) * (1.0 / GLA_TAU)
    tr = lax.broadcasted_iota(jnp.int32, (TILE, TILE), 0)
    tc = lax.broadcasted_iota(jnp.int32, (TILE, TILE), 1)
    tri = jnp.where((tr >= tc) & ((tr >> 6) == (tc >> 6)), 1.0, 0.0).astype(F32)
    cum = jnp.dot(tri, jnp.concatenate([lf, la], axis=1), preferred_element_type=F32,
                  precision=lax.Precision.HIGHEST)
    bml = cum[:, :LANES]
    bgl = cum[:, LANES:]
    pre_t = pre.T
    bml_t = bml.T
    bgl_t = bgl.T

    cr = lax.broadcasted_iota(jnp.int32, (CHUNK, CHUNK), 0)
    cc = lax.broadcasted_iota(jnp.int32, (CHUNK, CHUNK), 1)
    causal = cr >= cc
    prow = lax.broadcasted_iota(jnp.int32, (LANES, LANES), 0)
    pcol = lax.broadcasted_iota(jnp.int32, (LANES, LANES), 1)
    pair_diag = (prow < HEAD_DIM) == (pcol < HEAD_DIM)
    prow_first = prow[:, 0:1] < HEAD_DIM

    for p in range(ML_W // LANES):
        c0 = p * LANES
        ct = ml_c[p]
        nvec = ml_n[p][0:1, :]
        mprev = [ml_m[2 * p + e][0:1, 0:CHUNK] for e in range(2)]
        for c in range(NCH):
            r0 = c * CHUNK
            q2 = z_ref[r0:r0 + CHUNK, MQ + c0:MQ + c0 + LANES]
            k2 = z_ref[r0:r0 + CHUNK, MK + c0:MK + c0 + LANES]
            v2b = z_ref[r0:r0 + CHUNK, MV + c0:MV + c0 + LANES].astype(BF16)
            k2b = k2.astype(BF16)
            num_inter = _dot(q2.astype(BF16), ct.astype(BF16))
            qn = q2 * nvec
            nums, dens, wcols, solds, mnews = [], [], [], [], []
            for e in range(2):
                h = 2 * p + e
                msk = m0 if e == 0 else jnp.logical_not(m0)
                bt = jnp.broadcast_to(bml[r0:r0 + CHUNK, SM_F + h:SM_F + h + 1], (CHUNK, CHUNK))
                it = jnp.broadcast_to(pre[r0:r0 + CHUNK, SM_I + h:SM_I + h + 1], (CHUNK, CHUNK))
                bj = bml_t[SM_F + h:SM_F + h + 1, r0:r0 + CHUNK]
                ij = pre_t[SM_I + h:SM_I + h + 1, r0:r0 + CHUNK]
                s_qk = _dot_nt(jnp.where(msk, q2, 0.0).astype(BF16), k2b)
                log_d = jnp.where(causal, bt - bj + ij, NEG)
                inter_log = bt + mprev[e]
                m_row = jnp.maximum(inter_log, jnp.max(log_d, axis=1, keepdims=True))
                d_mat = jnp.exp(log_d - m_row)
                inter_w = jnp.exp(inter_log - m_row)
                sc = s_qk * d_mat
                num_intra = _dot(sc.astype(BF16), v2b)
                den_inter = jnp.sum(jnp.where(msk, qn, 0.0), axis=1, keepdims=True)
                den = jnp.sum(sc, axis=1, keepdims=True) + inter_w[:, 0:1] * den_inter
                den = jnp.maximum(jnp.abs(den), jnp.exp(-m_row[:, 0:1]))
                iw128 = jnp.broadcast_to(inter_w[:, 0:1], (CHUNK, LANES))
                nums.append(num_intra + iw128 * num_inter)
                dens.append(den)
                g_row = bt[CHUNK - 1:CHUNK, :]
                a_row = g_row - bj + ij
                m_loc = jnp.max(a_row, axis=1, keepdims=True)
                m_new = jnp.maximum(g_row + mprev[e], m_loc)
                wcols.append(jnp.exp(g_row - bt + it - m_new))
                solds.append(jnp.exp(g_row + mprev[e] - m_new))
                mnews.append(m_new)
            out = jnp.where(m0, nums[0] / dens[0], nums[1] / dens[1])
            y_ref[r0:r0 + CHUNK, ATT_W + c0:ATT_W + c0 + LANES] = out
            wpair = jnp.where(m0, jnp.broadcast_to(wcols[0][:, 0:1], (CHUNK, LANES)),
                              jnp.broadcast_to(wcols[1][:, 0:1], (CHUNK, LANES)))
            kw = k2 * wpair
            upd = _dot_tn(kw.astype(BF16), v2b)
            sold_col = jnp.where(prow_first, solds[0][:, 0:1], solds[1][:, 0:1])
            ct = jnp.where(pair_diag, sold_col * ct + upd, 0.0)
            sold_row = jnp.where(m0, jnp.broadcast_to(solds[0][:, 0:1], (1, LANES)),
                                 jnp.broadcast_to(solds[1][:, 0:1], (1, LANES)))
            nvec = sold_row * nvec + jnp.sum(kw, axis=0, keepdims=True)
            mprev = mnews
        ml_c[p] = ct
        ml_n[p] = jnp.broadcast_to(nvec, (8, LANES))
        for e in range(2):
            ml_m[2 * p + e] = jnp.broadcast_to(mprev[e][:, 0:1], (8, LANES))

    for p in range(ML_W // LANES):
        c0 = p * LANES
        hcur = y_ref[:, ATT_W + c0:ATT_W + c0 + LANES]
        sq = hcur * hcur
        ms0 = jnp.sum(jnp.where(m0, sq, 0.0), axis=1, keepdims=True)
        ms1 = jnp.sum(jnp.where(m0, 0.0, sq), axis=1, keepdims=True)
        ms = jnp.where(m0, ms0, ms1) * (1.0 / HEAD_DIM)
        hnorm = hcur * lax.rsqrt(ms + EPS) * mlg_ref[:, c0:c0 + LANES]
        y_ref[:, ATT_W + c0:ATT_W + c0 + LANES] = (
            jax.nn.sigmoid(z_ref[:, MO + c0:MO + c0 + LANES]) * hnorm
            * _silu(z_ref[:, MG + c0:MG + c0 + LANES]))

    grow = lax.broadcasted_iota(jnp.int32, (LANES, GLA_VW), 0)
    gcol = lax.broadcasted_iota(jnp.int32, (LANES, GLA_VW), 1)
    gla_diag = (grow >> 5) == (gcol >> 6)
    s_all = gla_s[...]
    yc0 = ATT_W + ML_W
    for c in range(NCH):
        r0 = c * CHUNK
        bc = bgl[r0:r0 + CHUNK, :]
        bref = bc[CHUNK // 2 - 1:CHUNK // 2, :]
        btot = bc[CHUNK - 1:CHUNK, :]
        gq = z_ref[r0:r0 + CHUNK, GQ:GQ + GLA_KW] * (GLA_DK ** -0.5)
        gk = z_ref[r0:r0 + CHUNK, GK:GK + GLA_KW]
        gvb = z_ref[r0:r0 + CHUNK, GV:GV + GLA_VW].astype(BF16)
        qe = gq * jnp.exp(bc - bref)
        keb = (gk * jnp.exp(bref - bc)).astype(BF16)
        o = _dot((gq * jnp.exp(bc)).astype(BF16), s_all.astype(BF16))
        intra = []
        for hp in range(2):
            oh = []
            for e in range(2):
                h = 2 * hp + e
                hm = (lane >> 5) == h
                a = _dot_nt(jnp.where(hm, qe, 0.0).astype(BF16), keb)
                a = jnp.where(causal, a, 0.0)
                oh.append(_dot(a.astype(BF16), gvb[:, hp * LANES:(hp + 1) * LANES]))
            intra.append(jnp.where(m0, oh[0], oh[1]))
        y_ref[r0:r0 + CHUNK, yc0:yc0 + GLA_VW] = o + jnp.concatenate(intra, axis=1)
        kd = (gk * jnp.exp(btot - bc)).astype(BF16)
        upd = _dot_tn(kd, gvb)
        dec_col = jnp.exp(bgl_t[:, r0 + CHUNK - 1:r0 + CHUNK])
        s_all = jnp.where(gla_diag, dec_col * s_all + upd, 0.0)
    gla_s[...] = s_all

    for hp in range(GLA_VW // LANES):
        c0 = hp * LANES
        hcur = y_ref[:, yc0 + c0:yc0 + c0 + LANES]
        sq = hcur * hcur
        ms0 = jnp.sum(jnp.where(m0, sq, 0.0), axis=1, keepdims=True)
        ms1 = jnp.sum(jnp.where(m0, 0.0, sq), axis=1, keepdims=True)
        ms = jnp.where(m0, ms0, ms1) * (1.0 / HEAD_DIM)
        hnorm = hcur * lax.rsqrt(ms + EPS) * glg_ref[:, c0:c0 + LANES]
        y_ref[:, yc0 + c0:yc0 + c0 + LANES] = hnorm * _silu(z_ref[:, GG + c0:GG + c0 + LANES])

    xn = x_ref[...] + _dot(y_ref[...].astype(BF16), wout_ref[...])
    if final_norm:
        xn = xn * lax.rsqrt(jnp.sum(xn * xn, axis=1, keepdims=True) * (1.0 / D_MODEL) + EPS) * fg_ref[...]
    o_ref[...] = xn


def _full_spec(shape):
    nd = len(shape)
    return pl.BlockSpec(shape, lambda b, i, _nd=nd: (0,) * _nd)


def _layer_call(x, prm, final_g, final_norm):
    bsz, seq, _ = x.shape
    args = (x, prm["ng"], prm["win"], prm["gb"], prm["conv"], prm["wal"], prm["bal"], prm["gp"],
            prm["mlg"], prm["glg"], prm["wout"], final_g)
    x_spec = pl.BlockSpec((None, TILE, D_MODEL), lambda b, i: (b, i, 0))
    in_specs = [x_spec] + [_full_spec(a.shape) for a in args[1:]]
    scratch = [
        pltpu.VMEM((TILE, D_INP), F32),
        pltpu.VMEM((KEYS, ATT_W), BF16),
        pltpu.VMEM((KEYS, ATT_W), BF16),
        pltpu.VMEM((ML_HEADS, TILE, KEYS), F32),
        pltpu.VMEM((TILE + 16, 2 * ML_W), F32),
        pltpu.VMEM((ML_W // LANES, LANES, LANES), F32),
        pltpu.VMEM((ML_W // LANES, 8, LANES), F32),
        pltpu.VMEM((ML_HEADS, 8, LANES), F32),
        pltpu.VMEM((LANES, GLA_VW), F32),
        pltpu.VMEM((TILE, D_MODEL), F32),
    ]
    return pl.pallas_call(
        functools.partial(_layer_kernel, final_norm=final_norm),
        out_shape=jax.ShapeDtypeStruct(x.shape, x.dtype),
        grid=(bsz, seq // TILE),
        in_specs=in_specs,
        out_specs=x_spec,
        scratch_shapes=scratch,
        compiler_params=pltpu.CompilerParams(
            dimension_semantics=("arbitrary", "arbitrary"),
            vmem_limit_bytes=VMEM_LIMIT),
        name="hybrid_layer_final" if final_norm else "hybrid_layer",
    )(*args)


def _prep_layer(norm_g, w_in_t, b_gates, conv_w, w_alpha, b_alpha, rel_bias, ml_norm_g, gla_norm_g, w_out):
    pad = jnp.zeros((D_INP - SM - GLA_RANK - 2 * ML_HEADS, D_MODEL), w_in_t.dtype)
    win = jnp.concatenate([w_in_t[:3072], w_in_t[3084:3980], w_in_t[3996:4252],
                           w_in_t[3980:3996], w_in_t[3072:3084], pad], axis=0).astype(BF16)
    gb = jnp.zeros((1, LANES), F32).at[0, SM_I:SM_I + 2 * ML_HEADS].set(b_gates)
    wal = jnp.zeros((LANES, GLA_KW), F32).at[:GLA_RANK, :].set(w_alpha).astype(BF16)
    nh = rel_bias.shape[0]
    gp = jnp.concatenate([
        jnp.broadcast_to(rel_bias[:, 2 * REL_CLIP:], (nh, KEYS - REL_CLIP + 1)),
        rel_bias[:, 2 * REL_CLIP - 1:0:-1],
        jnp.broadcast_to(rel_bias[:, :1], (nh, ROLL_W - KEYS - REL_CLIP)),
    ], axis=1)
    return dict(ng=norm_g.reshape(1, D_MODEL), win=win, gb=gb, conv=conv_w, wal=wal,
                bal=b_alpha.reshape(1, GLA_KW), gp=gp, mlg=ml_norm_g.reshape(1, ML_W),
                glg=gla_norm_g.reshape(1, GLA_VW), wout=w_out.astype(BF16))


def kernel(x, norm_g, w_in, b_gates, conv_w, w_alpha, b_alpha, rel_bias, ml_norm_g, gla_norm_g, w_out, final_g):
    depth = norm_g.shape[0]
    fg = final_g.reshape(1, D_MODEL)
    w_in_t = jnp.transpose(w_in, (2, 0, 1))
    for l in range(depth):
        prm = _prep_layer(norm_g[l], w_in_t[:, l, :], b_gates[l], conv_w[l], w_alpha[l], b_alpha[l], rel_bias[l],
                          ml_norm_g[l], gla_norm_g[l], w_out[l])
        x = _layer_call(x, prm, fg, final_norm=(l == depth - 1))
    return x
```

```python
import functools

import jax
import jax.numpy as jnp
from jax import lax
from jax.experimental import pallas as pl
from jax.experimental.pallas import tpu as pltpu

F32 = jnp.float32
BF16 = jnp.bfloat16

D_MODEL = 1024
CHUNK = 64
HEAD_DIM = 64
ATT_W = 384
ML_W = 384
GLA_KW = 128
GLA_VW = 256
GLA_DK = 32
GLA_RANK = 16
GLA_TAU = 16.0
ML_HEADS = 6
PAST_CHUNKS = 8
REL_CLIP = 128
CONV_W = 4
EPS = 1e-6
NEG = -1e30

LANES = 128
TILE = 256
NCH = TILE // CHUNK
HIST = PAST_CHUNKS * CHUNK
KEYS = HIST + TILE
BAND = (PAST_CHUNKS + 1) * CHUNK
ROLL_W = 1024

AQ, AK, AV, AG = 0, 384, 768, 1152
MQ, MK, MV, MO, MG = 1536, 1920, 2304, 2688, 3072
GQ, GK, GV, GG = 3456, 3584, 3712, 3968
SM = 4224
D_INP = 4352
SM_I = GLA_RANK
SM_F = GLA_RANK + ML_HEADS

VMEM_LIMIT = 56 * 1024 * 1024


def _log_sigmoid(x):
    return jnp.minimum(x, 0.0) - jnp.log1p(jnp.exp(-jnp.abs(x)))


def _silu(x):
    return x * jax.nn.sigmoid(x)


def _dot_nt(a, b):
    return lax.dot_general(a, b, (((1,), (1,)), ((), ())), preferred_element_type=F32)


def _dot_tn(a, b):
    return lax.dot_general(a, b, (((0,), (0,)), ((), ())), preferred_element_type=F32)


def _dot(a, b):
    return jnp.dot(a, b, preferred_element_type=F32)


def _layer_kernel(x_ref, ng_ref, win_ref, gb_ref, conv_ref, wal_ref, bal_ref, gp_ref,
                  mlg_ref, glg_ref, wout_ref, fg_ref,
                  o_ref,
                  z_ref, kbuf, vbuf, bias_ref, cbuf, ml_c, ml_n, ml_m, gla_s, y_ref, sc_ref,
                  *, final_norm):
    b = pl.program_id(0)
    i = pl.program_id(1)

    lane = lax.broadcasted_iota(jnp.int32, (1, LANES), 1)
    m0 = lane < HEAD_DIM

    @pl.when((b == 0) & (i == 0))
    def _build_bias():
        r = lax.broadcasted_iota(jnp.int32, (TILE, KEYS), 0)
        m = lax.broadcasted_iota(jnp.int32, (TILE, KEYS), 1)
        jj = m - ((r >> 6) << 6)
        in_band = (jj >= 0) & (jj < BAND)
        for h in range(ML_HEADS):
            row = jnp.broadcast_to(gp_ref[h:h + 1, :], (TILE, ROLL_W))
            rolled = pltpu.roll(row, KEYS, 1, stride=1, stride_axis=0)
            bias_ref[h] = jnp.where(in_band, rolled[:, :KEYS], NEG)

    @pl.when(i == 0)
    def _reset():
        kbuf[...] = jnp.zeros(kbuf.shape, kbuf.dtype)
        vbuf[...] = jnp.zeros(vbuf.shape, vbuf.dtype)
        cbuf[0:8, :] = jnp.zeros((8, cbuf.shape[1]), F32)
        ml_c[...] = jnp.zeros(ml_c.shape, F32)
        ml_n[...] = jnp.zeros(ml_n.shape, F32)
        ml_m[...] = jnp.zeros(ml_m.shape, F32)
        gla_s[...] = jnp.zeros(gla_s.shape, F32)

    x = x_ref[...]
    hn = x * lax.rsqrt(jnp.sum(x * x, axis=1, keepdims=True) * (1.0 / D_MODEL) + EPS) * ng_ref[...]
    hn_b = hn.astype(BF16)
    for n0 in range(0, D_INP, 256):
        z_ref[:, n0:n0 + 256] = _dot_nt(hn_b, win_ref[n0:n0 + 256, :])

    kbuf[0:HIST, :] = kbuf[TILE:KEYS, :]
    vbuf[0:HIST, :] = vbuf[TILE:KEYS, :]
    kbuf[HIST:KEYS, :] = z_ref[:, AK:AK + ATT_W].astype(BF16)
    vbuf[HIST:KEYS, :] = z_ref[:, AV:AV + ATT_W].astype(BF16)

    kcol = lax.broadcasted_iota(jnp.int32, (1, KEYS), 1)
    pen = jnp.where(kcol + (i * TILE - HIST) >= 0, 0.0, NEG)

    for p in range(ATT_W // LANES):
        c0 = p * LANES
        q2 = z_ref[:, AQ + c0:AQ + c0 + LANES] * (HEAD_DIM ** -0.5)
        k2 = kbuf[:, c0:c0 + LANES]
        v2 = vbuf[:, c0:c0 + LANES]
        outs = []
        for e in range(2):
            msk = m0 if e == 0 else jnp.logical_not(m0)
            qm = jnp.where(msk, q2, 0.0).astype(BF16)
            s = _dot_nt(qm, k2) + bias_ref[2 * p + e] + pen
            mx = jnp.max(s, axis=1, keepdims=True)
            pe = jnp.exp(s - mx)
            l = jnp.sum(pe, axis=1, keepdims=True)
            outs.append(_dot(pe.astype(BF16), v2) / l)
        att = jnp.where(m0, outs[0], outs[1])
        y_ref[:, c0:c0 + LANES] = att * _silu(z_ref[:, AG + c0:AG + c0 + LANES])

    cbuf[8:8 + TILE, :] = z_ref[:, MQ:MQ + 2 * ML_W]
    for c0 in range(0, 2 * ML_W, LANES):
        acc = cbuf[8:8 + TILE, c0:c0 + LANES] * conv_ref[CONV_W - 1:CONV_W, c0:c0 + LANES]
        for sft in range(1, CONV_W):
            acc = acc + cbuf[8 - sft:8 - sft + TILE, c0:c0 + LANES] * \
                conv_ref[CONV_W - 1 - sft:CONV_W - sft, c0:c0 + LANES]
        act = _silu(acc)
        if c0 >= ML_W:
            act = act * (HEAD_DIM ** -0.5)
        z_ref[:, MQ + c0:MQ + c0 + LANES] = act
    cbuf[0:8, :] = cbuf[TILE:TILE + 8, :]

    small = z_ref[:, SM:SM + LANES]
    pre = small + gb_ref[...]
    lf = _log_sigmoid(pre)
    la = _log_sigmoid(_dot(small.astype(BF16), wal_ref[...]) + bal_ref[...]) * (1.0 / GLA_TAU)
    tr = lax.broadcasted_iota(jnp.int32, (TILE, TILE), 0)
    tc = lax.broadcasted_iota(jnp.int32, (TILE, TILE), 1)
    tri_tile = jnp.where(tr >= tc, 1.0, 0.0).astype(F32)
    tri_chunk = jnp.where((tr >= tc) & ((tr >> 6) == (tc >> 6)), 1.0, 0.0).astype(F32)
    bml = jnp.dot(tri_tile, lf, preferred_element_type=F32, precision=lax.Precision.HIGHEST)
    bgl = jnp.dot(tri_chunk, la, preferred_element_type=F32, precision=lax.Precision.HIGHEST)
    pre_t = pre.T
    bml_t = bml.T
    bgl_t = bgl.T

    prow = lax.broadcasted_iota(jnp.int32, (LANES, LANES), 0)
    pcol = lax.broadcasted_iota(jnp.int32, (LANES, LANES), 1)
    pair_diag = (prow < HEAD_DIM) == (pcol < HEAD_DIM)
    prow_first = prow[:, 0:1] < HEAD_DIM

    for p in range(ML_W // LANES):
        c0 = p * LANES
        q2 = z_ref[:, MQ + c0:MQ + c0 + LANES]
        k2 = z_ref[:, MK + c0:MK + c0 + LANES]
        v2 = z_ref[:, MV + c0:MV + c0 + LANES]
        q2b = q2.astype(BF16)
        v2b = v2.astype(BF16)
        kk = jnp.concatenate([jnp.where(m0, k2, 0.0), jnp.where(m0, 0.0, k2)], axis=0).astype(BF16)
        vv = jnp.concatenate([jnp.where(m0, v2, 0.0), jnp.where(m0, 0.0, v2)], axis=0).astype(BF16)
        s_pair = _dot_nt(q2b, kk)
        ct = ml_c[p]
        nvec = ml_n[p][0:1, :]
        num_inter = _dot(q2b, ct.astype(BF16))
        qn = q2 * nvec
        iws, dens, wcols, solds = [], [], [], []
        for e in range(2):
            h = 2 * p + e
            msk = m0 if e == 0 else jnp.logical_not(m0)
            mprev = ml_m[h][0:1, :]
            bt_all = jnp.broadcast_to(bml[:, SM_F + h:SM_F + h + 1], (TILE, LANES))
            it_all = jnp.broadcast_to(pre[:, SM_I + h:SM_I + h + 1], (TILE, LANES))
            bj = bml_t[SM_F + h:SM_F + h + 1, :]
            ij = pre_t[SM_I + h:SM_I + h + 1, :]
            den_inter = jnp.sum(jnp.where(msk, qn, 0.0), axis=1, keepdims=True)
            iw_blocks, den_blocks = [], []
            for rb in range(NCH):
                r0 = rb * CHUNK
                ncol = LANES * ((r0 + CHUNK + LANES - 1) // LANES)
                rr = lax.broadcasted_iota(jnp.int32, (CHUNK, ncol), 0) + r0
                kc = lax.broadcasted_iota(jnp.int32, (CHUNK, ncol), 1)
                btc = bt_all[r0:r0 + CHUNK, 0:1]
                log_d = jnp.where(rr >= kc, btc - bj[:, :ncol] + ij[:, :ncol], NEG)
                inter_log = btc + mprev[:, 0:1]
                m_row = jnp.maximum(inter_log, jnp.max(log_d, axis=1, keepdims=True))
                sc = s_pair[r0:r0 + CHUNK, e * TILE:e * TILE + ncol] * jnp.exp(log_d - m_row)
                sc_ref[r0:r0 + CHUNK, e * TILE:e * TILE + ncol] = sc.astype(BF16)
                if ncol < TILE:
                    sc_ref[r0:r0 + CHUNK, e * TILE + ncol:(e + 1) * TILE] = jnp.zeros(
                        (CHUNK, TILE - ncol), BF16)
                inter_w = jnp.exp(inter_log - m_row)
                den = jnp.sum(sc, axis=1, keepdims=True) + inter_w * den_inter[r0:r0 + CHUNK]
                den_blocks.append(jnp.maximum(jnp.abs(den), jnp.exp(-m_row)))
                iw_blocks.append(inter_w)
            iws.append(jnp.concatenate(iw_blocks, axis=0))
            dens.append(jnp.concatenate(den_blocks, axis=0))
            g_row = bt_all[TILE - 1:TILE, :]
            a_row = g_row[:, 0:1] - bj + ij
            m_new = jnp.maximum(g_row + mprev, jnp.max(a_row, axis=1, keepdims=True))
            wcols.append(jnp.exp(g_row - bt_all + it_all - m_new))
            solds.append(jnp.exp(g_row + mprev - m_new))
            ml_m[h] = jnp.broadcast_to(m_new, (8, LANES))
        num = _dot(sc_ref[...], vv) + jnp.where(m0, iws[0], iws[1]) * num_inter
        y_ref[:, ATT_W + c0:ATT_W + c0 + LANES] = num / jnp.where(m0, dens[0], dens[1])
        kw = k2 * jnp.where(m0, wcols[0], wcols[1])
        upd = _dot_tn(kw.astype(BF16), v2b)
        sold_col = jnp.where(prow_first, solds[0][:, 0:1], solds[1][:, 0:1])
        ml_c[p] = jnp.where(pair_diag, sold_col * ct + upd, 0.0)
        nvec = jnp.where(m0, solds[0], solds[1]) * nvec + jnp.sum(kw, axis=0, keepdims=True)
        ml_n[p] = jnp.broadcast_to(nvec, (8, LANES))

    for p in range(ML_W // LANES):
        c0 = p * LANES
        hcur = y_ref[:, ATT_W + c0:ATT_W + c0 + LANES]
        sq = hcur * hcur
        ms0 = jnp.sum(jnp.where(m0, sq, 0.0), axis=1, keepdims=True)
        ms1 = jnp.sum(jnp.where(m0, 0.0, sq), axis=1, keepdims=True)
        ms = jnp.where(m0, ms0, ms1) * (1.0 / HEAD_DIM)
        hnorm = hcur * lax.rsqrt(ms + EPS) * mlg_ref[:, c0:c0 + LANES]
        y_ref[:, ATT_W + c0:ATT_W + c0 + LANES] = (
            jax.nn.sigmoid(z_ref[:, MO + c0:MO + c0 + LANES]) * hnorm
            * _silu(z_ref[:, MG + c0:MG + c0 + LANES]))

    grow = lax.broadcasted_iota(jnp.int32, (LANES, GLA_VW), 0)
    gcol = lax.broadcasted_iota(jnp.int32, (LANES, GLA_VW), 1)
    gla_diag = (grow >> 5) == (gcol >> 6)
    khead = lane >> 5
    vhead = lax.broadcasted_iota(jnp.int32, (1, GLA_VW), 1) >> 6
    arow = lax.broadcasted_iota(jnp.int32, (CHUNK, GLA_VW), 0)
    acol = lax.broadcasted_iota(jnp.int32, (CHUNK, GLA_VW), 1)
    causal4 = arow >= (acol & (CHUNK - 1))
    n_gh = GLA_KW // GLA_DK
    s_all = gla_s[...]
    yc0 = ATT_W + ML_W
    for c in range(NCH):
        r0 = c * CHUNK
        bc = bgl[r0:r0 + CHUNK, :]
        bref = bc[CHUNK // 2 - 1:CHUNK // 2, :]
        btot = bc[CHUNK - 1:CHUNK, :]
        gq = z_ref[r0:r0 + CHUNK, GQ:GQ + GLA_KW] * (GLA_DK ** -0.5)
        gk = z_ref[r0:r0 + CHUNK, GK:GK + GLA_KW]
        gv = z_ref[r0:r0 + CHUNK, GV:GV + GLA_VW]
        qe = (gq * jnp.exp(bc - bref)).astype(BF16)
        ke = gk * jnp.exp(bref - bc)
        kek = jnp.concatenate([jnp.where(khead == h, ke, 0.0) for h in range(n_gh)],
                              axis=0).astype(BF16)
        vvg = jnp.concatenate([jnp.where(vhead == h, gv, 0.0) for h in range(n_gh)],
                              axis=0).astype(BF16)
        a = jnp.where(causal4, _dot_nt(qe, kek), 0.0)
        o = _dot((gq * jnp.exp(bc)).astype(BF16), s_all.astype(BF16)) + _dot(a.astype(BF16), vvg)
        y_ref[r0:r0 + CHUNK, yc0:yc0 + GLA_VW] = o
        kd = (gk * jnp.exp(btot - bc)).astype(BF16)
        upd = _dot_tn(kd, gv.astype(BF16))
        dec_col = jnp.exp(bgl_t[:, r0 + CHUNK - 1:r0 + CHUNK])
        s_all = jnp.where(gla_diag, dec_col * s_all + upd, 0.0)
    gla_s[...] = s_all

    for hp in range(GLA_VW // LANES):
        c0 = hp * LANES
        hcur = y_ref[:, yc0 + c0:yc0 + c0 + LANES]
        sq = hcur * hcur
        ms0 = jnp.sum(jnp.where(m0, sq, 0.0), axis=1, keepdims=True)
        ms1 = jnp.sum(jnp.where(m0, 0.0, sq), axis=1, keepdims=True)
        ms = jnp.where(m0, ms0, ms1) * (1.0 / HEAD_DIM)
        hnorm = hcur * lax.rsqrt(ms + EPS) * glg_ref[:, c0:c0 + LANES]
        y_ref[:, yc0 + c0:yc0 + c0 + LANES] = hnorm * _silu(z_ref[:, GG + c0:GG + c0 + LANES])

    xn = x_ref[...] + _dot(y_ref[...].astype(BF16), wout_ref[...])
    if final_norm:
        xn = xn * lax.rsqrt(jnp.sum(xn * xn, axis=1, keepdims=True) * (1.0 / D_MODEL) + EPS) * fg_ref[...]
    o_ref[...] = xn


def _full_spec(shape):
    nd = len(shape)
    return pl.BlockSpec(shape, lambda b, i, _nd=nd: (0,) * _nd)


def _layer_call(x, prm, final_g, final_norm):
    bsz, seq, _ = x.shape
    args = (x, prm["ng"], prm["win"], prm["gb"], prm["conv"], prm["wal"], prm["bal"], prm["gp"],
            prm["mlg"], prm["glg"], prm["wout"], final_g)
    x_spec = pl.BlockSpec((None, TILE, D_MODEL), lambda b, i: (b, i, 0))
    in_specs = [x_spec] + [_full_spec(a.shape) for a in args[1:]]
    scratch = [
        pltpu.VMEM((TILE, D_INP), F32),
        pltpu.VMEM((KEYS, ATT_W), BF16),
        pltpu.VMEM((KEYS, ATT_W), BF16),
        pltpu.VMEM((ML_HEADS, TILE, KEYS), F32),
        pltpu.VMEM((TILE + 16, 2 * ML_W), F32),
        pltpu.VMEM((ML_W // LANES, LANES, LANES), F32),
        pltpu.VMEM((ML_W // LANES, 8, LANES), F32),
        pltpu.VMEM((ML_HEADS, 8, LANES), F32),
        pltpu.VMEM((LANES, GLA_VW), F32),
        pltpu.VMEM((TILE, D_MODEL), F32),
        pltpu.VMEM((TILE, 2 * TILE), BF16),
    ]
    return pl.pallas_call(
        functools.partial(_layer_kernel, final_norm=final_norm),
        out_shape=jax.ShapeDtypeStruct(x.shape, x.dtype),
        grid=(bsz, seq // TILE),
        in_specs=in_specs,
        out_specs=x_spec,
        scratch_shapes=scratch,
        compiler_params=pltpu.CompilerParams(
            dimension_semantics=("arbitrary", "arbitrary"),
            vmem_limit_bytes=VMEM_LIMIT),
        name="hybrid_layer_final" if final_norm else "hybrid_layer",
    )(*args)


def _prep_layer(norm_g, w_in_t, b_gates, conv_w, w_alpha, b_alpha, rel_bias, ml_norm_g, gla_norm_g, w_out):
    pad = jnp.zeros((D_INP - SM - GLA_RANK - 2 * ML_HEADS, D_MODEL), w_in_t.dtype)
    win = jnp.concatenate([w_in_t[:3072], w_in_t[3084:3980], w_in_t[3996:4252],
                           w_in_t[3980:3996], w_in_t[3072:3084], pad], axis=0).astype(BF16)
    gb = jnp.zeros((1, LANES), F32).at[0, SM_I:SM_I + 2 * ML_HEADS].set(b_gates)
    wal = jnp.zeros((LANES, GLA_KW), F32).at[:GLA_RANK, :].set(w_alpha).astype(BF16)
    nh = rel_bias.shape[0]
    gp = jnp.concatenate([
        jnp.broadcast_to(rel_bias[:, 2 * REL_CLIP:], (nh, KEYS - REL_CLIP + 1)),
        rel_bias[:, 2 * REL_CLIP - 1:0:-1],
        jnp.broadcast_to(rel_bias[:, :1], (nh, ROLL_W - KEYS - REL_CLIP)),
    ], axis=1)
    return dict(ng=norm_g.reshape(1, D_MODEL), win=win, gb=gb, conv=conv_w, wal=wal,
                bal=b_alpha.reshape(1, GLA_KW), gp=gp, mlg=ml_norm_g.reshape(1, ML_W),
                glg=gla_norm_g.reshape(1, GLA_VW), wout=w_out.astype(BF16))


def kernel(x, norm_g, w_in, b_gates, conv_w, w_alpha, b_alpha, rel_bias, ml_norm_g, gla_norm_g, w_out, final_g):
    depth = norm_g.shape[0]
    fg = final_g.reshape(1, D_MODEL)
    w_in_t = jnp.transpose(w_in, (2, 0, 1))
    for l in range(depth):
        prm = _prep_layer(norm_g[l], w_in_t[:, l, :], b_gates[l], conv_w[l], w_alpha[l], b_alpha[l], rel_bias[l],
                          ml_norm_g[l], gla_norm_g[l], w_out[l])
        x = _layer_call(x, prm, fg, final_norm=(l == depth - 1))
    return x
```

```python
import functools

import jax
import jax.numpy as jnp
from jax import lax
from jax.experimental import pallas as pl
from jax.experimental.pallas import tpu as pltpu

F32 = jnp.float32
BF16 = jnp.bfloat16

D_MODEL = 1024
CHUNK = 64
HEAD_DIM = 64
ATT_W = 384
ML_W = 384
GLA_KW = 128
GLA_VW = 256
GLA_DK = 32
GLA_RANK = 16
GLA_TAU = 16.0
ML_HEADS = 6
PAST_CHUNKS = 8
REL_CLIP = 128
CONV_W = 4
EPS = 1e-6
NEG = -1e30

LANES = 128
TILE = 256
NCH = TILE // CHUNK
HIST = PAST_CHUNKS * CHUNK
KEYS = HIST + TILE
BAND = (PAST_CHUNKS + 1) * CHUNK
ROLL_W = 1024
N_PAIRS = ML_W // LANES

AQ, AK, AV, AG = 0, 384, 768, 1152
MQ, MK, MV, MO, MG = 1536, 1920, 2304, 2688, 3072
GQ, GK, GV, GG = 3456, 3584, 3712, 3968
SM = 4224
D_INP = 4352
IP_BLK = 256
SM_I = GLA_RANK
SM_F = GLA_RANK + ML_HEADS

VMEM_LIMIT = 56 * 1024 * 1024


def _log_sigmoid(x):
    return jnp.minimum(x, 0.0) - jnp.log1p(jnp.exp(-jnp.abs(x)))


def _silu(x):
    return x * jax.nn.sigmoid(x)


def _dot_nt(a, b):
    return lax.dot_general(a, b, (((1,), (1,)), ((), ())), preferred_element_type=F32)


def _dot_tn(a, b):
    return lax.dot_general(a, b, (((0,), (0,)), ((), ())), preferred_element_type=F32)


def _dot(a, b):
    return jnp.dot(a, b, preferred_element_type=F32)


def _pair_rms(hcur, gain, m0):
    sq = hcur * hcur
    ms0 = jnp.sum(jnp.where(m0, sq, 0.0), axis=1, keepdims=True)
    ms1 = jnp.sum(jnp.where(m0, 0.0, sq), axis=1, keepdims=True)
    ms = jnp.where(m0, ms0, ms1) * (1.0 / HEAD_DIM)
    return hcur * lax.rsqrt(ms + EPS) * gain


def _step(i, zc, zn, xn_ref, xc_ref, ng_ref, win_ref, gb_ref, conv_ref, wal_ref, bal_ref,
          mlg_ref, glg_ref, wout_ref, fg_ref, o_ref,
          kbuf, vbuf, bias_ref, cbuf, ml_c, ml_n, ml_m, gla_s, y_ref, sc_ref, *, final_norm):
    lane = lax.broadcasted_iota(jnp.int32, (1, LANES), 1)
    m0 = lane < HEAD_DIM
    m1 = jnp.logical_not(m0)

    xn = xn_ref[...]
    hn = xn * lax.rsqrt(jnp.sum(xn * xn, axis=1, keepdims=True) * (1.0 / D_MODEL) + EPS) * ng_ref[...]
    hn_b = hn.astype(BF16)
    ip_next = [0]

    def ip(count):
        for _ in range(count):
            n0 = ip_next[0] * IP_BLK
            if n0 < D_INP:
                zn[:, n0:n0 + IP_BLK] = _dot_nt(hn_b, win_ref[n0:n0 + IP_BLK, :])
                ip_next[0] += 1

    kbuf[0:HIST, :] = kbuf[TILE:KEYS, :]
    vbuf[0:HIST, :] = vbuf[TILE:KEYS, :]
    kbuf[HIST:KEYS, :] = zc[:, AK:AK + ATT_W].astype(BF16)
    vbuf[HIST:KEYS, :] = zc[:, AV:AV + ATT_W].astype(BF16)

    kcol = lax.broadcasted_iota(jnp.int32, (1, KEYS), 1)
    pen = jnp.where(kcol + (i * TILE - HIST) >= 0, 0.0, NEG)

    def att_scores(h):
        c0 = (h // 2) * LANES
        q2 = zc[:, AQ + c0:AQ + c0 + LANES] * (HEAD_DIM ** -0.5)
        qm = jnp.where(m0 if h % 2 == 0 else m1, q2, 0.0).astype(BF16)
        return _dot_nt(qm, kbuf[:, c0:c0 + LANES]) + bias_ref[h] + pen

    def att_out(h, s):
        c0 = (h // 2) * LANES
        mx = jnp.max(s, axis=1, keepdims=True)
        pe = jnp.exp(s - mx)
        l = jnp.sum(pe, axis=1, keepdims=True)
        return _dot(pe.astype(BF16), vbuf[:, c0:c0 + LANES]) / l

    s_cur = att_scores(0)
    outs = []
    for h in range(ML_HEADS):
        s_nxt = att_scores(h + 1) if h + 1 < ML_HEADS else None
        ip(1)
        outs.append(att_out(h, s_cur))
        s_cur = s_nxt
        if h % 2 == 1:
            c0 = (h // 2) * LANES
            att = jnp.where(m0, outs[h - 1], outs[h])
            y_ref[:, c0:c0 + LANES] = att * _silu(zc[:, AG + c0:AG + c0 + LANES])
    ip(1)

    cbuf[8:8 + TILE, :] = zc[:, MQ:MQ + 2 * ML_W]
    for c0 in range(0, 2 * ML_W, LANES):
        acc = cbuf[8:8 + TILE, c0:c0 + LANES] * conv_ref[CONV_W - 1:CONV_W, c0:c0 + LANES]
        for sft in range(1, CONV_W):
            acc = acc + cbuf[8 - sft:8 - sft + TILE, c0:c0 + LANES] * \
                conv_ref[CONV_W - 1 - sft:CONV_W - sft, c0:c0 + LANES]
        act = _silu(acc)
        if c0 >= ML_W:
            act = act * (HEAD_DIM ** -0.5)
        zc[:, MQ + c0:MQ + c0 + LANES] = act
    cbuf[0:8, :] = cbuf[TILE:TILE + 8, :]

    small = zc[:, SM:SM + LANES]
    pre = small + gb_ref[...]
    lf = _log_sigmoid(pre)
    la = _log_sigmoid(_dot(small.astype(BF16), wal_ref[...]) + bal_ref[...]) * (1.0 / GLA_TAU)
    tr = lax.broadcasted_iota(jnp.int32, (TILE, TILE), 0)
    tc = lax.broadcasted_iota(jnp.int32, (TILE, TILE), 1)
    tri_tile = jnp.where(tr >= tc, 1.0, 0.0).astype(F32)
    tri_chunk = jnp.where((tr >= tc) & ((tr >> 6) == (tc >> 6)), 1.0, 0.0).astype(F32)
    bml = jnp.dot(tri_tile, lf, preferred_element_type=F32, precision=lax.Precision.HIGHEST)
    bgl = jnp.dot(tri_chunk, la, preferred_element_type=F32, precision=lax.Precision.HIGHEST)
    pre_t = pre.T
    bml_t = bml.T
    bgl_t = bgl.T
    ip(1)

    prow = lax.broadcasted_iota(jnp.int32, (LANES, LANES), 0)
    pcol = lax.broadcasted_iota(jnp.int32, (LANES, LANES), 1)
    pair_diag = (prow < HEAD_DIM) == (pcol < HEAD_DIM)
    prow_first = prow[:, 0:1] < HEAD_DIM

    ml_q, ml_k, ml_vb, ml_vv, ml_s, ml_ni, ml_ct = [], [], [], [], [], [], []
    for p in range(N_PAIRS):
        c0 = p * LANES
        q2 = zc[:, MQ + c0:MQ + c0 + LANES]
        k2 = zc[:, MK + c0:MK + c0 + LANES]
        v2 = zc[:, MV + c0:MV + c0 + LANES]
        q2b = q2.astype(BF16)
        kk = jnp.concatenate([jnp.where(m0, k2, 0.0), jnp.where(m0, 0.0, k2)], axis=0).astype(BF16)
        ct = ml_c[p]
        ml_q.append(q2)
        ml_k.append(k2)
        ml_vb.append(v2.astype(BF16))
        ml_vv.append(jnp.concatenate([jnp.where(m0, v2, 0.0), jnp.where(m0, 0.0, v2)],
                                     axis=0).astype(BF16))
        ml_ct.append(ct)
        ml_s.append(_dot_nt(q2b, kk))
        ml_ni.append(_dot(q2b, ct.astype(BF16)))
    ip(2)

    for p in range(N_PAIRS):
        c0 = p * LANES
        q2, k2, s_pair = ml_q[p], ml_k[p], ml_s[p]
        nvec = ml_n[p][0:1, :]
        qn = q2 * nvec
        iws, dens, wcols, solds = [], [], [], []
        for e in range(2):
            h = 2 * p + e
            mprev = ml_m[h][0:1, :]
            bt_all = jnp.broadcast_to(bml[:, SM_F + h:SM_F + h + 1], (TILE, LANES))
            it_all = jnp.broadcast_to(pre[:, SM_I + h:SM_I + h + 1], (TILE, LANES))
            bj = bml_t[SM_F + h:SM_F + h + 1, :]
            ij = pre_t[SM_I + h:SM_I + h + 1, :]
            den_inter = jnp.sum(jnp.where(m0 if e == 0 else m1, qn, 0.0), axis=1, keepdims=True)
            iw_blocks, den_blocks = [], []
            for rb in range(NCH):
                r0 = rb * CHUNK
                ncol = LANES * ((r0 + CHUNK + LANES - 1) // LANES)
                rr = lax.broadcasted_iota(jnp.int32, (CHUNK, ncol), 0) + r0
                kc = lax.broadcasted_iota(jnp.int32, (CHUNK, ncol), 1)
                btc = bt_all[r0:r0 + CHUNK, 0:1]
                log_d = jnp.where(rr >= kc, btc - bj[:, :ncol] + ij[:, :ncol], NEG)
                inter_log = btc + mprev[:, 0:1]
                m_row = jnp.maximum(inter_log, jnp.max(log_d, axis=1, keepdims=True))
                sc = s_pair[r0:r0 + CHUNK, e * TILE:e * TILE + ncol] * jnp.exp(log_d - m_row)
                sc_ref[p, r0:r0 + CHUNK, e * TILE:e * TILE + ncol] = sc.astype(BF16)
                if ncol < TILE:
                    sc_ref[p, r0:r0 + CHUNK, e * TILE + ncol:(e + 1) * TILE] = jnp.zeros(
                        (CHUNK, TILE - ncol), BF16)
                inter_w = jnp.exp(inter_log - m_row)
                den = jnp.sum(sc, axis=1, keepdims=True) + inter_w * den_inter[r0:r0 + CHUNK]
                den_blocks.append(jnp.maximum(jnp.abs(den), jnp.exp(-m_row)))
                iw_blocks.append(inter_w)
            iws.append(jnp.concatenate(iw_blocks, axis=0))
            dens.append(jnp.concatenate(den_blocks, axis=0))
            g_row = bt_all[TILE - 1:TILE, :]
            a_row = g_row[:, 0:1] - bj + ij
            m_new = jnp.maximum(g_row + mprev, jnp.max(a_row, axis=1, keepdims=True))
            wcols.append(jnp.exp(g_row - bt_all + it_all - m_new))
            solds.append(jnp.exp(g_row + mprev - m_new))
            ml_m[h] = jnp.broadcast_to(m_new, (8, LANES))
        num = _dot(sc_ref[p], ml_vv[p]) + jnp.where(m0, iws[0], iws[1]) * ml_ni[p]
        hcur = num / jnp.where(m0, dens[0], dens[1])
        y_ref[:, ATT_W + c0:ATT_W + c0 + LANES] = (
            jax.nn.sigmoid(zc[:, MO + c0:MO + c0 + LANES])
            * _pair_rms(hcur, mlg_ref[:, c0:c0 + LANES], m0)
            * _silu(zc[:, MG + c0:MG + c0 + LANES]))
        kw = k2 * jnp.where(m0, wcols[0], wcols[1])
        upd = _dot_tn(kw.astype(BF16), ml_vb[p])
        sold_col = jnp.where(prow_first, solds[0][:, 0:1], solds[1][:, 0:1])
        ml_c[p] = jnp.where(pair_diag, sold_col * ml_ct[p] + upd, 0.0)
        nvec = jnp.where(m0, solds[0], solds[1]) * nvec + jnp.sum(kw, axis=0, keepdims=True)
        ml_n[p] = jnp.broadcast_to(nvec, (8, LANES))
        ip(1)

    grow = lax.broadcasted_iota(jnp.int32, (LANES, GLA_VW), 0)
    gcol = lax.broadcasted_iota(jnp.int32, (LANES, GLA_VW), 1)
    gla_diag = (grow >> 5) == (gcol >> 6)
    khead = lane >> 5
    vhead = lax.broadcasted_iota(jnp.int32, (1, GLA_VW), 1) >> 6
    arow = lax.broadcasted_iota(jnp.int32, (CHUNK, GLA_VW), 0)
    acol = lax.broadcasted_iota(jnp.int32, (CHUNK, GLA_VW), 1)
    causal4 = arow >= (acol & (CHUNK - 1))
    n_gh = GLA_KW // GLA_DK
    yc0 = ATT_W + ML_W
    g_a, g_upd, g_qi, g_vvg, g_dec = [], [], [], [], []
    for c in range(NCH):
        r0 = c * CHUNK
        bc = bgl[r0:r0 + CHUNK, :]
        bref = bc[CHUNK // 2 - 1:CHUNK // 2, :]
        btot = bc[CHUNK - 1:CHUNK, :]
        gq = zc[r0:r0 + CHUNK, GQ:GQ + GLA_KW] * (GLA_DK ** -0.5)
        gk = zc[r0:r0 + CHUNK, GK:GK + GLA_KW]
        gv = zc[r0:r0 + CHUNK, GV:GV + GLA_VW]
        qe = (gq * jnp.exp(bc - bref)).astype(BF16)
        ke = gk * jnp.exp(bref - bc)
        kek = jnp.concatenate([jnp.where(khead == h, ke, 0.0) for h in range(n_gh)],
                              axis=0).astype(BF16)
        g_vvg.append(jnp.concatenate([jnp.where(vhead == h, gv, 0.0) for h in range(n_gh)],
                                     axis=0).astype(BF16))
        g_qi.append((gq * jnp.exp(bc)).astype(BF16))
        kd = (gk * jnp.exp(btot - bc)).astype(BF16)
        g_dec.append(jnp.exp(bgl_t[:, r0 + CHUNK - 1:r0 + CHUNK]))
        g_a.append(_dot_nt(qe, kek))
        g_upd.append(_dot_tn(kd, gv.astype(BF16)))
    ip(2)
    s_all = gla_s[...]
    for c in range(NCH):
        r0 = c * CHUNK
        a = jnp.where(causal4, g_a[c], 0.0)
        o = _dot(g_qi[c], s_all.astype(BF16)) + _dot(a.astype(BF16), g_vvg[c])
        y_ref[r0:r0 + CHUNK, yc0:yc0 + GLA_VW] = o
        s_all = jnp.where(gla_diag, g_dec[c] * s_all + g_upd[c], 0.0)
        ip(1)
    gla_s[...] = s_all

    for hp in range(GLA_VW // LANES):
        c0 = hp * LANES
        hcur = y_ref[:, yc0 + c0:yc0 + c0 + LANES]
        y_ref[:, yc0 + c0:yc0 + c0 + LANES] = (
            _pair_rms(hcur, glg_ref[:, c0:c0 + LANES], m0) * _silu(zc[:, GG + c0:GG + c0 + LANES]))
    ip(D_INP // IP_BLK)

    xo = xc_ref[...] + _dot(y_ref[...].astype(BF16), wout_ref[...])
    if final_norm:
        xo = xo * lax.rsqrt(jnp.sum(xo * xo, axis=1, keepdims=True) * (1.0 / D_MODEL) + EPS) * fg_ref[...]
    o_ref[...] = xo


def _layer_kernel(xn_ref, xc_ref, ng_ref, win_ref, gb_ref, conv_ref, wal_ref, bal_ref, gp_ref,
                  mlg_ref, glg_ref, wout_ref, fg_ref,
                  o_ref,
                  z_a, z_b, kbuf, vbuf, bias_ref, cbuf, ml_c, ml_n, ml_m, gla_s, y_ref, sc_ref,
                  *, final_norm, tiles_per_seq):
    g = pl.program_id(0)
    i = lax.rem(g + (tiles_per_seq - 1), tiles_per_seq)

    @pl.when(g == 0)
    def _build_bias():
        r = lax.broadcasted_iota(jnp.int32, (TILE, KEYS), 0)
        m = lax.broadcasted_iota(jnp.int32, (TILE, KEYS), 1)
        jj = m - ((r >> 6) << 6)
        in_band = (jj >= 0) & (jj < BAND)
        for h in range(ML_HEADS):
            row = jnp.broadcast_to(gp_ref[h:h + 1, :], (TILE, ROLL_W))
            rolled = pltpu.roll(row, KEYS, 1, stride=1, stride_axis=0)
            bias_ref[h] = jnp.where(in_band, rolled[:, :KEYS], NEG)
        z_b[...] = jnp.zeros(z_b.shape, F32)

    @pl.when((i == 0) | (g == 0))
    def _reset():
        kbuf[...] = jnp.zeros(kbuf.shape, kbuf.dtype)
        vbuf[...] = jnp.zeros(vbuf.shape, vbuf.dtype)
        cbuf[0:8, :] = jnp.zeros((8, cbuf.shape[1]), F32)
        ml_c[...] = jnp.zeros(ml_c.shape, F32)
        ml_n[...] = jnp.zeros(ml_n.shape, F32)
        ml_m[...] = jnp.zeros(ml_m.shape, F32)
        gla_s[...] = jnp.zeros(gla_s.shape, F32)

    step = functools.partial(
        _step, i, xn_ref=xn_ref, xc_ref=xc_ref, ng_ref=ng_ref, win_ref=win_ref, gb_ref=gb_ref,
        conv_ref=conv_ref, wal_ref=wal_ref, bal_ref=bal_ref, mlg_ref=mlg_ref, glg_ref=glg_ref,
        wout_ref=wout_ref, fg_ref=fg_ref, o_ref=o_ref, kbuf=kbuf, vbuf=vbuf, bias_ref=bias_ref,
        cbuf=cbuf, ml_c=ml_c, ml_n=ml_n, ml_m=ml_m, gla_s=gla_s, y_ref=y_ref, sc_ref=sc_ref,
        final_norm=final_norm)

    @pl.when(lax.rem(g, 2) == 0)
    def _even():
        step(z_b, z_a)

    @pl.when(lax.rem(g, 2) == 1)
    def _odd():
        step(z_a, z_b)


def _const_spec(shape):
    nd = len(shape)
    return pl.BlockSpec(shape, lambda g, _nd=nd: (0,) * _nd, pipeline_mode=pl.Buffered(1))


def _layer_call(x2d, prm, final_g, final_norm, tiles_per_seq):
    n_tiles = x2d.shape[0] // TILE
    args = (x2d, x2d, prm["ng"], prm["win"], prm["gb"], prm["conv"], prm["wal"], prm["bal"], prm["gp"],
            prm["mlg"], prm["glg"], prm["wout"], final_g)
    nxt_spec = pl.BlockSpec((TILE, D_MODEL), lambda g: (jnp.minimum(g, n_tiles - 1), 0))
    cur_spec = pl.BlockSpec((TILE, D_MODEL), lambda g: (jnp.maximum(g - 1, 0), 0))
    in_specs = [nxt_spec, cur_spec] + [_const_spec(a.shape) for a in args[2:]]
    scratch = [
        pltpu.VMEM((TILE, D_INP), F32),
        pltpu.VMEM((TILE, D_INP), F32),
        pltpu.VMEM((KEYS, ATT_W), BF16),
        pltpu.VMEM((KEYS, ATT_W), BF16),
        pltpu.VMEM((ML_HEADS, TILE, KEYS), F32),
        pltpu.VMEM((TILE + 16, 2 * ML_W), F32),
        pltpu.VMEM((N_PAIRS, LANES, LANES), F32),
        pltpu.VMEM((N_PAIRS, 8, LANES), F32),
        pltpu.VMEM((ML_HEADS, 8, LANES), F32),
        pltpu.VMEM((LANES, GLA_VW), F32),
        pltpu.VMEM((TILE, D_MODEL), F32),
        pltpu.VMEM((N_PAIRS, TILE, 2 * TILE), BF16),
    ]
    return pl.pallas_call(
        functools.partial(_layer_kernel, final_norm=final_norm, tiles_per_seq=tiles_per_seq),
        out_shape=jax.ShapeDtypeStruct(x2d.shape, x2d.dtype),
        grid=(n_tiles + 1,),
        in_specs=in_specs,
        out_specs=cur_spec,
        scratch_shapes=scratch,
        compiler_params=pltpu.CompilerParams(
            dimension_semantics=("arbitrary",),
            vmem_limit_bytes=VMEM_LIMIT),
        name="hybrid_layer_final" if final_norm else "hybrid_layer",
    )(*args)


def _prep_layer(norm_g, w_in_t, b_gates, conv_w, w_alpha, b_alpha, rel_bias, ml_norm_g, gla_norm_g, w_out):
    pad = jnp.zeros((D_INP - SM - GLA_RANK - 2 * ML_HEADS, D_MODEL), w_in_t.dtype)
    win = jnp.concatenate([w_in_t[:3072], w_in_t[3084:3980], w_in_t[3996:4252],
                           w_in_t[3980:3996], w_in_t[3072:3084], pad], axis=0).astype(BF16)
    gb = jnp.zeros((1, LANES), F32).at[0, SM_I:SM_I + 2 * ML_HEADS].set(b_gates)
    wal = jnp.zeros((LANES, GLA_KW), F32).at[:GLA_RANK, :].set(w_alpha).astype(BF16)
    nh = rel_bias.shape[0]
    gp = jnp.concatenate([
        jnp.broadcast_to(rel_bias[:, 2 * REL_CLIP:], (nh, KEYS - REL_CLIP + 1)),
        rel_bias[:, 2 * REL_CLIP - 1:0:-1],
        jnp.broadcast_to(rel_bias[:, :1], (nh, ROLL_W - KEYS - REL_CLIP)),
    ], axis=1)
    return dict(ng=norm_g.reshape(1, D_MODEL), win=win, gb=gb, conv=conv_w, wal=wal,
                bal=b_alpha.reshape(1, GLA_KW), gp=gp, mlg=ml_norm_g.reshape(1, ML_W),
                glg=gla_norm_g.reshape(1, GLA_VW), wout=w_out.astype(BF16))


def kernel(x, norm_g, w_in, b_gates, conv_w, w_alpha, b_alpha, rel_bias, ml_norm_g, gla_norm_g, w_out, final_g):
    depth = norm_g.shape[0]
    bsz, seq, _ = x.shape
    fg = final_g.reshape(1, D_MODEL)
    w_in_t = jnp.transpose(w_in, (2, 0, 1))
    x2d = x.reshape(bsz * seq, D_MODEL)
    for l in range(depth):
        prm = _prep_layer(norm_g[l], w_in_t[:, l, :], b_gates[l], conv_w[l], w_alpha[l], b_alpha[l], rel_bias[l],
                          ml_norm_g[l], gla_norm_g[l], w_out[l])
        x2d = _layer_call(x2d, prm, fg, final_norm=(l == depth - 1), tiles_per_seq=seq // TILE)
    return x2d.reshape(bsz, seq, D_MODEL)
```

```python
import functools

import jax
import jax.numpy as jnp
from jax import lax
from jax.experimental import pallas as pl
from jax.experimental.pallas import tpu as pltpu

F32 = jnp.float32
BF16 = jnp.bfloat16

D_MODEL = 1024
CHUNK = 64
HEAD_DIM = 64
ATT_W = 384
ML_W = 384
GLA_KW = 128
GLA_VW = 256
GLA_DK = 32
GLA_RANK = 16
GLA_TAU = 16.0
ML_HEADS = 6
PAST_CHUNKS = 8
REL_CLIP = 128
CONV_W = 4
EPS = 1e-6
NEG = -1e30

LANES = 128
TILE = 256
NCH = TILE // CHUNK
HIST = PAST_CHUNKS * CHUNK
KEYS = HIST + TILE
HALF_ROWS = TILE // 2
HALF_KEYS = HIST + HALF_ROWS
BAND = (PAST_CHUNKS + 1) * CHUNK
ROLL_W = 1024
N_PAIRS = ML_W // LANES

AQ, AK, AV, AG = 0, 384, 768, 1152
MQ, MK, MV, MO, MG = 1536, 1920, 2304, 2688, 3072
GQ, GK, GV, GG = 3456, 3584, 3712, 3968
SM = 4224
D_INP = 4352
IP_BLK = 256
SM_I = GLA_RANK
SM_F = GLA_RANK + ML_HEADS

VMEM_LIMIT = 56 * 1024 * 1024


def _log_sigmoid(x):
    return jnp.minimum(x, 0.0) - jnp.log1p(jnp.exp(-jnp.abs(x)))


def _silu(x):
    return x * jax.nn.sigmoid(x)


def _dot_nt(a, b):
    return lax.dot_general(a, b, (((1,), (1,)), ((), ())), preferred_element_type=F32)


def _dot_tn(a, b):
    return lax.dot_general(a, b, (((0,), (0,)), ((), ())), preferred_element_type=F32)


def _dot(a, b):
    return jnp.dot(a, b, preferred_element_type=F32)


def _pair_rms(hcur, gain, m0):
    sq = hcur * hcur
    ms0 = jnp.sum(jnp.where(m0, sq, 0.0), axis=1, keepdims=True)
    ms1 = jnp.sum(jnp.where(m0, 0.0, sq), axis=1, keepdims=True)
    ms = jnp.where(m0, ms0, ms1) * (1.0 / HEAD_DIM)
    return hcur * lax.rsqrt(ms + EPS) * gain


def _step(i, zc, zn, xn_ref, xc_ref, ng_ref, win_ref, gb_ref, conv_ref, wal_ref, bal_ref,
          mlg_ref, glg_ref, wout_ref, fg_ref, o_ref,
          kbuf, vbuf, bias_ref, cbuf, ml_c, ml_n, ml_m, gla_s, y_ref, sc_ref, *, final_norm):
    lane = lax.broadcasted_iota(jnp.int32, (1, LANES), 1)
    m0 = lane < HEAD_DIM
    m1 = jnp.logical_not(m0)

    xn = xn_ref[...]
    hn = xn * lax.rsqrt(jnp.sum(xn * xn, axis=1, keepdims=True) * (1.0 / D_MODEL) + EPS) * ng_ref[...]
    hn_b = hn.astype(BF16)
    ip_next = [0]

    def ip(count):
        for _ in range(count):
            n0 = ip_next[0] * IP_BLK
            if n0 < D_INP:
                zn[:, n0:n0 + IP_BLK] = _dot_nt(hn_b, win_ref[n0:n0 + IP_BLK, :])
                ip_next[0] += 1

    kbuf[0:HIST, :] = kbuf[TILE:KEYS, :]
    vbuf[0:HIST, :] = vbuf[TILE:KEYS, :]
    kbuf[HIST:KEYS, :] = zc[:, AK:AK + ATT_W].astype(BF16)
    vbuf[HIST:KEYS, :] = zc[:, AV:AV + ATT_W].astype(BF16)

    kcol = lax.broadcasted_iota(jnp.int32, (1, KEYS), 1)
    pen = jnp.where(kcol + (i * TILE - HIST) >= 0, 0.0, NEG)

    def att_scores(u):
        h, half = u // 2, u % 2
        c0 = (h // 2) * LANES
        r0, k0 = half * HALF_ROWS, half * HALF_ROWS
        q2 = zc[r0:r0 + HALF_ROWS, AQ + c0:AQ + c0 + LANES] * (HEAD_DIM ** -0.5)
        qm = jnp.where(m0 if h % 2 == 0 else m1, q2, 0.0).astype(BF16)
        return (_dot_nt(qm, kbuf[k0:k0 + HALF_KEYS, c0:c0 + LANES])
                + bias_ref[h, r0:r0 + HALF_ROWS, k0:k0 + HALF_KEYS] + pen[:, k0:k0 + HALF_KEYS])

    def att_out(u, s):
        h, half = u // 2, u % 2
        c0 = (h // 2) * LANES
        k0 = half * HALF_ROWS
        mx = jnp.max(s, axis=1, keepdims=True)
        pe = jnp.exp(s - mx)
        l = jnp.sum(pe, axis=1, keepdims=True)
        return _dot(pe.astype(BF16), vbuf[k0:k0 + HALF_KEYS, c0:c0 + LANES]) / l

    n_units = 2 * ML_HEADS
    s_cur = att_scores(0)
    outs = []
    for u in range(n_units):
        s_nxt = att_scores(u + 1) if u + 1 < n_units else None
        if u % 2 == 1:
            ip(1)
        outs.append(att_out(u, s_cur))
        s_cur = s_nxt
        if u % 4 == 3:
            c0 = (u // 4) * LANES
            att = jnp.where(m0, jnp.concatenate(outs[u - 3:u - 1], axis=0),
                            jnp.concatenate(outs[u - 1:u + 1], axis=0))
            y_ref[:, c0:c0 + LANES] = att * _silu(zc[:, AG + c0:AG + c0 + LANES])
    ip(1)

    cbuf[8:8 + TILE, :] = zc[:, MQ:MQ + 2 * ML_W]
    for c0 in range(0, 2 * ML_W, LANES):
        acc = cbuf[8:8 + TILE, c0:c0 + LANES] * conv_ref[CONV_W - 1:CONV_W, c0:c0 + LANES]
        for sft in range(1, CONV_W):
            acc = acc + cbuf[8 - sft:8 - sft + TILE, c0:c0 + LANES] * \
                conv_ref[CONV_W - 1 - sft:CONV_W - sft, c0:c0 + LANES]
        act = _silu(acc)
        if c0 >= ML_W:
            act = act * (HEAD_DIM ** -0.5)
        zc[:, MQ + c0:MQ + c0 + LANES] = act
    cbuf[0:8, :] = cbuf[TILE:TILE + 8, :]

    small = zc[:, SM:SM + LANES]
    pre = small + gb_ref[...]
    lf = _log_sigmoid(pre)
    la = _log_sigmoid(_dot(small.astype(BF16), wal_ref[...]) + bal_ref[...]) * (1.0 / GLA_TAU)
    tr = lax.broadcasted_iota(jnp.int32, (TILE, TILE), 0)
    tc = lax.broadcasted_iota(jnp.int32, (TILE, TILE), 1)
    tri = jnp.where(tr >= tc, 1.0, 0.0).astype(BF16)
    terms = []
    for v in (lf, la):
        hi = v.astype(BF16)
        r1 = v - hi.astype(F32)
        mid = r1.astype(BF16)
        terms += [hi, mid, (r1 - mid.astype(F32)).astype(BF16)]
    cs = _dot(tri, jnp.concatenate(terms, axis=1))
    bml = (cs[:, 2 * LANES:3 * LANES] + cs[:, LANES:2 * LANES]) + cs[:, 0:LANES]
    bla = (cs[:, 5 * LANES:6 * LANES] + cs[:, 4 * LANES:5 * LANES]) + cs[:, 3 * LANES:4 * LANES]
    bgl = jnp.concatenate(
        [bla[0:CHUNK]] + [bla[c * CHUNK:(c + 1) * CHUNK] - bla[c * CHUNK - 1:c * CHUNK]
                          for c in range(1, NCH)], axis=0)
    pre_t = pre.T
    bml_t = bml.T
    bgl_t = bgl.T
    ip(1)

    prow = lax.broadcasted_iota(jnp.int32, (LANES, LANES), 0)
    pcol = lax.broadcasted_iota(jnp.int32, (LANES, LANES), 1)
    pair_diag = (prow < HEAD_DIM) == (pcol < HEAD_DIM)
    prow_first = prow[:, 0:1] < HEAD_DIM

    ml_q, ml_k, ml_vb, ml_vv, ml_s, ml_ni, ml_ct = [], [], [], [], [], [], []
    for p in range(N_PAIRS):
        c0 = p * LANES
        q2 = zc[:, MQ + c0:MQ + c0 + LANES]
        k2 = zc[:, MK + c0:MK + c0 + LANES]
        v2 = zc[:, MV + c0:MV + c0 + LANES]
        q2b = q2.astype(BF16)
        kk = jnp.concatenate([jnp.where(m0, k2, 0.0), jnp.where(m0, 0.0, k2)], axis=0).astype(BF16)
        ct = ml_c[p]
        ml_q.append(q2)
        ml_k.append(k2)
        ml_vb.append(v2.astype(BF16))
        ml_vv.append(jnp.concatenate([jnp.where(m0, v2, 0.0), jnp.where(m0, 0.0, v2)],
                                     axis=0).astype(BF16))
        ml_ct.append(ct)
        ml_s.append(_dot_nt(q2b, kk))
        ml_ni.append(_dot(q2b, ct.astype(BF16)))
    ip(2)

    for p in range(N_PAIRS):
        c0 = p * LANES
        q2, k2, s_pair = ml_q[p], ml_k[p], ml_s[p]
        nvec = ml_n[p][0:1, :]
        qn = q2 * nvec
        iws, dens, wcols, solds = [], [], [], []
        for e in range(2):
            h = 2 * p + e
            mprev = ml_m[h][0:1, :]
            bt_all = jnp.broadcast_to(bml[:, SM_F + h:SM_F + h + 1], (TILE, LANES))
            it_all = jnp.broadcast_to(pre[:, SM_I + h:SM_I + h + 1], (TILE, LANES))
            bj = bml_t[SM_F + h:SM_F + h + 1, :]
            ij = pre_t[SM_I + h:SM_I + h + 1, :]
            den_inter = jnp.sum(jnp.where(m0 if e == 0 else m1, qn, 0.0), axis=1, keepdims=True)
            iw_blocks, den_blocks = [], []
            for rb in range(NCH):
                r0 = rb * CHUNK
                ncol = LANES * ((r0 + CHUNK + LANES - 1) // LANES)
                rr = lax.broadcasted_iota(jnp.int32, (CHUNK, ncol), 0) + r0
                kc = lax.broadcasted_iota(jnp.int32, (CHUNK, ncol), 1)
                btc = bt_all[r0:r0 + CHUNK, 0:1]
                log_d = jnp.where(rr >= kc, btc - bj[:, :ncol] + ij[:, :ncol], NEG)
                inter_log = btc + mprev[:, 0:1]
                m_row = jnp.maximum(inter_log, jnp.max(log_d, axis=1, keepdims=True))
                sc = s_pair[r0:r0 + CHUNK, e * TILE:e * TILE + ncol] * jnp.exp(log_d - m_row)
                sc_ref[p, r0:r0 + CHUNK, e * TILE:e * TILE + ncol] = sc.astype(BF16)
                if ncol < TILE:
                    sc_ref[p, r0:r0 + CHUNK, e * TILE + ncol:(e + 1) * TILE] = jnp.zeros(
                        (CHUNK, TILE - ncol), BF16)
                inter_w = jnp.exp(inter_log - m_row)
                den = jnp.sum(sc, axis=1, keepdims=True) + inter_w * den_inter[r0:r0 + CHUNK]
                den_blocks.append(jnp.maximum(jnp.abs(den), jnp.exp(-m_row)))
                iw_blocks.append(inter_w)
            iws.append(jnp.concatenate(iw_blocks, axis=0))
            dens.append(jnp.concatenate(den_blocks, axis=0))
            g_row = bt_all[TILE - 1:TILE, :]
            a_row = g_row[:, 0:1] - bj + ij
            m_new = jnp.maximum(g_row + mprev, jnp.max(a_row, axis=1, keepdims=True))
            wcols.append(jnp.exp(g_row - bt_all + it_all - m_new))
            solds.append(jnp.exp(g_row + mprev - m_new))
            ml_m[h] = jnp.broadcast_to(m_new, (8, LANES))
        num = _dot(sc_ref[p], ml_vv[p]) + jnp.where(m0, iws[0], iws[1]) * ml_ni[p]
        hcur = num / jnp.where(m0, dens[0], dens[1])
        y_ref[:, ATT_W + c0:ATT_W + c0 + LANES] = (
            jax.nn.sigmoid(zc[:, MO + c0:MO + c0 + LANES])
            * _pair_rms(hcur, mlg_ref[:, c0:c0 + LANES], m0)
            * _silu(zc[:, MG + c0:MG + c0 + LANES]))
        kw = k2 * jnp.where(m0, wcols[0], wcols[1])
        upd = _dot_tn(kw.astype(BF16), ml_vb[p])
        sold_col = jnp.where(prow_first, solds[0][:, 0:1], solds[1][:, 0:1])
        ml_c[p] = jnp.where(pair_diag, sold_col * ml_ct[p] + upd, 0.0)
        nvec = jnp.where(m0, solds[0], solds[1]) * nvec + jnp.sum(kw, axis=0, keepdims=True)
        ml_n[p] = jnp.broadcast_to(nvec, (8, LANES))
        ip(1)

    grow = lax.broadcasted_iota(jnp.int32, (LANES, GLA_VW), 0)
    gcol = lax.broadcasted_iota(jnp.int32, (LANES, GLA_VW), 1)
    gla_diag = (grow >> 5) == (gcol >> 6)
    khead = lane >> 5
    vhead = lax.broadcasted_iota(jnp.int32, (1, GLA_VW), 1) >> 6
    arow = lax.broadcasted_iota(jnp.int32, (CHUNK, GLA_VW), 0)
    acol = lax.broadcasted_iota(jnp.int32, (CHUNK, GLA_VW), 1)
    causal4 = arow >= (acol & (CHUNK - 1))
    n_gh = GLA_KW // GLA_DK
    yc0 = ATT_W + ML_W
    g_a, g_upd, g_qi, g_vvg, g_dec = [], [], [], [], []
    for c in range(NCH):
        r0 = c * CHUNK
        bc = bgl[r0:r0 + CHUNK, :]
        bref = bc[CHUNK // 2 - 1:CHUNK // 2, :]
        btot = bc[CHUNK - 1:CHUNK, :]
        gq = zc[r0:r0 + CHUNK, GQ:GQ + GLA_KW] * (GLA_DK ** -0.5)
        gk = zc[r0:r0 + CHUNK, GK:GK + GLA_KW]
        gv = zc[r0:r0 + CHUNK, GV:GV + GLA_VW]
        qe = (gq * jnp.exp(bc - bref)).astype(BF16)
        ke = gk * jnp.exp(bref - bc)
        kek = jnp.concatenate([jnp.where(khead == h, ke, 0.0) for h in range(n_gh)],
                              axis=0).astype(BF16)
        g_vvg.append(jnp.concatenate([jnp.where(vhead == h, gv, 0.0) for h in range(n_gh)],
                                     axis=0).astype(BF16))
        g_qi.append((gq * jnp.exp(bc)).astype(BF16))
        kd = (gk * jnp.exp(btot - bc)).astype(BF16)
        g_dec.append(jnp.exp(bgl_t[:, r0 + CHUNK - 1:r0 + CHUNK]))
        g_a.append(_dot_nt(qe, kek))
        g_upd.append(_dot_tn(kd, gv.astype(BF16)))
    ip(2)
    s_all = gla_s[...]
    for c in range(NCH):
        r0 = c * CHUNK
        a = jnp.where(causal4, g_a[c], 0.0)
        o = _dot(g_qi[c], s_all.astype(BF16)) + _dot(a.astype(BF16), g_vvg[c])
        y_ref[r0:r0 + CHUNK, yc0:yc0 + GLA_VW] = o
        s_all = jnp.where(gla_diag, g_dec[c] * s_all + g_upd[c], 0.0)
        ip(1)
    gla_s[...] = s_all

    for hp in range(GLA_VW // LANES):
        c0 = hp * LANES
        hcur = y_ref[:, yc0 + c0:yc0 + c0 + LANES]
        y_ref[:, yc0 + c0:yc0 + c0 + LANES] = (
            _pair_rms(hcur, glg_ref[:, c0:c0 + LANES], m0) * _silu(zc[:, GG + c0:GG + c0 + LANES]))
    ip(D_INP // IP_BLK)

    xo = xc_ref[...] + _dot(y_ref[...].astype(BF16), wout_ref[...])
    if final_norm:
        xo = xo * lax.rsqrt(jnp.sum(xo * xo, axis=1, keepdims=True) * (1.0 / D_MODEL) + EPS) * fg_ref[...]
    o_ref[...] = xo


def _layer_kernel(xn_ref, xc_ref, ng_ref, win_ref, gb_ref, conv_ref, wal_ref, bal_ref, gp_ref,
                  mlg_ref, glg_ref, wout_ref, fg_ref,
                  o_ref,
                  z_a, z_b, kbuf, vbuf, bias_ref, cbuf, ml_c, ml_n, ml_m, gla_s, y_ref, sc_ref,
                  *, final_norm, tiles_per_seq):
    g = pl.program_id(0)
    i = lax.rem(g + (tiles_per_seq - 1), tiles_per_seq)

    @pl.when(g == 0)
    def _build_bias():
        r = lax.broadcasted_iota(jnp.int32, (TILE, KEYS), 0)
        m = lax.broadcasted_iota(jnp.int32, (TILE, KEYS), 1)
        jj = m - ((r >> 6) << 6)
        in_band = (jj >= 0) & (jj < BAND)
        for h in range(ML_HEADS):
            row = jnp.broadcast_to(gp_ref[h:h + 1, :], (TILE, ROLL_W))
            rolled = pltpu.roll(row, KEYS, 1, stride=1, stride_axis=0)
            bias_ref[h] = jnp.where(in_band, rolled[:, :KEYS], NEG)
        z_b[...] = jnp.zeros(z_b.shape, F32)

    @pl.when((i == 0) | (g == 0))
    def _reset():
        kbuf[...] = jnp.zeros(kbuf.shape, kbuf.dtype)
        vbuf[...] = jnp.zeros(vbuf.shape, vbuf.dtype)
        cbuf[0:8, :] = jnp.zeros((8, cbuf.shape[1]), F32)
        ml_c[...] = jnp.zeros(ml_c.shape, F32)
        ml_n[...] = jnp.zeros(ml_n.shape, F32)
        ml_m[...] = jnp.zeros(ml_m.shape, F32)
        gla_s[...] = jnp.zeros(gla_s.shape, F32)

    step = functools.partial(
        _step, i, xn_ref=xn_ref, xc_ref=xc_ref, ng_ref=ng_ref, win_ref=win_ref, gb_ref=gb_ref,
        conv_ref=conv_ref, wal_ref=wal_ref, bal_ref=bal_ref, mlg_ref=mlg_ref, glg_ref=glg_ref,
        wout_ref=wout_ref, fg_ref=fg_ref, o_ref=o_ref, kbuf=kbuf, vbuf=vbuf, bias_ref=bias_ref,
        cbuf=cbuf, ml_c=ml_c, ml_n=ml_n, ml_m=ml_m, gla_s=gla_s, y_ref=y_ref, sc_ref=sc_ref,
        final_norm=final_norm)

    @pl.when(lax.rem(g, 2) == 0)
    def _even():
        step(z_b, z_a)

    @pl.when(lax.rem(g, 2) == 1)
    def _odd():
        step(z_a, z_b)


def _const_spec(shape):
    nd = len(shape)
    return pl.BlockSpec(shape, lambda g, _nd=nd: (0,) * _nd, pipeline_mode=pl.Buffered(1))


def _layer_call(x2d, prm, final_g, final_norm, tiles_per_seq):
    n_tiles = x2d.shape[0] // TILE
    args = (x2d, x2d, prm["ng"], prm["win"], prm["gb"], prm["conv"], prm["wal"], prm["bal"], prm["gp"],
            prm["mlg"], prm["glg"], prm["wout"], final_g)
    nxt_spec = pl.BlockSpec((TILE, D_MODEL), lambda g: (jnp.minimum(g, n_tiles - 1), 0))
    cur_spec = pl.BlockSpec((TILE, D_MODEL), lambda g: (jnp.maximum(g - 1, 0), 0))
    in_specs = [nxt_spec, cur_spec] + [_const_spec(a.shape) for a in args[2:]]
    scratch = [
        pltpu.VMEM((TILE, D_INP), F32),
        pltpu.VMEM((TILE, D_INP), F32),
        pltpu.VMEM((KEYS, ATT_W), BF16),
        pltpu.VMEM((KEYS, ATT_W), BF16),
        pltpu.VMEM((ML_HEADS, TILE, KEYS), F32),
        pltpu.VMEM((TILE + 16, 2 * ML_W), F32),
        pltpu.VMEM((N_PAIRS, LANES, LANES), F32),
        pltpu.VMEM((N_PAIRS, 8, LANES), F32),
        pltpu.VMEM((ML_HEADS, 8, LANES), F32),
        pltpu.VMEM((LANES, GLA_VW), F32),
        pltpu.VMEM((TILE, D_MODEL), F32),
        pltpu.VMEM((N_PAIRS, TILE, 2 * TILE), BF16),
    ]
    return pl.pallas_call(
        functools.partial(_layer_kernel, final_norm=final_norm, tiles_per_seq=tiles_per_seq),
        out_shape=jax.ShapeDtypeStruct(x2d.shape, x2d.dtype),
        grid=(n_tiles + 1,),
        in_specs=in_specs,
        out_specs=cur_spec,
        scratch_shapes=scratch,
        compiler_params=pltpu.CompilerParams(
            dimension_semantics=("arbitrary",),
            vmem_limit_bytes=VMEM_LIMIT),
        name="hybrid_layer_final" if final_norm else "hybrid_layer",
    )(*args)


def _prep_layer(norm_g, w_in_t, b_gates, conv_w, w_alpha, b_alpha, rel_bias, ml_norm_g, gla_norm_g, w_out):
    pad = jnp.zeros((D_INP - SM - GLA_RANK - 2 * ML_HEADS, D_MODEL), w_in_t.dtype)
    win = jnp.concatenate([w_in_t[:3072], w_in_t[3084:3980], w_in_t[3996:4252],
                           w_in_t[3980:3996], w_in_t[3072:3084], pad], axis=0).astype(BF16)
    gb = jnp.zeros((1, LANES), F32).at[0, SM_I:SM_I + 2 * ML_HEADS].set(b_gates)
    wal = jnp.zeros((LANES, GLA_KW), F32).at[:GLA_RANK, :].set(w_alpha).astype(BF16)
    nh = rel_bias.shape[0]
    gp = jnp.concatenate([
        jnp.broadcast_to(rel_bias[:, 2 * REL_CLIP:], (nh, KEYS - REL_CLIP + 1)),
        rel_bias[:, 2 * REL_CLIP - 1:0:-1],
        jnp.broadcast_to(rel_bias[:, :1], (nh, ROLL_W - KEYS - REL_CLIP)),
    ], axis=1)
    return dict(ng=norm_g.reshape(1, D_MODEL), win=win, gb=gb, conv=conv_w, wal=wal,
                bal=b_alpha.reshape(1, GLA_KW), gp=gp, mlg=ml_norm_g.reshape(1, ML_W),
                glg=gla_norm_g.reshape(1, GLA_VW), wout=w_out.astype(BF16))


def kernel(x, norm_g, w_in, b_gates, conv_w, w_alpha, b_alpha, rel_bias, ml_norm_g, gla_norm_g, w_out, final_g):
    depth = norm_g.shape[0]
    bsz, seq, _ = x.shape
    fg = final_g.reshape(1, D_MODEL)
    w_in_t = jnp.transpose(w_in, (2, 0, 1))
    x2d = x.reshape(bsz * seq, D_MODEL)
    for l in range(depth):
        prm = _prep_layer(norm_g[l], w_in_t[:, l, :], b_gates[l], conv_w[l], w_alpha[l], b_alpha[l], rel_bias[l],
                          ml_norm_g[l], gla_norm_g[l], w_out[l])
        x2d = _layer_call(x2d, prm, fg, final_norm=(l == depth - 1), tiles_per_seq=seq // TILE)
    return x2d.reshape(bsz, seq, D_MODEL)
```

```python
import functools

import jax
import jax.numpy as jnp
from jax import lax
from jax.experimental import pallas as pl
from jax.experimental.pallas import tpu as pltpu

F32 = jnp.float32
BF16 = jnp.bfloat16

D_MODEL = 1024
CHUNK = 64
HEAD_DIM = 64
ATT_W = 384
ML_W = 384
GLA_KW = 128
GLA_VW = 256
GLA_DK = 32
GLA_RANK = 16
GLA_TAU = 16.0
ML_HEADS = 6
PAST_CHUNKS = 8
REL_CLIP = 128
CONV_W = 4
EPS = 1e-6
NEG = -1e30

LANES = 128
TILE = 256
NCH = TILE // CHUNK
HIST = PAST_CHUNKS * CHUNK
KEYS = HIST + TILE
HALF_ROWS = TILE // 2
HALF_KEYS = HIST + HALF_ROWS
BAND = (PAST_CHUNKS + 1) * CHUNK
ROLL_W = 1024
N_PAIRS = ML_W // LANES

AQ, AK, AV, AG = 0, 384, 768, 1152
MQ, MK, MV, MO, MG = 1536, 1920, 2304, 2688, 3072
GQ, GK, GV, GG = 3456, 3584, 3712, 3968
SM = 4224
D_INP = 4352
IP_BLK = 256
SM_I = GLA_RANK
SM_F = GLA_RANK + ML_HEADS

VMEM_LIMIT = 56 * 1024 * 1024


def _log_sigmoid(x):
    return jnp.minimum(x, 0.0) - jnp.log1p(jnp.exp(-jnp.abs(x)))


def _silu(x):
    return x * jax.nn.sigmoid(x)


def _dot_nt(a, b):
    return lax.dot_general(a, b, (((1,), (1,)), ((), ())), preferred_element_type=F32)


def _dot_tn(a, b):
    return lax.dot_general(a, b, (((0,), (0,)), ((), ())), preferred_element_type=F32)


def _dot(a, b):
    return jnp.dot(a, b, preferred_element_type=F32)


def _pair_rms(hcur, gain, m0):
    sq = hcur * hcur
    ms0 = jnp.sum(jnp.where(m0, sq, 0.0), axis=1, keepdims=True)
    ms1 = jnp.sum(jnp.where(m0, 0.0, sq), axis=1, keepdims=True)
    ms = jnp.where(m0, ms0, ms1) * (1.0 / HEAD_DIM)
    return hcur * lax.rsqrt(ms + EPS) * gain


def _step(i, zc, zn, xn_ref, xc_ref, ng_ref, win_ref, gb_ref, conv_ref, wal_ref, bal_ref,
          mlg_ref, glg_ref, wout_ref, fg_ref, o_ref,
          kbuf, vbuf, bias_ref, cbuf, ml_c, ml_n, ml_m, gla_s, y_ref, sc_ref, *, final_norm):
    lane = lax.broadcasted_iota(jnp.int32, (1, LANES), 1)
    m0 = lane < HEAD_DIM
    m1 = jnp.logical_not(m0)

    xn = xn_ref[...]
    hn = xn * lax.rsqrt(jnp.sum(xn * xn, axis=1, keepdims=True) * (1.0 / D_MODEL) + EPS) * ng_ref[...]
    hn_b = hn.astype(BF16)
    ip_next = [0]

    def ip(count):
        for _ in range(count):
            n0 = ip_next[0] * IP_BLK
            if n0 < D_INP:
                zn[:, n0:n0 + IP_BLK] = _dot_nt(hn_b, win_ref[n0:n0 + IP_BLK, :])
                ip_next[0] += 1

    kbuf[0:HIST, :] = kbuf[TILE:KEYS, :]
    vbuf[0:HIST, :] = vbuf[TILE:KEYS, :]
    kbuf[HIST:KEYS, :] = zc[:, AK:AK + ATT_W].astype(BF16)
    vbuf[HIST:KEYS, :] = zc[:, AV:AV + ATT_W].astype(BF16)

    kcol = lax.broadcasted_iota(jnp.int32, (1, KEYS), 1)
    pen = jnp.where(kcol + (i * TILE - HIST) >= 0, 0.0, NEG)

    def att_scores(u):
        h, half = u // 2, u % 2
        c0 = (h // 2) * LANES
        r0, k0 = half * HALF_ROWS, half * HALF_ROWS
        q2 = zc[r0:r0 + HALF_ROWS, AQ + c0:AQ + c0 + LANES] * (HEAD_DIM ** -0.5)
        qm = jnp.where(m0 if h % 2 == 0 else m1, q2, 0.0).astype(BF16)
        return (_dot_nt(qm, kbuf[k0:k0 + HALF_KEYS, c0:c0 + LANES])
                + bias_ref[h, r0:r0 + HALF_ROWS, k0:k0 + HALF_KEYS] + pen[:, k0:k0 + HALF_KEYS])

    def att_out(u, s):
        h, half = u // 2, u % 2
        c0 = (h // 2) * LANES
        k0 = half * HALF_ROWS
        mx = jnp.max(s, axis=1, keepdims=True)
        pe = jnp.exp(s - mx)
        l = jnp.sum(pe, axis=1, keepdims=True)
        return _dot(pe.astype(BF16), vbuf[k0:k0 + HALF_KEYS, c0:c0 + LANES]) / l

    n_units = 2 * ML_HEADS
    s_cur = att_scores(0)
    outs = []
    for u in range(n_units):
        s_nxt = att_scores(u + 1) if u + 1 < n_units else None
        if u % 2 == 1:
            ip(1)
        outs.append(att_out(u, s_cur))
        s_cur = s_nxt
        if u % 4 == 3:
            c0 = (u // 4) * LANES
            att = jnp.where(m0, jnp.concatenate(outs[u - 3:u - 1], axis=0),
                            jnp.concatenate(outs[u - 1:u + 1], axis=0))
            y_ref[:, c0:c0 + LANES] = att * _silu(zc[:, AG + c0:AG + c0 + LANES])
    ip(1)

    cbuf[8:8 + TILE, :] = zc[:, MQ:MQ + 2 * ML_W]
    for c0 in range(0, 2 * ML_W, LANES):
        acc = cbuf[8:8 + TILE, c0:c0 + LANES] * conv_ref[CONV_W - 1:CONV_W, c0:c0 + LANES]
        for sft in range(1, CONV_W):
            acc = acc + cbuf[8 - sft:8 - sft + TILE, c0:c0 + LANES] * \
                conv_ref[CONV_W - 1 - sft:CONV_W - sft, c0:c0 + LANES]
        act = _silu(acc)
        if c0 >= ML_W:
            act = act * (HEAD_DIM ** -0.5)
        zc[:, MQ + c0:MQ + c0 + LANES] = act
    cbuf[0:8, :] = cbuf[TILE:TILE + 8, :]

    small = zc[:, SM:SM + LANES]
    pre = small + gb_ref[...]
    lf = _log_sigmoid(pre)
    la = _log_sigmoid(_dot(small.astype(BF16), wal_ref[...]) + bal_ref[...]) * (1.0 / GLA_TAU)
    tr = lax.broadcasted_iota(jnp.int32, (TILE, TILE), 0)
    tc = lax.broadcasted_iota(jnp.int32, (TILE, TILE), 1)
    tri = jnp.where(tr >= tc, 1.0, 0.0).astype(BF16)
    terms = []
    for v in (lf, la):
        hi = v.astype(BF16)
        r1 = v - hi.astype(F32)
        mid = r1.astype(BF16)
        terms += [hi, mid, (r1 - mid.astype(F32)).astype(BF16)]
    cs = _dot(tri, jnp.concatenate(terms, axis=1))
    bml = (cs[:, 2 * LANES:3 * LANES] + cs[:, LANES:2 * LANES]) + cs[:, 0:LANES]
    bla = (cs[:, 5 * LANES:6 * LANES] + cs[:, 4 * LANES:5 * LANES]) + cs[:, 3 * LANES:4 * LANES]
    bgl = jnp.concatenate(
        [bla[0:CHUNK]] + [bla[c * CHUNK:(c + 1) * CHUNK] - bla[c * CHUNK - 1:c * CHUNK]
                          for c in range(1, NCH)], axis=0)
    pre_t = pre.T
    bml_t = bml.T
    bgl_t = bgl.T
    ip(1)

    prow = lax.broadcasted_iota(jnp.int32, (LANES, LANES), 0)
    pcol = lax.broadcasted_iota(jnp.int32, (LANES, LANES), 1)
    pair_diag = (prow < HEAD_DIM) == (pcol < HEAD_DIM)
    prow_first = prow[:, 0:1] < HEAD_DIM

    ml_q, ml_k, ml_vb, ml_vv, ml_s, ml_ni, ml_ct = [], [], [], [], [], [], []
    for p in range(N_PAIRS):
        c0 = p * LANES
        q2 = zc[:, MQ + c0:MQ + c0 + LANES]
        k2 = zc[:, MK + c0:MK + c0 + LANES]
        v2 = zc[:, MV + c0:MV + c0 + LANES]
        q2b = q2.astype(BF16)
        kk = jnp.concatenate([jnp.where(m0, k2, 0.0), jnp.where(m0, 0.0, k2)], axis=0).astype(BF16)
        ct = ml_c[p]
        ml_q.append(q2)
        ml_k.append(k2)
        ml_vb.append(v2.astype(BF16))
        ml_vv.append(jnp.concatenate([jnp.where(m0, v2, 0.0), jnp.where(m0, 0.0, v2)],
                                     axis=0).astype(BF16))
        ml_ct.append(ct)
        ml_s.append(_dot_nt(q2b, kk))
        ml_ni.append(_dot(q2b, ct.astype(BF16)))
    ip(2)

    for p in range(N_PAIRS):
        c0 = p * LANES
        q2, k2, s_pair = ml_q[p], ml_k[p], ml_s[p]
        nvec = ml_n[p][0:1, :]
        qn = q2 * nvec
        iws, dens, wcols, solds = [], [], [], []
        for e in range(2):
            h = 2 * p + e
            mprev = ml_m[h][0:1, :]
            bt_all = jnp.broadcast_to(bml[:, SM_F + h:SM_F + h + 1], (TILE, LANES))
            it_all = jnp.broadcast_to(pre[:, SM_I + h:SM_I + h + 1], (TILE, LANES))
            bj = bml_t[SM_F + h:SM_F + h + 1, :]
            ij = pre_t[SM_I + h:SM_I + h + 1, :]
            den_inter = jnp.sum(jnp.where(m0 if e == 0 else m1, qn, 0.0), axis=1, keepdims=True)
            iw_blocks, den_blocks = [], []
            for rb in range(NCH):
                r0 = rb * CHUNK
                ncol = LANES * ((r0 + CHUNK + LANES - 1) // LANES)
                rr = lax.broadcasted_iota(jnp.int32, (CHUNK, ncol), 0) + r0
                kc = lax.broadcasted_iota(jnp.int32, (CHUNK, ncol), 1)
                btc = bt_all[r0:r0 + CHUNK, 0:1]
                log_d = jnp.where(rr >= kc, btc - bj[:, :ncol] + ij[:, :ncol], NEG)
                inter_log = btc + mprev[:, 0:1]
                m_row = jnp.maximum(inter_log, jnp.max(log_d, axis=1, keepdims=True))
                sc = s_pair[r0:r0 + CHUNK, e * TILE:e * TILE + ncol] * jnp.exp(log_d - m_row)
                sc_ref[p, r0:r0 + CHUNK, e * TILE:e * TILE + ncol] = sc.astype(BF16)
                if ncol < TILE:
                    sc_ref[p, r0:r0 + CHUNK, e * TILE + ncol:(e + 1) * TILE] = jnp.zeros(
                        (CHUNK, TILE - ncol), BF16)
                inter_w = jnp.exp(inter_log - m_row)
                den = jnp.sum(sc, axis=1, keepdims=True) + inter_w * den_inter[r0:r0 + CHUNK]
                den_blocks.append(jnp.maximum(jnp.abs(den), jnp.exp(-m_row)))
                iw_blocks.append(inter_w)
            iws.append(jnp.concatenate(iw_blocks, axis=0))
            dens.append(jnp.concatenate(den_blocks, axis=0))
            g_row = bt_all[TILE - 1:TILE, :]
            a_row = g_row[:, 0:1] - bj + ij
            m_new = jnp.maximum(g_row + mprev, jnp.max(a_row, axis=1, keepdims=True))
            wcols.append(jnp.exp(g_row - bt_all + it_all - m_new))
            solds.append(jnp.exp(g_row + mprev - m_new))
            ml_m[h] = jnp.broadcast_to(m_new, (8, LANES))
        num = _dot(sc_ref[p], ml_vv[p]) + jnp.where(m0, iws[0], iws[1]) * ml_ni[p]
        hcur = num / jnp.where(m0, dens[0], dens[1])
        y_ref[:, ATT_W + c0:ATT_W + c0 + LANES] = (
            jax.nn.sigmoid(zc[:, MO + c0:MO + c0 + LANES])
            * _pair_rms(hcur, mlg_ref[:, c0:c0 + LANES], m0)
            * _silu(zc[:, MG + c0:MG + c0 + LANES]))
        kw = k2 * jnp.where(m0, wcols[0], wcols[1])
        upd = _dot_tn(kw.astype(BF16), ml_vb[p])
        sold_col = jnp.where(prow_first, solds[0][:, 0:1], solds[1][:, 0:1])
        ml_c[p] = jnp.where(pair_diag, sold_col * ml_ct[p] + upd, 0.0)
        nvec = jnp.where(m0, solds[0], solds[1]) * nvec + jnp.sum(kw, axis=0, keepdims=True)
        ml_n[p] = jnp.broadcast_to(nvec, (8, LANES))
        ip(1)

    grow = lax.broadcasted_iota(jnp.int32, (LANES, GLA_VW), 0)
    gcol = lax.broadcasted_iota(jnp.int32, (LANES, GLA_VW), 1)
    gla_diag = (grow >> 5) == (gcol >> 6)
    khead = lane >> 5
    vhead = lax.broadcasted_iota(jnp.int32, (1, GLA_VW), 1) >> 6
    arow = lax.broadcasted_iota(jnp.int32, (CHUNK, GLA_VW), 0)
    acol = lax.broadcasted_iota(jnp.int32, (CHUNK, GLA_VW), 1)
    causal4 = arow >= (acol & (CHUNK - 1))
    n_gh = GLA_KW // GLA_DK
    yc0 = ATT_W + ML_W
    g_a, g_upd, g_qi, g_vvg, g_dec = [], [], [], [], []
    for c in range(NCH):
        r0 = c * CHUNK
        bc = bgl[r0:r0 + CHUNK, :]
        bref = bc[CHUNK // 2 - 1:CHUNK // 2, :]
        btot = bc[CHUNK - 1:CHUNK, :]
        gq = zc[r0:r0 + CHUNK, GQ:GQ + GLA_KW] * (GLA_DK ** -0.5)
        gk = zc[r0:r0 + CHUNK, GK:GK + GLA_KW]
        gv = zc[r0:r0 + CHUNK, GV:GV + GLA_VW]
        qe = (gq * jnp.exp(bc - bref)).astype(BF16)
        ke = gk * jnp.exp(bref - bc)
        kek = jnp.concatenate([jnp.where(khead == h, ke, 0.0) for h in range(n_gh)],
                              axis=0).astype(BF16)
        g_vvg.append(jnp.concatenate([jnp.where(vhead == h, gv, 0.0) for h in range(n_gh)],
                                     axis=0).astype(BF16))
        g_qi.append((gq * jnp.exp(bc)).astype(BF16))
        kd = (gk * jnp.exp(btot - bc)).astype(BF16)
        g_dec.append(jnp.exp(bgl_t[:, r0 + CHUNK - 1:r0 + CHUNK]))
        g_a.append(_dot_nt(qe, kek))
        g_upd.append(_dot_tn(kd, gv.astype(BF16)))
    ip(2)
    s_all = gla_s[...]
    for c in range(NCH):
        r0 = c * CHUNK
        a = jnp.where(causal4, g_a[c], 0.0)
        o = _dot(g_qi[c], s_all.astype(BF16)) + _dot(a.astype(BF16), g_vvg[c])
        y_ref[r0:r0 + CHUNK, yc0:yc0 + GLA_VW] = o
        s_all = jnp.where(gla_diag, g_dec[c] * s_all + g_upd[c], 0.0)
        ip(1)
    gla_s[...] = s_all

    for hp in range(GLA_VW // LANES):
        c0 = hp * LANES
        hcur = y_ref[:, yc0 + c0:yc0 + c0 + LANES]
        y_ref[:, yc0 + c0:yc0 + c0 + LANES] = (
            _pair_rms(hcur, glg_ref[:, c0:c0 + LANES], m0) * _silu(zc[:, GG + c0:GG + c0 + LANES]))
    ip(D_INP // IP_BLK)

    xo = xc_ref[...] + _dot(y_ref[...].astype(BF16), wout_ref[...])
    if final_norm:
        xo = xo * lax.rsqrt(jnp.sum(xo * xo, axis=1, keepdims=True) * (1.0 / D_MODEL) + EPS) * fg_ref[...]
    o_ref[...] = xo


def _project(x, ng_ref, win_ref, z_out):
    hn = x * lax.rsqrt(jnp.sum(x * x, axis=1, keepdims=True) * (1.0 / D_MODEL) + EPS) * ng_ref[...]
    hn_b = hn.astype(BF16)
    for n0 in range(0, D_INP, IP_BLK):
        z_out[:, n0:n0 + IP_BLK] = _dot_nt(hn_b, win_ref[n0:n0 + IP_BLK, :])


def _layer_kernel(xa_ref, xn_ref, ng_ref, win_ref, gb_ref, conv_ref, wal_ref, bal_ref, gp_ref,
                  mlg_ref, glg_ref, wout_ref, fg_ref,
                  o_ref,
                  z_a, z_b, kbuf, vbuf, bias_ref, cbuf, ml_c, ml_n, ml_m, gla_s, y_ref, sc_ref,
                  *, final_norm, tiles_per_seq):
    g = pl.program_id(0)
    i = lax.rem(2 * g, tiles_per_seq)

    @pl.when(g == 0)
    def _first_step():
        r = lax.broadcasted_iota(jnp.int32, (TILE, KEYS), 0)
        m = lax.broadcasted_iota(jnp.int32, (TILE, KEYS), 1)
        jj = m - ((r >> 6) << 6)
        in_band = (jj >= 0) & (jj < BAND)
        for h in range(ML_HEADS):
            row = jnp.broadcast_to(gp_ref[h:h + 1, :], (TILE, ROLL_W))
            rolled = pltpu.roll(row, KEYS, 1, stride=1, stride_axis=0)
            bias_ref[h] = jnp.where(in_band, rolled[:, :KEYS], NEG)
        _project(xa_ref[0:TILE, :], ng_ref, win_ref, z_a)

    @pl.when(i == 0)
    def _reset():
        kbuf[...] = jnp.zeros(kbuf.shape, kbuf.dtype)
        vbuf[...] = jnp.zeros(vbuf.shape, vbuf.dtype)
        cbuf[0:8, :] = jnp.zeros((8, cbuf.shape[1]), F32)
        ml_c[...] = jnp.zeros(ml_c.shape, F32)
        ml_n[...] = jnp.zeros(ml_n.shape, F32)
        ml_m[...] = jnp.zeros(ml_m.shape, F32)
        gla_s[...] = jnp.zeros(gla_s.shape, F32)

    step = functools.partial(
        _step, ng_ref=ng_ref, win_ref=win_ref, gb_ref=gb_ref,
        conv_ref=conv_ref, wal_ref=wal_ref, bal_ref=bal_ref, mlg_ref=mlg_ref, glg_ref=glg_ref,
        wout_ref=wout_ref, fg_ref=fg_ref, kbuf=kbuf, vbuf=vbuf, bias_ref=bias_ref,
        cbuf=cbuf, ml_c=ml_c, ml_n=ml_n, ml_m=ml_m, gla_s=gla_s, y_ref=y_ref, sc_ref=sc_ref,
        final_norm=final_norm)

    lo = pl.ds(0, TILE)
    hi = pl.ds(TILE, TILE)
    step(i, z_a, z_b, xn_ref=xa_ref.at[hi], xc_ref=xa_ref.at[lo], o_ref=o_ref.at[lo])
    step(i + 1, z_b, z_a, xn_ref=xn_ref, xc_ref=xa_ref.at[hi], o_ref=o_ref.at[hi])


def _const_spec(shape):
    nd = len(shape)
    return pl.BlockSpec(shape, lambda g, _nd=nd: (0,) * _nd, pipeline_mode=pl.Buffered(1))


def _layer_call(x2d, prm, final_g, final_norm, tiles_per_seq):
    n_tiles = x2d.shape[0] // TILE
    n_steps = n_tiles // 2
    args = (x2d, x2d, prm["ng"], prm["win"], prm["gb"], prm["conv"], prm["wal"], prm["bal"], prm["gp"],
            prm["mlg"], prm["glg"], prm["wout"], final_g)
    cur_spec = pl.BlockSpec((2 * TILE, D_MODEL), lambda g: (g, 0))
    nxt_spec = pl.BlockSpec((TILE, D_MODEL), lambda g: (jnp.minimum(2 * g + 2, n_tiles - 1), 0))
    in_specs = [cur_spec, nxt_spec] + [_const_spec(a.shape) for a in args[2:]]
    scratch = [
        pltpu.VMEM((TILE, D_INP), F32),
        pltpu.VMEM((TILE, D_INP), F32),
        pltpu.VMEM((KEYS, ATT_W), BF16),
        pltpu.VMEM((KEYS, ATT_W), BF16),
        pltpu.VMEM((ML_HEADS, TILE, KEYS), F32),
        pltpu.VMEM((TILE + 16, 2 * ML_W), F32),
        pltpu.VMEM((N_PAIRS, LANES, LANES), F32),
        pltpu.VMEM((N_PAIRS, 8, LANES), F32),
        pltpu.VMEM((ML_HEADS, 8, LANES), F32),
        pltpu.VMEM((LANES, GLA_VW), F32),
        pltpu.VMEM((TILE, D_MODEL), F32),
        pltpu.VMEM((N_PAIRS, TILE, 2 * TILE), BF16),
    ]
    return pl.pallas_call(
        functools.partial(_layer_kernel, final_norm=final_norm, tiles_per_seq=tiles_per_seq),
        out_shape=jax.ShapeDtypeStruct(x2d.shape, x2d.dtype),
        grid=(n_steps,),
        in_specs=in_specs,
        out_specs=cur_spec,
        scratch_shapes=scratch,
        compiler_params=pltpu.CompilerParams(
            dimension_semantics=("arbitrary",),
            vmem_limit_bytes=VMEM_LIMIT),
        name="hybrid_layer_final" if final_norm else "hybrid_layer",
    )(*args)


def _pack_w_in(w_in):
    w3 = jnp.transpose(w_in, (2, 0, 1))
    pad = jnp.zeros((D_INP - SM - GLA_RANK - 2 * ML_HEADS,) + w3.shape[1:], w3.dtype)
    w3 = jnp.concatenate([w3[:3072], w3[3084:3980], w3[3996:4252],
                          w3[3980:3996], w3[3072:3084], pad], axis=0)
    return lax.optimization_barrier(w3)


def _prep_layer(norm_g, w_in_packed, b_gates, conv_w, w_alpha, b_alpha, rel_bias, ml_norm_g, gla_norm_g, w_out):
    win = w_in_packed.astype(BF16)
    gb =jnp.zeros((1, LANES), F32).at[0, SM_I:SM_I + 2 * ML_HEADS].set(b_gates)
    wal = jnp.zeros((LANES, GLA_KW), F32).at[:GLA_RANK, :].set(w_alpha).astype(BF16)
    nh = rel_bias.shape[0]
    gp = jnp.concatenate([
        jnp.broadcast_to(rel_bias[:, 2 * REL_CLIP:], (nh, KEYS - REL_CLIP + 1)),
        rel_bias[:, 2 * REL_CLIP - 1:0:-1],
        jnp.broadcast_to(rel_bias[:, :1], (nh, ROLL_W - KEYS - REL_CLIP)),
    ], axis=1)
    return dict(ng=norm_g.reshape(1, D_MODEL), win=win, gb=gb, conv=conv_w, wal=wal,
                bal=b_alpha.reshape(1, GLA_KW), gp=gp, mlg=ml_norm_g.reshape(1, ML_W),
                glg=gla_norm_g.reshape(1, GLA_VW), wout=w_out.astype(BF16))


def kernel(x, norm_g, w_in, b_gates, conv_w, w_alpha, b_alpha, rel_bias, ml_norm_g, gla_norm_g, w_out, final_g):
    depth = norm_g.shape[0]
    bsz, seq, _ = x.shape
    fg = final_g.reshape(1, D_MODEL)
    w_in_p = _pack_w_in(w_in)
    x2d = x.reshape(bsz * seq, D_MODEL)
    for l in range(depth):
        prm = _prep_layer(norm_g[l], w_in_p[:, l, :], b_gates[l], conv_w[l], w_alpha[l], b_alpha[l], rel_bias[l],
                          ml_norm_g[l], gla_norm_g[l], w_out[l])
        x2d = _layer_call(x2d, prm, fg, final_norm=(l == depth - 1), tiles_per_seq=seq // TILE)
    return x2d.reshape(bsz, seq, D_MODEL)
```

```python
import functools

import jax
import jax.numpy as jnp
from jax import lax
from jax.experimental import pallas as pl
from jax.experimental.pallas import tpu as pltpu

F32 = jnp.float32
BF16 = jnp.bfloat16

D_MODEL = 1024
CHUNK = 64
HEAD_DIM = 64
ATT_W = 384
ML_W = 384
GLA_KW = 128
GLA_VW = 256
GLA_DK = 32
GLA_RANK = 16
GLA_TAU = 16.0
ML_HEADS = 6
PAST_CHUNKS = 8
REL_CLIP = 128
CONV_W = 4
EPS = 1e-6
NEG = -1e30

LANES = 128
TILE = 256
NCH = TILE // CHUNK
HIST = PAST_CHUNKS * CHUNK
KEYS = HIST + TILE
HALF_ROWS = TILE // 2
HALF_KEYS = HIST + HALF_ROWS
BAND = (PAST_CHUNKS + 1) * CHUNK
ROLL_W = 1024
N_PAIRS = ML_W // LANES

AQ, AK, AV, AG = 0, 384, 768, 1152
MQ, MK, MV, MO, MG = 1536, 1920, 2304, 2688, 3072
GQ, GK, GV, GG = 3456, 3584, 3712, 3968
SM = 4224
D_INP = 4352
IP_BLK = 256
W_MAIN = 3072
SM_I = GLA_RANK
SM_F = GLA_RANK + ML_HEADS

VMEM_LIMIT = 56 * 1024 * 1024


def _log_sigmoid(x):
    return jnp.minimum(x, 0.0) - jnp.log1p(jnp.exp(-jnp.abs(x)))


def _silu(x):
    return x * jax.nn.sigmoid(x)


def _dot_nt(a, b):
    return lax.dot_general(a, b, (((1,), (1,)), ((), ())), preferred_element_type=F32)


def _dot_tn(a, b):
    return lax.dot_general(a, b, (((0,), (0,)), ((), ())), preferred_element_type=F32)


def _dot(a, b):
    return jnp.dot(a, b, preferred_element_type=F32)


def _pair_rms(hcur, gain, m0):
    sq = hcur * hcur
    ms0 = jnp.sum(jnp.where(m0, sq, 0.0), axis=1, keepdims=True)
    ms1 = jnp.sum(jnp.where(m0, 0.0, sq), axis=1, keepdims=True)
    ms = jnp.where(m0, ms0, ms1) * (1.0 / HEAD_DIM)
    return hcur * lax.rsqrt(ms + EPS) * gain


def _step(i, zc, zn, xn_ref, xc_ref, ng_ref, w_rows, gb_ref, conv_ref, wal_ref, bal_ref,
          mlg_ref, glg_ref, wout_ref, fg_ref, o_ref,
          kbuf, vbuf, bias_ref, cbuf, ml_c, ml_n, ml_m, gla_s, y_ref, sc_ref, *, final_norm):
    lane = lax.broadcasted_iota(jnp.int32, (1, LANES), 1)
    m0 = lane < HEAD_DIM
    m1 = jnp.logical_not(m0)

    xn = xn_ref[...]
    hn = xn * lax.rsqrt(jnp.sum(xn * xn, axis=1, keepdims=True) * (1.0 / D_MODEL) + EPS) * ng_ref[...]
    hn_b = hn.astype(BF16)
    ip_next = [0]

    def ip(count):
        for _ in range(count):
            n0 = ip_next[0] * IP_BLK
            if n0 < D_INP:
                zn[:, n0:n0 + IP_BLK] = _dot_nt(hn_b, w_rows(n0))
                ip_next[0] += 1

    kbuf[0:HIST, :] = kbuf[TILE:KEYS, :]
    vbuf[0:HIST, :] = vbuf[TILE:KEYS, :]
    kbuf[HIST:KEYS, :] = zc[:, AK:AK + ATT_W].astype(BF16)
    vbuf[HIST:KEYS, :] = zc[:, AV:AV + ATT_W].astype(BF16)

    kcol = lax.broadcasted_iota(jnp.int32, (1, KEYS), 1)
    pen = jnp.where(kcol + (i * TILE - HIST) >= 0, 0.0, NEG)

    def att_scores(u):
        h, half = u // 2, u % 2
        c0 = (h // 2) * LANES
        r0, k0 = half * HALF_ROWS, half * HALF_ROWS
        q2 = zc[r0:r0 + HALF_ROWS, AQ + c0:AQ + c0 + LANES] * (HEAD_DIM ** -0.5)
        qm = jnp.where(m0 if h % 2 == 0 else m1, q2, 0.0).astype(BF16)
        return (_dot_nt(qm, kbuf[k0:k0 + HALF_KEYS, c0:c0 + LANES])
                + bias_ref[h, r0:r0 + HALF_ROWS, k0:k0 + HALF_KEYS] + pen[:, k0:k0 + HALF_KEYS])

    def att_out(u, s):
        h, half = u // 2, u % 2
        c0 = (h // 2) * LANES
        k0 = half * HALF_ROWS
        mx = jnp.max(s, axis=1, keepdims=True)
        pe = jnp.exp(s - mx)
        l = jnp.sum(pe, axis=1, keepdims=True)
        return _dot(pe.astype(BF16), vbuf[k0:k0 + HALF_KEYS, c0:c0 + LANES]) / l

    n_units = 2 * ML_HEADS
    s_cur = att_scores(0)
    outs = []
    for u in range(n_units):
        s_nxt = att_scores(u + 1) if u + 1 < n_units else None
        if u % 2 == 1:
            ip(1)
        outs.append(att_out(u, s_cur))
        s_cur = s_nxt
        if u % 4 == 3:
            c0 = (u // 4) * LANES
            att = jnp.where(m0, jnp.concatenate(outs[u - 3:u - 1], axis=0),
                            jnp.concatenate(outs[u - 1:u + 1], axis=0))
            y_ref[:, c0:c0 + LANES] = att * _silu(zc[:, AG + c0:AG + c0 + LANES])
    ip(1)

    cbuf[8:8 + TILE, :] = zc[:, MQ:MQ + 2 * ML_W]
    for c0 in range(0, 2 * ML_W, LANES):
        acc = cbuf[8:8 + TILE, c0:c0 + LANES] * conv_ref[CONV_W - 1:CONV_W, c0:c0 + LANES]
        for sft in range(1, CONV_W):
            acc = acc + cbuf[8 - sft:8 - sft + TILE, c0:c0 + LANES] * \
                conv_ref[CONV_W - 1 - sft:CONV_W - sft, c0:c0 + LANES]
        act = _silu(acc)
        if c0 >= ML_W:
            act = act * (HEAD_DIM ** -0.5)
        zc[:, MQ + c0:MQ + c0 + LANES] = act
    cbuf[0:8, :] = cbuf[TILE:TILE + 8, :]

    small = zc[:, SM:SM + LANES]
    pre = small + gb_ref[...]
    lf = _log_sigmoid(pre)
    la = _log_sigmoid(_dot(small.astype(BF16), wal_ref[...]) + bal_ref[...]) * (1.0 / GLA_TAU)
    tr = lax.broadcasted_iota(jnp.int32, (TILE, TILE), 0)
    tc = lax.broadcasted_iota(jnp.int32, (TILE, TILE), 1)
    tri = jnp.where(tr >= tc, 1.0, 0.0).astype(BF16)
    terms = []
    for v in (lf, la):
        hi = v.astype(BF16)
        r1 = v - hi.astype(F32)
        mid = r1.astype(BF16)
        terms += [hi, mid, (r1 - mid.astype(F32)).astype(BF16)]
    cs = _dot(tri, jnp.concatenate(terms, axis=1))
    bml = (cs[:, 2 * LANES:3 * LANES] + cs[:, LANES:2 * LANES]) + cs[:, 0:LANES]
    bla = (cs[:, 5 * LANES:6 * LANES] + cs[:, 4 * LANES:5 * LANES]) + cs[:, 3 * LANES:4 * LANES]
    bgl = jnp.concatenate(
        [bla[0:CHUNK]] + [bla[c * CHUNK:(c + 1) * CHUNK] - bla[c * CHUNK - 1:c * CHUNK]
                          for c in range(1, NCH)], axis=0)
    pre_t = pre.T
    bml_t = bml.T
    bgl_t = bgl.T
    ip(1)

    prow = lax.broadcasted_iota(jnp.int32, (LANES, LANES), 0)
    pcol = lax.broadcasted_iota(jnp.int32, (LANES, LANES), 1)
    pair_diag = (prow < HEAD_DIM) == (pcol < HEAD_DIM)
    prow_first = prow[:, 0:1] < HEAD_DIM

    ml_q, ml_k, ml_vb, ml_vv, ml_s, ml_ni, ml_ct = [], [], [], [], [], [], []
    for p in range(N_PAIRS):
        c0 = p * LANES
        q2 = zc[:, MQ + c0:MQ + c0 + LANES]
        k2 = zc[:, MK + c0:MK + c0 + LANES]
        v2 = zc[:, MV + c0:MV + c0 + LANES]
        q2b = q2.astype(BF16)
        kk = jnp.concatenate([jnp.where(m0, k2, 0.0), jnp.where(m0, 0.0, k2)], axis=0).astype(BF16)
        ct = ml_c[p]
        ml_q.append(q2)
        ml_k.append(k2)
        ml_vb.append(v2.astype(BF16))
        ml_vv.append(jnp.concatenate([jnp.where(m0, v2, 0.0), jnp.where(m0, 0.0, v2)],
                                     axis=0).astype(BF16))
        ml_ct.append(ct)
        ml_s.append(_dot_nt(q2b, kk))
        ml_ni.append(_dot(q2b, ct.astype(BF16)))
    ip(2)

    for p in range(N_PAIRS):
        c0 = p * LANES
        q2, k2, s_pair = ml_q[p], ml_k[p], ml_s[p]
        nvec = ml_n[p][0:1, :]
        qn = q2 * nvec
        iws, dens, wcols, solds = [], [], [], []
        for e in range(2):
            h = 2 * p + e
            mprev = ml_m[h][0:1, :]
            bt_all = jnp.broadcast_to(bml[:, SM_F + h:SM_F + h + 1], (TILE, LANES))
            it_all = jnp.broadcast_to(pre[:, SM_I + h:SM_I + h + 1], (TILE, LANES))
            bj = bml_t[SM_F + h:SM_F + h + 1, :]
            ij = pre_t[SM_I + h:SM_I + h + 1, :]
            den_inter = jnp.sum(jnp.where(m0 if e == 0 else m1, qn, 0.0), axis=1, keepdims=True)
            iw_blocks, den_blocks = [], []
            for rb in range(NCH):
                r0 = rb * CHUNK
                ncol = LANES * ((r0 + CHUNK + LANES - 1) // LANES)
                rr = lax.broadcasted_iota(jnp.int32, (CHUNK, ncol), 0) + r0
                kc = lax.broadcasted_iota(jnp.int32, (CHUNK, ncol), 1)
                btc = bt_all[r0:r0 + CHUNK, 0:1]
                log_d = jnp.where(rr >= kc, btc - bj[:, :ncol] + ij[:, :ncol], NEG)
                inter_log = btc + mprev[:, 0:1]
                m_row = jnp.maximum(inter_log, jnp.max(log_d, axis=1, keepdims=True))
                sc = s_pair[r0:r0 + CHUNK, e * TILE:e * TILE + ncol] * jnp.exp(log_d - m_row)
                sc_ref[p, r0:r0 + CHUNK, e * TILE:e * TILE + ncol] = sc.astype(BF16)
                if ncol < TILE:
                    sc_ref[p, r0:r0 + CHUNK, e * TILE + ncol:(e + 1) * TILE] = jnp.zeros(
                        (CHUNK, TILE - ncol), BF16)
                inter_w = jnp.exp(inter_log - m_row)
                den = jnp.sum(sc, axis=1, keepdims=True) + inter_w * den_inter[r0:r0 + CHUNK]
                den_blocks.append(jnp.maximum(jnp.abs(den), jnp.exp(-m_row)))
                iw_blocks.append(inter_w)
            iws.append(jnp.concatenate(iw_blocks, axis=0))
            dens.append(jnp.concatenate(den_blocks, axis=0))
            g_row = bt_all[TILE - 1:TILE, :]
            a_row = g_row[:, 0:1] - bj + ij
            m_new = jnp.maximum(g_row + mprev, jnp.max(a_row, axis=1, keepdims=True))
            wcols.append(jnp.exp(g_row - bt_all + it_all - m_new))
            solds.append(jnp.exp(g_row + mprev - m_new))
            ml_m[h] = jnp.broadcast_to(m_new, (8, LANES))
        num = _dot(sc_ref[p], ml_vv[p]) + jnp.where(m0, iws[0], iws[1]) * ml_ni[p]
        hcur = num / jnp.where(m0, dens[0], dens[1])
        y_ref[:, ATT_W + c0:ATT_W + c0 + LANES] = (
            jax.nn.sigmoid(zc[:, MO + c0:MO + c0 + LANES])
            * _pair_rms(hcur, mlg_ref[:, c0:c0 + LANES], m0)
            * _silu(zc[:, MG + c0:MG + c0 + LANES]))
        kw = k2 * jnp.where(m0, wcols[0], wcols[1])
        upd = _dot_tn(kw.astype(BF16), ml_vb[p])
        sold_col = jnp.where(prow_first, solds[0][:, 0:1], solds[1][:, 0:1])
        ml_c[p] = jnp.where(pair_diag, sold_col * ml_ct[p] + upd, 0.0)
        nvec = jnp.where(m0, solds[0], solds[1]) * nvec + jnp.sum(kw, axis=0, keepdims=True)
        ml_n[p] = jnp.broadcast_to(nvec, (8, LANES))
        ip(1)

    grow = lax.broadcasted_iota(jnp.int32, (LANES, GLA_VW), 0)
    gcol = lax.broadcasted_iota(jnp.int32, (LANES, GLA_VW), 1)
    gla_diag = (grow >> 5) == (gcol >> 6)
    khead = lane >> 5
    vhead = lax.broadcasted_iota(jnp.int32, (1, GLA_VW), 1) >> 6
    arow = lax.broadcasted_iota(jnp.int32, (CHUNK, GLA_VW), 0)
    acol = lax.broadcasted_iota(jnp.int32, (CHUNK, GLA_VW), 1)
    causal4 = arow >= (acol & (CHUNK - 1))
    n_gh = GLA_KW // GLA_DK
    yc0 = ATT_W + ML_W
    g_a, g_upd, g_qi, g_vvg, g_dec = [], [], [], [], []
    for c in range(NCH):
        r0 = c * CHUNK
        bc = bgl[r0:r0 + CHUNK, :]
        bref = bc[CHUNK // 2 - 1:CHUNK // 2, :]
        btot = bc[CHUNK - 1:CHUNK, :]
        gq = zc[r0:r0 + CHUNK, GQ:GQ + GLA_KW] * (GLA_DK ** -0.5)
        gk = zc[r0:r0 + CHUNK, GK:GK + GLA_KW]
        gv = zc[r0:r0 + CHUNK, GV:GV + GLA_VW]
        qe = (gq * jnp.exp(bc - bref)).astype(BF16)
        ke = gk * jnp.exp(bref - bc)
        kek = jnp.concatenate([jnp.where(khead == h, ke, 0.0) for h in range(n_gh)],
                              axis=0).astype(BF16)
        g_vvg.append(jnp.concatenate([jnp.where(vhead == h, gv, 0.0) for h in range(n_gh)],
                                     axis=0).astype(BF16))
        g_qi.append((gq * jnp.exp(bc)).astype(BF16))
        kd = (gk * jnp.exp(btot - bc)).astype(BF16)
        g_dec.append(jnp.exp(bgl_t[:, r0 + CHUNK - 1:r0 + CHUNK]))
        g_a.append(_dot_nt(qe, kek))
        g_upd.append(_dot_tn(kd, gv.astype(BF16)))
    ip(2)
    s_all = gla_s[...]
    for c in range(NCH):
        r0 = c * CHUNK
        a = jnp.where(causal4, g_a[c], 0.0)
        o = _dot(g_qi[c], s_all.astype(BF16)) + _dot(a.astype(BF16), g_vvg[c])
        y_ref[r0:r0 + CHUNK, yc0:yc0 + GLA_VW] = o
        s_all = jnp.where(gla_diag, g_dec[c] * s_all + g_upd[c], 0.0)
        ip(1)
    gla_s[...] = s_all

    for hp in range(GLA_VW // LANES):
        c0 = hp * LANES
        hcur = y_ref[:, yc0 + c0:yc0 + c0 + LANES]
        y_ref[:, yc0 + c0:yc0 + c0 + LANES] = (
            _pair_rms(hcur, glg_ref[:, c0:c0 + LANES], m0) * _silu(zc[:, GG + c0:GG + c0 + LANES]))
    ip(D_INP // IP_BLK)

    xo = xc_ref[...] + _dot(y_ref[...].astype(BF16), wout_ref[...])
    if final_norm:
        xo = xo * lax.rsqrt(jnp.sum(xo * xo, axis=1, keepdims=True) * (1.0 / D_MODEL) + EPS) * fg_ref[...]
    o_ref[...] = xo


def _project(x, ng_ref, w_rows, z_out):
    hn = x * lax.rsqrt(jnp.sum(x * x, axis=1, keepdims=True) * (1.0 / D_MODEL) + EPS) * ng_ref[...]
    hn_b = hn.astype(BF16)
    for n0 in range(0, D_INP, IP_BLK):
        z_out[:, n0:n0 + IP_BLK] = _dot_nt(hn_b, w_rows(n0))


def _layer_kernel(xa_ref, xn_ref, ng_ref, win_ref, wtl_ref, gb_ref, conv_ref, wal_ref, bal_ref, gp_ref,
                  mlg_ref, glg_ref, wout_ref, fg_ref,
                  o_ref,
                  z_a, z_b, kbuf, vbuf, bias_ref, cbuf, ml_c, ml_n, ml_m, gla_s, y_ref, sc_ref,
                  *, final_norm, tiles_per_seq):
    g = pl.program_id(0)
    i = lax.rem(2 * g, tiles_per_seq)

    def w_rows(n0):
        if n0 < W_MAIN:
            return win_ref[n0:n0 + IP_BLK, :]
        return wtl_ref[n0 - W_MAIN:n0 - W_MAIN + IP_BLK, :]

    @pl.when(g == 0)
    def _first_step():
        r = lax.broadcasted_iota(jnp.int32, (TILE, KEYS), 0)
        m = lax.broadcasted_iota(jnp.int32, (TILE, KEYS), 1)
        jj = m - ((r >> 6) << 6)
        in_band = (jj >= 0) & (jj < BAND)
        for h in range(ML_HEADS):
            row = jnp.broadcast_to(gp_ref[h:h + 1, :], (TILE, ROLL_W))
            rolled = pltpu.roll(row, KEYS, 1, stride=1, stride_axis=0)
            bias_ref[h] = jnp.where(in_band, rolled[:, :KEYS], NEG)
        _project(xa_ref[0:TILE, :], ng_ref, w_rows, z_a)

    @pl.when(i == 0)
    def _reset():
        kbuf[...] = jnp.zeros(kbuf.shape, kbuf.dtype)
        vbuf[...] = jnp.zeros(vbuf.shape, vbuf.dtype)
        cbuf[0:8, :] = jnp.zeros((8, cbuf.shape[1]), F32)
        ml_c[...] = jnp.zeros(ml_c.shape, F32)
        ml_n[...] = jnp.zeros(ml_n.shape, F32)
        ml_m[...] = jnp.zeros(ml_m.shape, F32)
        gla_s[...] = jnp.zeros(gla_s.shape, F32)

    step = functools.partial(
        _step, ng_ref=ng_ref, w_rows=w_rows, gb_ref=gb_ref,
        conv_ref=conv_ref, wal_ref=wal_ref, bal_ref=bal_ref, mlg_ref=mlg_ref, glg_ref=glg_ref,
        wout_ref=wout_ref, fg_ref=fg_ref, kbuf=kbuf, vbuf=vbuf, bias_ref=bias_ref,
        cbuf=cbuf, ml_c=ml_c, ml_n=ml_n, ml_m=ml_m, gla_s=gla_s, y_ref=y_ref, sc_ref=sc_ref,
        final_norm=final_norm)

    lo = pl.ds(0, TILE)
    hi = pl.ds(TILE, TILE)
    step(i, z_a, z_b, xn_ref=xa_ref.at[hi], xc_ref=xa_ref.at[lo], o_ref=o_ref.at[lo])
    step(i + 1, z_b, z_a, xn_ref=xn_ref, xc_ref=xa_ref.at[hi], o_ref=o_ref.at[hi])


def _const_spec(shape):
    nd = len(shape)
    return pl.BlockSpec(shape, lambda g, _nd=nd: (0,) * _nd, pipeline_mode=pl.Buffered(1))


def _layer_call(x2d, prm, final_g, final_norm, tiles_per_seq):
    n_tiles = x2d.shape[0] // TILE
    n_steps = n_tiles // 2
    args = (x2d, x2d, prm["ng"], prm["win"], prm["wtl"], prm["gb"], prm["conv"], prm["wal"], prm["bal"], prm["gp"],
            prm["mlg"], prm["glg"], prm["wout"], final_g)
    cur_spec = pl.BlockSpec((2 * TILE, D_MODEL), lambda g: (g, 0))
    nxt_spec = pl.BlockSpec((TILE, D_MODEL), lambda g: (jnp.minimum(2 * g + 2, n_tiles - 1), 0))
    in_specs = [cur_spec, nxt_spec] + [_const_spec(a.shape) for a in args[2:]]
    scratch = [
        pltpu.VMEM((TILE, D_INP), F32),
        pltpu.VMEM((TILE, D_INP), F32),
        pltpu.VMEM((KEYS, ATT_W), BF16),
        pltpu.VMEM((KEYS, ATT_W), BF16),
        pltpu.VMEM((ML_HEADS, TILE, KEYS), F32),
        pltpu.VMEM((TILE + 16, 2 * ML_W), F32),
        pltpu.VMEM((N_PAIRS, LANES, LANES), F32),
        pltpu.VMEM((N_PAIRS, 8, LANES), F32),
        pltpu.VMEM((ML_HEADS, 8, LANES), F32),
        pltpu.VMEM((LANES, GLA_VW), F32),
        pltpu.VMEM((TILE, D_MODEL), F32),
        pltpu.VMEM((N_PAIRS, TILE, 2 * TILE), BF16),
    ]
    return pl.pallas_call(
        functools.partial(_layer_kernel, final_norm=final_norm, tiles_per_seq=tiles_per_seq),
        out_shape=jax.ShapeDtypeStruct(x2d.shape, x2d.dtype),
        grid=(n_steps,),
        in_specs=in_specs,
        out_specs=cur_spec,
        scratch_shapes=scratch,
        compiler_params=pltpu.CompilerParams(
            dimension_semantics=("arbitrary",),
            vmem_limit_bytes=VMEM_LIMIT),
        name="hybrid_layer_final" if final_norm else "hybrid_layer",
    )(*args)


def _prep_layer(norm_g, w_in_t, b_gates, conv_w, w_alpha, b_alpha, rel_bias, ml_norm_g, gla_norm_g, w_out):
    pad = jnp.zeros((D_INP - SM - GLA_RANK - 2 * ML_HEADS, D_MODEL), w_in_t.dtype)
    win = w_in_t[:W_MAIN].astype(BF16)
    wtl = jnp.concatenate([w_in_t[3084:3980], w_in_t[3996:4252], w_in_t[3980:3996],
                           w_in_t[3072:3084], pad], axis=0).astype(BF16)
    gb = jnp.zeros((1, LANES), F32).at[0, SM_I:SM_I + 2 * ML_HEADS].set(b_gates)
    wal = jnp.zeros((LANES, GLA_KW), F32).at[:GLA_RANK, :].set(w_alpha).astype(BF16)
    nh = rel_bias.shape[0]
    gp = jnp.concatenate([
        jnp.broadcast_to(rel_bias[:, 2 * REL_CLIP:], (nh, KEYS - REL_CLIP + 1)),
        rel_bias[:, 2 * REL_CLIP - 1:0:-1],
        jnp.broadcast_to(rel_bias[:, :1], (nh, ROLL_W - KEYS - REL_CLIP)),
    ], axis=1)
    return dict(ng=norm_g.reshape(1, D_MODEL), win=win, wtl=wtl, gb=gb, conv=conv_w, wal=wal,
                bal=b_alpha.reshape(1, GLA_KW), gp=gp, mlg=ml_norm_g.reshape(1, ML_W),
                glg=gla_norm_g.reshape(1, GLA_VW), wout=w_out.astype(BF16))


def kernel(x, norm_g, w_in, b_gates, conv_w, w_alpha, b_alpha, rel_bias, ml_norm_g, gla_norm_g, w_out, final_g):
    depth = norm_g.shape[0]
    bsz, seq, _ = x.shape
    fg = final_g.reshape(1, D_MODEL)
    w_in_t = jnp.transpose(w_in, (2, 0, 1))
    x2d = x.reshape(bsz * seq, D_MODEL)
    for l in range(depth):
        prm = _prep_layer(norm_g[l], w_in_t[:, l, :], b_gates[l], conv_w[l], w_alpha[l], b_alpha[l], rel_bias[l],
                          ml_norm_g[l], gla_norm_g[l], w_out[l])
        x2d = _layer_call(x2d, prm, fg, final_norm=(l == depth - 1), tiles_per_seq=seq // TILE)
    return x2d.reshape(bsz, seq, D_MODEL)
```

```python
import functools

import jax
import jax.numpy as jnp
from jax import lax
from jax.experimental import pallas as pl
from jax.experimental.pallas import tpu as pltpu

F32 = jnp.float32
BF16 = jnp.bfloat16

D_MODEL = 1024
CHUNK = 64
HEAD_DIM = 64
ATT_W = 384
ML_W = 384
GLA_KW = 128
GLA_VW = 256
GLA_DK = 32
GLA_RANK = 16
GLA_TAU = 16.0
ML_HEADS = 6
PAST_CHUNKS = 8
REL_CLIP = 128
CONV_W = 4
EPS = 1e-6
NEG = -1e30
LOG2E = 1.4426950408889634

LANES = 128
TILE = 256
NCH = TILE // CHUNK
HIST = PAST_CHUNKS * CHUNK
KEYS = HIST + TILE
HALF_ROWS = TILE // 2
HALF_KEYS = HIST + HALF_ROWS
PEN_LANE_A = HEAD_DIM
PEN_LANE_B = 0
BAND = (PAST_CHUNKS + 1) * CHUNK
ROLL_W = 1024
N_PAIRS = ML_W // LANES

AQ, AK, AV, AG = 0, 384, 768, 1152
MQ, MK, MV, MO, MG = 1536, 1920, 2304, 2688, 3072
GQ, GK, GV, GG = 3456, 3584, 3712, 3968
SM = 4224
D_INP = 4352
IP_BLK = 256
W_MAIN = 3072
SM_I = GLA_RANK
SM_F = GLA_RANK + ML_HEADS

VMEM_LIMIT = 56 * 1024 * 1024


def _log_sigmoid(x):
    return jnp.minimum(x, 0.0) - jnp.log1p(jnp.exp(-jnp.abs(x)))


def _silu(x):
    return x * jax.nn.sigmoid(x)


def _dot_nt(a, b):
    return lax.dot_general(a, b, (((1,), (1,)), ((), ())), preferred_element_type=F32)


def _dot_tn(a, b):
    return lax.dot_general(a, b, (((0,), (0,)), ((), ())), preferred_element_type=F32)


def _dot(a, b):
    return jnp.dot(a, b, preferred_element_type=F32)


def _pair_rms(hcur, gain, m0):
    sq = hcur * hcur
    ms0 = jnp.sum(jnp.where(m0, sq, 0.0), axis=1, keepdims=True)
    ms1 = jnp.sum(jnp.where(m0, 0.0, sq), axis=1, keepdims=True)
    ms = jnp.where(m0, ms0, ms1) * (1.0 / HEAD_DIM)
    return hcur * lax.rsqrt(ms + EPS) * gain


def _step(i, zc, zn, xn_ref, xc_ref, ng_ref, w_rows, gb_ref, conv_ref, wal_ref, bal_ref,
          mlg_ref, glg_ref, wout_ref, fg_ref, o_ref,
          kbuf, vbuf, bias_ref, cbuf, ml_c, ml_n, ml_m, gla_s, y_ref, sc_ref, *, final_norm):
    lane = lax.broadcasted_iota(jnp.int32, (1, LANES), 1)
    m0 = lane < HEAD_DIM
    m1 = jnp.logical_not(m0)

    xn = xn_ref[...]
    hn = xn * lax.rsqrt(jnp.sum(xn * xn, axis=1, keepdims=True) * (1.0 / D_MODEL) + EPS) * ng_ref[...]
    hn_b = hn.astype(BF16)
    ip_next = [0]

    def ip(count):
        for _ in range(count):
            n0 = ip_next[0] * IP_BLK
            if n0 < D_INP:
                zn[:, n0:n0 + IP_BLK] = _dot_nt(hn_b, w_rows(n0))
                ip_next[0] += 1

    kbuf[0:HIST, :] = kbuf[TILE:KEYS, :]
    vbuf[0:HIST, :] = vbuf[TILE:KEYS, :]
    for p in range(N_PAIRS):
        c0 = p * LANES
        kn = zc[:, AK + c0:AK + c0 + LANES]
        vn = zc[:, AV + c0:AV + c0 + LANES]
        kbuf[HIST:KEYS, c0:c0 + LANES] = jnp.where(m0, kn, 0.0).astype(BF16)
        kbuf[HIST:KEYS, ATT_W + c0:ATT_W + c0 + LANES] = jnp.where(m0, 0.0, kn).astype(BF16)
        vbuf[HIST:KEYS, c0:c0 + LANES] = jnp.where(m0, vn, 1.0).astype(BF16)
        vbuf[HIST:KEYS, ATT_W + c0:ATT_W + c0 + LANES] = jnp.where(m0, 1.0, vn).astype(BF16)

    def att_scores(u):
        h, half = u // 2, u % 2
        c0 = (h // 2) * LANES
        cb = (h % 2) * ATT_W + c0
        r0, k0 = half * HALF_ROWS, half * HALF_ROWS
        q2 = zc[r0:r0 + HALF_ROWS, AQ + c0:AQ + c0 + LANES] * (HEAD_DIM ** -0.5 * LOG2E)
        if h % 2 == 0:
            qm = jnp.where(m0, q2, jnp.where(lane == PEN_LANE_A, 1.0, 0.0))
        else:
            qm = jnp.where(m1, q2, jnp.where(lane == PEN_LANE_B, 1.0, 0.0))
        return (_dot_nt(qm.astype(BF16), kbuf[k0:k0 + HALF_KEYS, cb:cb + LANES])
                + bias_ref[h, r0:r0 + HALF_ROWS, k0:k0 + HALF_KEYS])

    def att_out(u, s):
        h, half = u // 2, u % 2
        cb = (h % 2) * ATT_W + (h // 2) * LANES
        k0 = half * HALF_ROWS
        pe = jnp.exp2(s - jnp.max(s, axis=1, keepdims=True))
        return _dot(pe.astype(BF16), vbuf[k0:k0 + HALF_KEYS, cb:cb + LANES])

    n_units = 2 * ML_HEADS
    s_cur = att_scores(0)
    outs = []
    for u in range(n_units):
        s_nxt = att_scores(u + 1) if u + 1 < n_units else None
        if u % 2 == 1:
            ip(1)
        outs.append(att_out(u, s_cur))
        s_cur = s_nxt
        if u % 4 == 3:
            c0 = (u // 4) * LANES
            r_a = jnp.concatenate(outs[u - 3:u - 1], axis=0)
            r_b = jnp.concatenate(outs[u - 1:u + 1], axis=0)
            den = pltpu.roll(jnp.where(m0, r_b, r_a), HEAD_DIM, 1)
            att = jnp.where(m0, r_a, r_b) / den
            y_ref[:, c0:c0 + LANES] = att * _silu(zc[:, AG + c0:AG + c0 + LANES])
    ip(1)

    cbuf[8:8 + TILE, :] = zc[:, MQ:MQ + 2 * ML_W]
    for c0 in range(0, 2 * ML_W, LANES):
        acc = cbuf[8:8 + TILE, c0:c0 + LANES] * conv_ref[CONV_W - 1:CONV_W, c0:c0 + LANES]
        for sft in range(1, CONV_W):
            acc = acc + cbuf[8 - sft:8 - sft + TILE, c0:c0 + LANES] * \
                conv_ref[CONV_W - 1 - sft:CONV_W - sft, c0:c0 + LANES]
        act = _silu(acc)
        if c0 >= ML_W:
            act = act * (HEAD_DIM ** -0.5)
        zc[:, MQ + c0:MQ + c0 + LANES] = act
    cbuf[0:8, :] = cbuf[TILE:TILE + 8, :]

    small = zc[:, SM:SM + LANES]
    pre = small + gb_ref[...]
    lf = _log_sigmoid(pre)
    la = _log_sigmoid(_dot(small.astype(BF16), wal_ref[...]) + bal_ref[...]) * (1.0 / GLA_TAU)
    tr = lax.broadcasted_iota(jnp.int32, (TILE, TILE), 0)
    tc = lax.broadcasted_iota(jnp.int32, (TILE, TILE), 1)
    tri = jnp.where(tr >= tc, 1.0, 0.0).astype(BF16)
    terms = []
    for v in (lf, la):
        hi = v.astype(BF16)
        r1 = v - hi.astype(F32)
        mid = r1.astype(BF16)
        terms += [hi, mid, (r1 - mid.astype(F32)).astype(BF16)]
    cs = _dot(tri, jnp.concatenate(terms, axis=1))
    bml = (cs[:, 2 * LANES:3 * LANES] + cs[:, LANES:2 * LANES]) + cs[:, 0:LANES]
    bla = (cs[:, 5 * LANES:6 * LANES] + cs[:, 4 * LANES:5 * LANES]) + cs[:, 3 * LANES:4 * LANES]
    bgl = jnp.concatenate(
        [bla[0:CHUNK]] + [bla[c * CHUNK:(c + 1) * CHUNK] - bla[c * CHUNK - 1:c * CHUNK]
                          for c in range(1, NCH)], axis=0)
    pre_t = pre.T
    bml_t = bml.T
    bgl_t = bgl.T
    ip(1)

    prow = lax.broadcasted_iota(jnp.int32, (LANES, LANES), 0)
    pcol = lax.broadcasted_iota(jnp.int32, (LANES, LANES), 1)
    pair_diag = (prow < HEAD_DIM) == (pcol < HEAD_DIM)
    prow_first = prow[:, 0:1] < HEAD_DIM

    ml_q, ml_k, ml_vb, ml_vv, ml_s, ml_ni, ml_ct = [], [], [], [], [], [], []
    for p in range(N_PAIRS):
        c0 = p * LANES
        q2 = zc[:, MQ + c0:MQ + c0 + LANES]
        k2 = zc[:, MK + c0:MK + c0 + LANES]
        v2 = zc[:, MV + c0:MV + c0 + LANES]
        q2b = q2.astype(BF16)
        kk = jnp.concatenate([jnp.where(m0, k2, 0.0), jnp.where(m0, 0.0, k2)], axis=0).astype(BF16)
        ct = ml_c[p]
        ml_q.append(q2)
        ml_k.append(k2)
        ml_vb.append(v2.astype(BF16))
        ml_vv.append(jnp.concatenate([jnp.where(m0, v2, 0.0), jnp.where(m0, 0.0, v2)],
                                     axis=0).astype(BF16))
        ml_ct.append(ct)
        ml_s.append(_dot_nt(q2b, kk))
        ml_ni.append(_dot(q2b, ct.astype(BF16)))
    ip(2)

    for p in range(N_PAIRS):
        c0 = p * LANES
        q2, k2, s_pair = ml_q[p], ml_k[p], ml_s[p]
        nvec = ml_n[p][0:1, :]
        qn = q2 * nvec
        iws, dens, wcols, solds = [], [], [], []
        for e in range(2):
            h = 2 * p + e
            mprev = ml_m[h][0:1, :]
            bt_all = jnp.broadcast_to(bml[:, SM_F + h:SM_F + h + 1], (TILE, LANES))
            it_all = jnp.broadcast_to(pre[:, SM_I + h:SM_I + h + 1], (TILE, LANES))
            bj = bml_t[SM_F + h:SM_F + h + 1, :]
            ij = pre_t[SM_I + h:SM_I + h + 1, :]
            den_inter = jnp.sum(jnp.where(m0 if e == 0 else m1, qn, 0.0), axis=1, keepdims=True)
            iw_blocks, den_blocks = [], []
            for rb in range(NCH):
                r0 = rb * CHUNK
                ncol = LANES * ((r0 + CHUNK + LANES - 1) // LANES)
                rr = lax.broadcasted_iota(jnp.int32, (CHUNK, ncol), 0) + r0
                kc = lax.broadcasted_iota(jnp.int32, (CHUNK, ncol), 1)
                btc = bt_all[r0:r0 + CHUNK, 0:1]
                log_d = jnp.where(rr >= kc, btc - bj[:, :ncol] + ij[:, :ncol], NEG)
                inter_log = btc + mprev[:, 0:1]
                m_row = jnp.maximum(inter_log, jnp.max(log_d, axis=1, keepdims=True))
                sc = s_pair[r0:r0 + CHUNK, e * TILE:e * TILE + ncol] * jnp.exp(log_d - m_row)
                sc_ref[p, r0:r0 + CHUNK, e * TILE:e * TILE + ncol] = sc.astype(BF16)
                if ncol < TILE:
                    sc_ref[p, r0:r0 + CHUNK, e * TILE + ncol:(e + 1) * TILE] = jnp.zeros(
                        (CHUNK, TILE - ncol), BF16)
                inter_w = jnp.exp(inter_log - m_row)
                den = jnp.sum(sc, axis=1, keepdims=True) + inter_w * den_inter[r0:r0 + CHUNK]
                den_blocks.append(jnp.maximum(jnp.abs(den), jnp.exp(-m_row)))
                iw_blocks.append(inter_w)
            iws.append(jnp.concatenate(iw_blocks, axis=0))
            dens.append(jnp.concatenate(den_blocks, axis=0))
            g_row = bt_all[TILE - 1:TILE, :]
            a_row = g_row[:, 0:1] - bj + ij
            m_new = jnp.maximum(g_row + mprev, jnp.max(a_row, axis=1, keepdims=True))
            wcols.append(jnp.exp(g_row - bt_all + it_all - m_new))
            solds.append(jnp.exp(g_row + mprev - m_new))
            ml_m[h] = jnp.broadcast_to(m_new, (8, LANES))
        num = _dot(sc_ref[p], ml_vv[p]) + jnp.where(m0, iws[0], iws[1]) * ml_ni[p]
        hcur = num / jnp.where(m0, dens[0], dens[1])
        y_ref[:, ATT_W + c0:ATT_W + c0 + LANES] = (
            jax.nn.sigmoid(zc[:, MO + c0:MO + c0 + LANES])
            * _pair_rms(hcur, mlg_ref[:, c0:c0 + LANES], m0)
            * _silu(zc[:, MG + c0:MG + c0 + LANES]))
        kw = k2 * jnp.where(m0, wcols[0], wcols[1])
        upd = _dot_tn(kw.astype(BF16), ml_vb[p])
        sold_col = jnp.where(prow_first, solds[0][:, 0:1], solds[1][:, 0:1])
        ml_c[p] = jnp.where(pair_diag, sold_col * ml_ct[p] + upd, 0.0)
        nvec = jnp.where(m0, solds[0], solds[1]) * nvec + jnp.sum(kw, axis=0, keepdims=True)
        ml_n[p] = jnp.broadcast_to(nvec, (8, LANES))
        ip(1)

    grow = lax.broadcasted_iota(jnp.int32, (LANES, GLA_VW), 0)
    gcol = lax.broadcasted_iota(jnp.int32, (LANES, GLA_VW), 1)
    gla_diag = (grow >> 5) == (gcol >> 6)
    khead = lane >> 5
    vhead = lax.broadcasted_iota(jnp.int32, (1, GLA_VW), 1) >> 6
    arow = lax.broadcasted_iota(jnp.int32, (CHUNK, GLA_VW), 0)
    acol = lax.broadcasted_iota(jnp.int32, (CHUNK, GLA_VW), 1)
    causal4 = arow >= (acol & (CHUNK - 1))
    n_gh = GLA_KW // GLA_DK
    yc0 = ATT_W + ML_W
    g_a, g_upd, g_qi, g_vvg, g_dec = [], [], [], [], []
    for c in range(NCH):
        r0 = c * CHUNK
        bc = bgl[r0:r0 + CHUNK, :]
        bref = bc[CHUNK // 2 - 1:CHUNK // 2, :]
        btot = bc[CHUNK - 1:CHUNK, :]
        gq = zc[r0:r0 + CHUNK, GQ:GQ + GLA_KW] * (GLA_DK ** -0.5)
        gk = zc[r0:r0 + CHUNK, GK:GK + GLA_KW]
        gv = zc[r0:r0 + CHUNK, GV:GV + GLA_VW]
        qe = (gq * jnp.exp(bc - bref)).astype(BF16)
        ke = gk * jnp.exp(bref - bc)
        kek = jnp.concatenate([jnp.where(khead == h, ke, 0.0) for h in range(n_gh)],
                              axis=0).astype(BF16)
        g_vvg.append(jnp.concatenate([jnp.where(vhead == h, gv, 0.0) for h in range(n_gh)],
                                     axis=0).astype(BF16))
        g_qi.append((gq * jnp.exp(bc)).astype(BF16))
        kd = (gk * jnp.exp(btot - bc)).astype(BF16)
        g_dec.append(jnp.exp(bgl_t[:, r0 + CHUNK - 1:r0 + CHUNK]))
        g_a.append(_dot_nt(qe, kek))
        g_upd.append(_dot_tn(kd, gv.astype(BF16)))
    ip(2)
    s_all = gla_s[...]
    for c in range(NCH):
        r0 = c * CHUNK
        a = jnp.where(causal4, g_a[c], 0.0)
        o = _dot(g_qi[c], s_all.astype(BF16)) + _dot(a.astype(BF16), g_vvg[c])
        y_ref[r0:r0 + CHUNK, yc0:yc0 + GLA_VW] = o
        s_all = jnp.where(gla_diag, g_dec[c] * s_all + g_upd[c], 0.0)
        ip(1)
    gla_s[...] = s_all

    for hp in range(GLA_VW // LANES):
        c0 = hp * LANES
        hcur = y_ref[:, yc0 + c0:yc0 + c0 + LANES]
        y_ref[:, yc0 + c0:yc0 + c0 + LANES] = (
            _pair_rms(hcur, glg_ref[:, c0:c0 + LANES], m0) * _silu(zc[:, GG + c0:GG + c0 + LANES]))
    ip(D_INP // IP_BLK)

    xo = xc_ref[...] + _dot(y_ref[...].astype(BF16), wout_ref[...])
    if final_norm:
        xo = xo * lax.rsqrt(jnp.sum(xo * xo, axis=1, keepdims=True) * (1.0 / D_MODEL) + EPS) * fg_ref[...]
    o_ref[...] = xo


def _project(x, ng_ref, w_rows, z_out):
    hn = x * lax.rsqrt(jnp.sum(x * x, axis=1, keepdims=True) * (1.0 / D_MODEL) + EPS) * ng_ref[...]
    hn_b = hn.astype(BF16)
    for n0 in range(0, D_INP, IP_BLK):
        z_out[:, n0:n0 + IP_BLK] = _dot_nt(hn_b, w_rows(n0))


def _layer_kernel(xa_ref, xn_ref, ng_ref, win_ref, wtl_ref, gb_ref, conv_ref, wal_ref, bal_ref, gp_ref,
                  mlg_ref, glg_ref, wout_ref, fg_ref,
                  o_ref,
                  z_a, z_b, kbuf, vbuf, bias_ref, cbuf, ml_c, ml_n, ml_m, gla_s, y_ref, sc_ref,
                  *, final_norm, tiles_per_seq):
    g = pl.program_id(0)
    i = lax.rem(2 * g, tiles_per_seq)

    def w_rows(n0):
        if n0 < W_MAIN:
            return win_ref[n0:n0 + IP_BLK, :]
        return wtl_ref[n0 - W_MAIN:n0 - W_MAIN + IP_BLK, :]

    @pl.when(g == 0)
    def _first_step():
        r = lax.broadcasted_iota(jnp.int32, (TILE, KEYS), 0)
        m = lax.broadcasted_iota(jnp.int32, (TILE, KEYS), 1)
        jj = m - ((r >> 6) << 6)
        in_band = (jj >= 0) & (jj < BAND)
        for h in range(ML_HEADS):
            row = jnp.broadcast_to(gp_ref[h:h + 1, :], (TILE, ROLL_W))
            rolled = pltpu.roll(row, KEYS, 1, stride=1, stride_axis=0)
            bias_ref[h] = jnp.where(in_band, rolled[:, :KEYS] * LOG2E, NEG)
        _project(xa_ref[0:TILE, :], ng_ref, w_rows, z_a)

    @pl.when(i == 0)
    def _reset():
        kc = lax.broadcasted_iota(jnp.int32, (1, 2 * ATT_W), 1)
        pen_lane = (kc & (LANES - 1)) == jnp.where(kc < ATT_W, PEN_LANE_A, PEN_LANE_B)
        kbuf[...] = jnp.broadcast_to(jnp.where(pen_lane, NEG, 0.0), kbuf.shape).astype(kbuf.dtype)
        vbuf[...] = jnp.zeros(vbuf.shape, vbuf.dtype)
        cbuf[0:8, :] = jnp.zeros((8, cbuf.shape[1]), F32)
        ml_c[...] = jnp.zeros(ml_c.shape, F32)
        ml_n[...] = jnp.zeros(ml_n.shape, F32)
        ml_m[...] = jnp.zeros(ml_m.shape, F32)
        gla_s[...] = jnp.zeros(gla_s.shape, F32)

    step = functools.partial(
        _step, ng_ref=ng_ref, w_rows=w_rows, gb_ref=gb_ref,
        conv_ref=conv_ref, wal_ref=wal_ref, bal_ref=bal_ref, mlg_ref=mlg_ref, glg_ref=glg_ref,
        wout_ref=wout_ref, fg_ref=fg_ref, kbuf=kbuf, vbuf=vbuf, bias_ref=bias_ref,
        cbuf=cbuf, ml_c=ml_c, ml_n=ml_n, ml_m=ml_m, gla_s=gla_s, y_ref=y_ref, sc_ref=sc_ref,
        final_norm=final_norm)

    lo = pl.ds(0, TILE)
    hi = pl.ds(TILE, TILE)
    step(i, z_a, z_b, xn_ref=xa_ref.at[hi], xc_ref=xa_ref.at[lo], o_ref=o_ref.at[lo])
    step(i + 1, z_b, z_a, xn_ref=xn_ref, xc_ref=xa_ref.at[hi], o_ref=o_ref.at[hi])


def _const_spec(shape):
    nd = len(shape)
    return pl.BlockSpec(shape, lambda g, _nd=nd: (0,) * _nd, pipeline_mode=pl.Buffered(1))


def _layer_call(x2d, prm, final_g, final_norm, tiles_per_seq):
    n_tiles = x2d.shape[0] // TILE
    n_steps = n_tiles // 2
    args = (x2d, x2d, prm["ng"], prm["win"], prm["wtl"], prm["gb"], prm["conv"], prm["wal"], prm["bal"], prm["gp"],
            prm["mlg"], prm["glg"], prm["wout"], final_g)
    cur_spec = pl.BlockSpec((2 * TILE, D_MODEL), lambda g: (g, 0))
    nxt_spec = pl.BlockSpec((TILE, D_MODEL), lambda g: (jnp.minimum(2 * g + 2, n_tiles - 1), 0))
    in_specs = [cur_spec, nxt_spec] + [_const_spec(a.shape) for a in args[2:]]
    scratch = [
        pltpu.VMEM((TILE, D_INP), F32),
        pltpu.VMEM((TILE, D_INP), F32),
        pltpu.VMEM((KEYS, 2 * ATT_W), BF16),
        pltpu.VMEM((KEYS, 2 * ATT_W), BF16),
        pltpu.VMEM((ML_HEADS, TILE, KEYS), F32),
        pltpu.VMEM((TILE + 16, 2 * ML_W), F32),
        pltpu.VMEM((N_PAIRS, LANES, LANES), F32),
        pltpu.VMEM((N_PAIRS, 8, LANES), F32),
        pltpu.VMEM((ML_HEADS, 8, LANES), F32),
        pltpu.VMEM((LANES, GLA_VW), F32),
        pltpu.VMEM((TILE, D_MODEL), F32),
        pltpu.VMEM((N_PAIRS, TILE, 2 * TILE), BF16),
    ]
    return pl.pallas_call(
        functools.partial(_layer_kernel, final_norm=final_norm, tiles_per_seq=tiles_per_seq),
        out_shape=jax.ShapeDtypeStruct(x2d.shape, x2d.dtype),
        grid=(n_steps,),
        in_specs=in_specs,
        out_specs=cur_spec,
        scratch_shapes=scratch,
        compiler_params=pltpu.CompilerParams(
            dimension_semantics=("arbitrary",),
            vmem_limit_bytes=VMEM_LIMIT),
        name="hybrid_layer_final" if final_norm else "hybrid_layer",
    )(*args)


def _prep_layer(norm_g, w_in_t, b_gates, conv_w, w_alpha, b_alpha, rel_bias, ml_norm_g, gla_norm_g, w_out):
    pad = jnp.zeros((D_INP - SM - GLA_RANK - 2 * ML_HEADS, D_MODEL), w_in_t.dtype)
    win = w_in_t[:W_MAIN].astype(BF16)
    wtl = jnp.concatenate([w_in_t[3084:3980], w_in_t[3996:4252], w_in_t[3980:3996],
                           w_in_t[3072:3084], pad], axis=0).astype(BF16)
    gb = jnp.zeros((1, LANES), F32).at[0, SM_I:SM_I + 2 * ML_HEADS].set(b_gates)
    wal = jnp.zeros((LANES, GLA_KW), F32).at[:GLA_RANK, :].set(w_alpha).astype(BF16)
    nh = rel_bias.shape[0]
    gp = jnp.concatenate([
        jnp.broadcast_to(rel_bias[:, 2 * REL_CLIP:], (nh, KEYS - REL_CLIP + 1)),
        rel_bias[:, 2 * REL_CLIP - 1:0:-1],
        jnp.broadcast_to(rel_bias[:, :1], (nh, ROLL_W - KEYS - REL_CLIP)),
    ], axis=1)
    return dict(ng=norm_g.reshape(1, D_MODEL), win=win, wtl=wtl, gb=gb, conv=conv_w, wal=wal,
                bal=b_alpha.reshape(1, GLA_KW), gp=gp, mlg=ml_norm_g.reshape(1, ML_W),
                glg=gla_norm_g.reshape(1, GLA_VW), wout=w_out.astype(BF16))


def kernel(x, norm_g, w_in, b_gates, conv_w, w_alpha, b_alpha, rel_bias, ml_norm_g, gla_norm_g, w_out, final_g):
    depth = norm_g.shape[0]
    bsz, seq, _ = x.shape
    fg = final_g.reshape(1, D_MODEL)
    w_in_t = jnp.transpose(w_in, (2, 0, 1))
    x2d = x.reshape(bsz * seq, D_MODEL)
    for l in range(depth):
        prm = _prep_layer(norm_g[l], w_in_t[:, l, :], b_gates[l], conv_w[l], w_alpha[l], b_alpha[l], rel_bias[l],
                          ml_norm_g[l], gla_norm_g[l], w_out[l])
        x2d = _layer_call(x2d, prm, fg, final_norm=(l == depth - 1), tiles_per_seq=seq // TILE)
    return x2d.reshape(bsz, seq, D_MODEL)
```

```python
import functools

import jax
import jax.numpy as jnp
from jax import lax
from jax.experimental import pallas as pl
from jax.experimental.pallas import tpu as pltpu

F32 = jnp.float32
BF16 = jnp.bfloat16

D_MODEL = 1024
CHUNK = 64
HEAD_DIM = 64
ATT_W = 384
ML_W = 384
GLA_KW = 128
GLA_VW = 256
GLA_DK = 32
GLA_RANK = 16
GLA_TAU = 16.0
ML_HEADS = 6
PAST_CHUNKS = 8
REL_CLIP = 128
CONV_W = 4
EPS = 1e-6
NEG = -1e30
LOG2E = 1.4426950408889634

LANES = 128
TILE = 256
NCH = TILE // CHUNK
HIST = PAST_CHUNKS * CHUNK
KEYS = HIST + TILE
HALF_ROWS = TILE // 2
HALF_KEYS = HIST + HALF_ROWS
PEN_LANE_A = HEAD_DIM
PEN_LANE_B = 0
BAND = (PAST_CHUNKS + 1) * CHUNK
ROLL_W = 1024
N_PAIRS = ML_W // LANES

AQ, AK, AV, AG = 0, 384, 768, 1152
MQ, MK, MV, MO, MG = 1536, 1920, 2304, 2688, 3072
GQ, GK, GV, GG = 3456, 3584, 3712, 3968
SM = 4224
D_INP = 4352
IP_BLK = 256
W_MAIN = 3072
IP_PER_HEAD, IP_AFTER_ATT, IP_PER_ML_PAIR = 1, 1, 1
SM_I = GLA_RANK
SM_F = GLA_RANK + ML_HEADS

VMEM_LIMIT = 56 * 1024 * 1024


def _log_sigmoid(x):
    return jnp.minimum(x, 0.0) - jnp.log1p(jnp.exp(-jnp.abs(x)))


def _silu(x):
    return x * jax.nn.sigmoid(x)


def _dot_nt(a, b):
    return lax.dot_general(a, b, (((1,), (1,)), ((), ())), preferred_element_type=F32)


def _dot_tn(a, b):
    return lax.dot_general(a, b, (((0,), (0,)), ((), ())), preferred_element_type=F32)


def _dot(a, b):
    return jnp.dot(a, b, preferred_element_type=F32)


def _pair_rms(hcur, gain, m0):
    sq = hcur * hcur
    ms0 = jnp.sum(jnp.where(m0, sq, 0.0), axis=1, keepdims=True)
    ms1 = jnp.sum(jnp.where(m0, 0.0, sq), axis=1, keepdims=True)
    ms = jnp.where(m0, ms0, ms1) * (1.0 / HEAD_DIM)
    return hcur * lax.rsqrt(ms + EPS) * gain


def _step(i, zc, zn, xn_ref, xc_ref, ng_ref, w_rows, gb_ref, conv_ref, wal_ref, bal_ref,
          mlg_ref, glg_ref, wout_ref, fg_ref, o_ref,
          kbuf, vbuf, kold, vold, bias_ref, cbuf, ml_c, ml_n, ml_m, gla_s, y_ref, sc_ref, *, final_norm):
    lane = lax.broadcasted_iota(jnp.int32, (1, LANES), 1)
    m0 = lane < HEAD_DIM
    m1 = jnp.logical_not(m0)

    xn = xn_ref[...]
    hn = xn * lax.rsqrt(jnp.sum(xn * xn, axis=1, keepdims=True) * (1.0 / D_MODEL) + EPS) * ng_ref[...]
    hn_b = hn.astype(BF16)
    ip_next = [0]

    def ip(count):
        for _ in range(count):
            n0 = ip_next[0] * IP_BLK
            if n0 < D_INP:
                zn[:, n0:n0 + IP_BLK] = _dot_nt(hn_b, w_rows(n0))
                ip_next[0] += 1

    kbuf[:, 0:HIST] = kold[:, TILE:KEYS]
    vbuf[0:HIST, :] = vold[TILE:KEYS, :]
    frow = lax.broadcasted_iota(jnp.int32, (LANES, 1), 0) < HEAD_DIM
    for p in range(N_PAIRS):
        c0 = p * LANES
        kn_t = zc[:, AK + c0:AK + c0 + LANES].T
        vn = zc[:, AV + c0:AV + c0 + LANES]
        kbuf[c0:c0 + LANES, HIST:KEYS] = jnp.where(frow, kn_t, 0.0).astype(BF16)
        kbuf[ATT_W + c0:ATT_W + c0 + LANES, HIST:KEYS] = jnp.where(frow, 0.0, kn_t).astype(BF16)
        vbuf[HIST:KEYS, c0:c0 + LANES] = jnp.where(m0, vn, 1.0).astype(BF16)
        vbuf[HIST:KEYS, ATT_W + c0:ATT_W + c0 + LANES] = jnp.where(m0, 1.0, vn).astype(BF16)

    def att_scores(u):
        h, half = u // 2, u % 2
        c0 = (h // 2) * LANES
        cb = (h % 2) * ATT_W + c0
        r0, k0 = half * HALF_ROWS, half * HALF_ROWS
        q2 = zc[r0:r0 + HALF_ROWS, AQ + c0:AQ + c0 + LANES] * (HEAD_DIM ** -0.5 * LOG2E)
        if h % 2 == 0:
            qm = jnp.where(m0, q2, jnp.where(lane == PEN_LANE_A, 1.0, 0.0))
        else:
            qm = jnp.where(m1, q2, jnp.where(lane == PEN_LANE_B, 1.0, 0.0))
        return (_dot(qm.astype(BF16), kbuf[cb:cb + LANES, k0:k0 + HALF_KEYS])
                + bias_ref[h, r0:r0 + HALF_ROWS, k0:k0 + HALF_KEYS])

    def att_out(u, s):
        h, half = u // 2, u % 2
        cb = (h % 2) * ATT_W + (h // 2) * LANES
        k0 = half * HALF_ROWS
        pe = jnp.exp2(s - jnp.max(s, axis=1, keepdims=True))
        return _dot(pe.astype(BF16), vbuf[k0:k0 + HALF_KEYS, cb:cb + LANES])

    n_units = 2 * ML_HEADS
    s_cur = att_scores(0)
    outs = []
    for u in range(n_units):
        s_nxt = att_scores(u + 1) if u + 1 < n_units else None
        if u % 2 == 1:
            ip(IP_PER_HEAD)
        outs.append(att_out(u, s_cur))
        s_cur = s_nxt
        if u % 4 == 3:
            c0 = (u // 4) * LANES
            r_a = jnp.concatenate(outs[u - 3:u - 1], axis=0)
            r_b = jnp.concatenate(outs[u - 1:u + 1], axis=0)
            den = pltpu.roll(jnp.where(m0, r_b, r_a), HEAD_DIM, 1)
            att = jnp.where(m0, r_a, r_b) / den
            y_ref[:, c0:c0 + LANES] = att * _silu(zc[:, AG + c0:AG + c0 + LANES])
    ip(IP_AFTER_ATT)

    cbuf[8:8 + TILE, :] = zc[:, MQ:MQ + 2 * ML_W]
    for c0 in range(0, 2 * ML_W, LANES):
        acc = cbuf[8:8 + TILE, c0:c0 + LANES] * conv_ref[CONV_W - 1:CONV_W, c0:c0 + LANES]
        for sft in range(1, CONV_W):
            acc = acc + cbuf[8 - sft:8 - sft + TILE, c0:c0 + LANES] * \
                conv_ref[CONV_W - 1 - sft:CONV_W - sft, c0:c0 + LANES]
        act = _silu(acc)
        if c0 >= ML_W:
            act = act * (HEAD_DIM ** -0.5)
        zc[:, MQ + c0:MQ + c0 + LANES] = act
    cbuf[0:8, :] = cbuf[TILE:TILE + 8, :]

    small = zc[:, SM:SM + LANES]
    pre = small + gb_ref[...]
    lf = _log_sigmoid(pre)
    la = _log_sigmoid(_dot(small.astype(BF16), wal_ref[...]) + bal_ref[...]) * (1.0 / GLA_TAU)
    tr = lax.broadcasted_iota(jnp.int32, (TILE, TILE), 0)
    tc = lax.broadcasted_iota(jnp.int32, (TILE, TILE), 1)
    tri = jnp.where(tr >= tc, 1.0, 0.0).astype(BF16)
    terms = []
    for v in (lf, la):
        hi = v.astype(BF16)
        r1 = v - hi.astype(F32)
        mid = r1.astype(BF16)
        terms += [hi, mid, (r1 - mid.astype(F32)).astype(BF16)]
    cs = _dot(tri, jnp.concatenate(terms, axis=1))
    bml = (cs[:, 2 * LANES:3 * LANES] + cs[:, LANES:2 * LANES]) + cs[:, 0:LANES]
    bla = (cs[:, 5 * LANES:6 * LANES] + cs[:, 4 * LANES:5 * LANES]) + cs[:, 3 * LANES:4 * LANES]
    bgl = jnp.concatenate(
        [bla[0:CHUNK]] + [bla[c * CHUNK:(c + 1) * CHUNK] - bla[c * CHUNK - 1:c * CHUNK]
                          for c in range(1, NCH)], axis=0)
    pre_t = pre.T
    bml_t = bml.T
    bgl_t = bgl.T
    ip(1)

    prow = lax.broadcasted_iota(jnp.int32, (LANES, LANES), 0)
    pcol = lax.broadcasted_iota(jnp.int32, (LANES, LANES), 1)
    pair_diag = (prow < HEAD_DIM) == (pcol < HEAD_DIM)
    prow_first = prow[:, 0:1] < HEAD_DIM

    ml_q, ml_k, ml_vb, ml_vv, ml_s, ml_ni, ml_ct = [], [], [], [], [], [], []
    for p in range(N_PAIRS):
        c0 = p * LANES
        q2 = zc[:, MQ + c0:MQ + c0 + LANES]
        k2 = zc[:, MK + c0:MK + c0 + LANES]
        v2 = zc[:, MV + c0:MV + c0 + LANES]
        q2b = q2.astype(BF16)
        kk = jnp.concatenate([jnp.where(m0, k2, 0.0), jnp.where(m0, 0.0, k2)], axis=0).astype(BF16)
        ct = ml_c[p]
        ml_q.append(q2)
        ml_k.append(k2)
        ml_vb.append(v2.astype(BF16))
        ml_vv.append(jnp.concatenate([jnp.where(m0, v2, 0.0), jnp.where(m0, 0.0, v2)],
                                     axis=0).astype(BF16))
        ml_ct.append(ct)
        ml_s.append(_dot_nt(q2b, kk))
        ml_ni.append(_dot(q2b, ct.astype(BF16)))
    ip(2)

    for p in range(N_PAIRS):
        c0 = p * LANES
        q2, k2, s_pair = ml_q[p], ml_k[p], ml_s[p]
        nvec = ml_n[p][0:1, :]
        qn = q2 * nvec
        iws, dens, wcols, solds = [], [], [], []
        for e in range(2):
            h = 2 * p + e
            mprev = ml_m[h][0:1, :]
            bt_all = jnp.broadcast_to(bml[:, SM_F + h:SM_F + h + 1], (TILE, LANES))
            it_all = jnp.broadcast_to(pre[:, SM_I + h:SM_I + h + 1], (TILE, LANES))
            bj = bml_t[SM_F + h:SM_F + h + 1, :]
            ij = pre_t[SM_I + h:SM_I + h + 1, :]
            den_inter = jnp.sum(jnp.where(m0 if e == 0 else m1, qn, 0.0), axis=1, keepdims=True)
            iw_blocks, den_blocks = [], []
            for rb in range(NCH):
                r0 = rb * CHUNK
                ncol = LANES * ((r0 + CHUNK + LANES - 1) // LANES)
                rr = lax.broadcasted_iota(jnp.int32, (CHUNK, ncol), 0) + r0
                kc = lax.broadcasted_iota(jnp.int32, (CHUNK, ncol), 1)
                btc = bt_all[r0:r0 + CHUNK, 0:1]
                log_d = jnp.where(rr >= kc, btc - bj[:, :ncol] + ij[:, :ncol], NEG)
                inter_log = btc + mprev[:, 0:1]
                m_row = jnp.maximum(inter_log, jnp.max(log_d, axis=1, keepdims=True))
                sc = s_pair[r0:r0 + CHUNK, e * TILE:e * TILE + ncol] * jnp.exp(log_d - m_row)
                sc_ref[p, r0:r0 + CHUNK, e * TILE:e * TILE + ncol] = sc.astype(BF16)
                if ncol < TILE:
                    sc_ref[p, r0:r0 + CHUNK, e * TILE + ncol:(e + 1) * TILE] = jnp.zeros(
                        (CHUNK, TILE - ncol), BF16)
                inter_w = jnp.exp(inter_log - m_row)
                den = jnp.sum(sc, axis=1, keepdims=True) + inter_w * den_inter[r0:r0 + CHUNK]
                den_blocks.append(jnp.maximum(jnp.abs(den), jnp.exp(-m_row)))
                iw_blocks.append(inter_w)
            iws.append(jnp.concatenate(iw_blocks, axis=0))
            dens.append(jnp.concatenate(den_blocks, axis=0))
            g_row = bt_all[TILE - 1:TILE, :]
            a_row = g_row[:, 0:1] - bj + ij
            m_new = jnp.maximum(g_row + mprev, jnp.max(a_row, axis=1, keepdims=True))
            wcols.append(jnp.exp(g_row - bt_all + it_all - m_new))
            solds.append(jnp.exp(g_row + mprev - m_new))
            ml_m[h] = jnp.broadcast_to(m_new, (8, LANES))
        num = _dot(sc_ref[p], ml_vv[p]) + jnp.where(m0, iws[0], iws[1]) * ml_ni[p]
        hcur = num / jnp.where(m0, dens[0], dens[1])
        y_ref[:, ATT_W + c0:ATT_W + c0 + LANES] = (
            jax.nn.sigmoid(zc[:, MO + c0:MO + c0 + LANES])
            * _pair_rms(hcur, mlg_ref[:, c0:c0 + LANES], m0)
            * _silu(zc[:, MG + c0:MG + c0 + LANES]))
        kw = k2 * jnp.where(m0, wcols[0], wcols[1])
        upd = _dot_tn(kw.astype(BF16), ml_vb[p])
        sold_col = jnp.where(prow_first, solds[0][:, 0:1], solds[1][:, 0:1])
        ml_c[p] = jnp.where(pair_diag, sold_col * ml_ct[p] + upd, 0.0)
        nvec = jnp.where(m0, solds[0], solds[1]) * nvec + jnp.sum(kw, axis=0, keepdims=True)
        ml_n[p] = jnp.broadcast_to(nvec, (8, LANES))
        ip(IP_PER_ML_PAIR)

    grow = lax.broadcasted_iota(jnp.int32, (LANES, GLA_VW), 0)
    gcol = lax.broadcasted_iota(jnp.int32, (LANES, GLA_VW), 1)
    gla_diag = (grow >> 5) == (gcol >> 6)
    khead = lane >> 5
    vhead = lax.broadcasted_iota(jnp.int32, (1, GLA_VW), 1) >> 6
    arow = lax.broadcasted_iota(jnp.int32, (CHUNK, GLA_VW), 0)
    acol = lax.broadcasted_iota(jnp.int32, (CHUNK, GLA_VW), 1)
    causal4 = arow >= (acol & (CHUNK - 1))
    n_gh = GLA_KW // GLA_DK
    yc0 = ATT_W + ML_W
    g_a, g_upd, g_qi, g_vvg, g_dec = [], [], [], [], []
    for c in range(NCH):
        r0 = c * CHUNK
        bc = bgl[r0:r0 + CHUNK, :]
        bref = bc[CHUNK // 2 - 1:CHUNK // 2, :]
        btot = bc[CHUNK - 1:CHUNK, :]
        gq = zc[r0:r0 + CHUNK, GQ:GQ + GLA_KW] * (GLA_DK ** -0.5)
        gk = zc[r0:r0 + CHUNK, GK:GK + GLA_KW]
        gv = zc[r0:r0 + CHUNK, GV:GV + GLA_VW]
        qe = (gq * jnp.exp(bc - bref)).astype(BF16)
        ke = gk * jnp.exp(bref - bc)
        kek = jnp.concatenate([jnp.where(khead == h, ke, 0.0) for h in range(n_gh)],
                              axis=0).astype(BF16)
        g_vvg.append(jnp.concatenate([jnp.where(vhead == h, gv, 0.0) for h in range(n_gh)],
                                     axis=0).astype(BF16))
        g_qi.append((gq * jnp.exp(bc)).astype(BF16))
        kd = (gk * jnp.exp(btot - bc)).astype(BF16)
        g_dec.append(jnp.exp(bgl_t[:, r0 + CHUNK - 1:r0 + CHUNK]))
        g_a.append(_dot_nt(qe, kek))
        g_upd.append(_dot_tn(kd, gv.astype(BF16)))
    ip(2)
    s_all = gla_s[...]
    for c in range(NCH):
        r0 = c * CHUNK
        a = jnp.where(causal4, g_a[c], 0.0)
        o = _dot(g_qi[c], s_all.astype(BF16)) + _dot(a.astype(BF16), g_vvg[c])
        y_ref[r0:r0 + CHUNK, yc0:yc0 + GLA_VW] = o
        s_all = jnp.where(gla_diag, g_dec[c] * s_all + g_upd[c], 0.0)
        ip(1)
    gla_s[...] = s_all

    for hp in range(GLA_VW // LANES):
        c0 = hp * LANES
        hcur = y_ref[:, yc0 + c0:yc0 + c0 + LANES]
        y_ref[:, yc0 + c0:yc0 + c0 + LANES] = (
            _pair_rms(hcur, glg_ref[:, c0:c0 + LANES], m0) * _silu(zc[:, GG + c0:GG + c0 + LANES]))
    ip(D_INP // IP_BLK)

    xo = xc_ref[...] + _dot(y_ref[...].astype(BF16), wout_ref[...])
    if final_norm:
        xo = xo * lax.rsqrt(jnp.sum(xo * xo, axis=1, keepdims=True) * (1.0 / D_MODEL) + EPS) * fg_ref[...]
    o_ref[...] = xo


def _project(x, ng_ref, w_rows, z_out):
    hn = x * lax.rsqrt(jnp.sum(x * x, axis=1, keepdims=True) * (1.0 / D_MODEL) + EPS) * ng_ref[...]
    hn_b = hn.astype(BF16)
    for n0 in range(0, D_INP, IP_BLK):
        z_out[:, n0:n0 + IP_BLK] = _dot_nt(hn_b, w_rows(n0))


def _layer_kernel(xa_ref, xn_ref, ng_ref, win_ref, wtl_ref, gb_ref, conv_ref, wal_ref, bal_ref, gp_ref,
                  mlg_ref, glg_ref, wout_ref, fg_ref,
                  o_ref,
                  z_a, z_b, kb, vb, bias_ref, cbuf, ml_c, ml_n, ml_m, gla_s, y_ref, sc_ref,
                  *, final_norm, tiles_per_seq):
    g = pl.program_id(0)
    i = lax.rem(2 * g, tiles_per_seq)

    def w_rows(n0):
        if n0 < W_MAIN:
            return win_ref[n0:n0 + IP_BLK, :]
        return wtl_ref[n0 - W_MAIN:n0 - W_MAIN + IP_BLK, :]

    @pl.when(g == 0)
    def _first_step():
        r = lax.broadcasted_iota(jnp.int32, (TILE, KEYS), 0)
        m = lax.broadcasted_iota(jnp.int32, (TILE, KEYS), 1)
        jj = m - ((r >> 6) << 6)
        in_band = (jj >= 0) & (jj < BAND)
        for h in range(ML_HEADS):
            row = jnp.broadcast_to(gp_ref[h:h + 1, :], (TILE, ROLL_W))
            rolled = pltpu.roll(row, KEYS, 1, stride=1, stride_axis=0)
            bias_ref[h] = jnp.where(in_band, rolled[:, :KEYS] * LOG2E, NEG)
        _project(xa_ref[0:TILE, :], ng_ref, w_rows, z_a)

    @pl.when(i == 0)
    def _reset():
        kr = lax.broadcasted_iota(jnp.int32, (2 * ATT_W, 1), 0)
        pen_row = (kr & (LANES - 1)) == jnp.where(kr < ATT_W, PEN_LANE_A, PEN_LANE_B)
        pen_init = jnp.broadcast_to(jnp.where(pen_row, NEG, 0.0), kb.shape[1:]).astype(kb.dtype)
        for slot in range(2):
            kb[slot] = pen_init
            vb[slot] = jnp.zeros(vb.shape[1:], vb.dtype)
        cbuf[0:8, :] = jnp.zeros((8, cbuf.shape[1]), F32)
        ml_c[...] = jnp.zeros(ml_c.shape, F32)
        ml_n[...] = jnp.zeros(ml_n.shape, F32)
        ml_m[...] = jnp.zeros(ml_m.shape, F32)
        gla_s[...] = jnp.zeros(gla_s.shape, F32)

    step = functools.partial(
        _step, ng_ref=ng_ref, w_rows=w_rows, gb_ref=gb_ref,
        conv_ref=conv_ref, wal_ref=wal_ref, bal_ref=bal_ref, mlg_ref=mlg_ref, glg_ref=glg_ref,
        wout_ref=wout_ref, fg_ref=fg_ref, bias_ref=bias_ref,
        cbuf=cbuf, ml_c=ml_c, ml_n=ml_n, ml_m=ml_m, gla_s=gla_s, y_ref=y_ref, sc_ref=sc_ref,
        final_norm=final_norm)

    lo = pl.ds(0, TILE)
    hi = pl.ds(TILE, TILE)
    step(i, z_a, z_b, xn_ref=xa_ref.at[hi], xc_ref=xa_ref.at[lo], o_ref=o_ref.at[lo],
         kbuf=kb.at[0], vbuf=vb.at[0], kold=kb.at[1], vold=vb.at[1])
    step(i + 1, z_b, z_a, xn_ref=xn_ref, xc_ref=xa_ref.at[hi], o_ref=o_ref.at[hi],
         kbuf=kb.at[1], vbuf=vb.at[1], kold=kb.at[0], vold=vb.at[0])


def _const_spec(shape):
    nd = len(shape)
    return pl.BlockSpec(shape, lambda g, _nd=nd: (0,) * _nd, pipeline_mode=pl.Buffered(1))


def _layer_call(x2d, prm, final_g, final_norm, tiles_per_seq):
    n_tiles = x2d.shape[0] // TILE
    n_steps = n_tiles // 2
    args = (x2d, x2d, prm["ng"], prm["win"], prm["wtl"], prm["gb"], prm["conv"], prm["wal"], prm["bal"], prm["gp"],
            prm["mlg"], prm["glg"], prm["wout"], final_g)
    cur_spec = pl.BlockSpec((2 * TILE, D_MODEL), lambda g: (g, 0))
    nxt_spec = pl.BlockSpec((TILE, D_MODEL), lambda g: (jnp.minimum(2 * g + 2, n_tiles - 1), 0))
    in_specs = [cur_spec, nxt_spec] + [_const_spec(a.shape) for a in args[2:]]
    scratch = [
        pltpu.VMEM((TILE, D_INP), F32),
        pltpu.VMEM((TILE, D_INP), F32),
        pltpu.VMEM((2, 2 * ATT_W, KEYS), BF16),
        pltpu.VMEM((2, KEYS, 2 * ATT_W), BF16),
        pltpu.VMEM((ML_HEADS, TILE, KEYS), F32),
        pltpu.VMEM((TILE + 16, 2 * ML_W), F32),
        pltpu.VMEM((N_PAIRS, LANES, LANES), F32),
        pltpu.VMEM((N_PAIRS, 8, LANES), F32),
        pltpu.VMEM((ML_HEADS, 8, LANES), F32),
        pltpu.VMEM((LANES, GLA_VW), F32),
        pltpu.VMEM((TILE, D_MODEL), F32),
        pltpu.VMEM((N_PAIRS, TILE, 2 * TILE), BF16),
    ]
    return pl.pallas_call(
        functools.partial(_layer_kernel, final_norm=final_norm, tiles_per_seq=tiles_per_seq),
        out_shape=jax.ShapeDtypeStruct(x2d.shape, x2d.dtype),
        grid=(n_steps,),
        in_specs=in_specs,
        out_specs=cur_spec,
        scratch_shapes=scratch,
        compiler_params=pltpu.CompilerParams(
            dimension_semantics=("arbitrary",),
            vmem_limit_bytes=VMEM_LIMIT),
        name="hybrid_layer_final" if final_norm else "hybrid_layer",
    )(*args)


def _prep_layer(norm_g, w_in_t, b_gates, conv_w, w_alpha, b_alpha, rel_bias, ml_norm_g, gla_norm_g, w_out):
    pad = jnp.zeros((D_INP - SM - GLA_RANK - 2 * ML_HEADS, D_MODEL), w_in_t.dtype)
    win = w_in_t[:W_MAIN].astype(BF16)
    wtl = jnp.concatenate([w_in_t[3084:3980], w_in_t[3996:4252], w_in_t[3980:3996],
                           w_in_t[3072:3084], pad], axis=0).astype(BF16)
    gb = jnp.zeros((1, LANES), F32).at[0, SM_I:SM_I + 2 * ML_HEADS].set(b_gates)
    wal = jnp.zeros((LANES, GLA_KW), F32).at[:GLA_RANK, :].set(w_alpha).astype(BF16)
    nh = rel_bias.shape[0]
    gp = jnp.concatenate([
        jnp.broadcast_to(rel_bias[:, 2 * REL_CLIP:], (nh, KEYS - REL_CLIP + 1)),
        rel_bias[:, 2 * REL_CLIP - 1:0:-1],
        jnp.broadcast_to(rel_bias[:, :1], (nh, ROLL_W - KEYS - REL_CLIP)),
    ], axis=1)
    return dict(ng=norm_g.reshape(1, D_MODEL), win=win, wtl=wtl, gb=gb, conv=conv_w, wal=wal,
                bal=b_alpha.reshape(1, GLA_KW), gp=gp, mlg=ml_norm_g.reshape(1, ML_W),
                glg=gla_norm_g.reshape(1, GLA_VW), wout=w_out.astype(BF16))


def kernel(x, norm_g, w_in, b_gates, conv_w, w_alpha, b_alpha, rel_bias, ml_norm_g, gla_norm_g, w_out, final_g):
    depth = norm_g.shape[0]
    bsz, seq, _ = x.shape
    fg = final_g.reshape(1, D_MODEL)
    w_in_t = jnp.transpose(w_in, (2, 0, 1))
    x2d = x.reshape(bsz * seq, D_MODEL)
    for l in range(depth):
        prm = _prep_layer(norm_g[l], w_in_t[:, l, :], b_gates[l], conv_w[l], w_alpha[l], b_alpha[l], rel_bias[l],
                          ml_norm_g[l], gla_norm_g[l], w_out[l])
        x2d = _layer_call(x2d, prm, fg, final_norm=(l == depth - 1), tiles_per_seq=seq // TILE)
    return x2d.reshape(bsz, seq, D_MODEL)
```

```python
import functools

import jax
import jax.numpy as jnp
from jax import lax
from jax.experimental import pallas as pl
from jax.experimental.pallas import tpu as pltpu

F32 = jnp.float32
BF16 = jnp.bfloat16

D_MODEL = 1024
CHUNK = 64
HEAD_DIM = 64
ATT_W = 384
ML_W = 384
GLA_KW = 128
GLA_VW = 256
GLA_DK = 32
GLA_RANK = 16
GLA_TAU = 16.0
ML_HEADS = 6
PAST_CHUNKS = 8
REL_CLIP = 128
CONV_W = 4
EPS = 1e-6
NEG = -1e30
LOG2E = 1.4426950408889634

LANES = 128
TILE = 256
NCH = TILE // CHUNK
HIST = PAST_CHUNKS * CHUNK
KEYS = HIST + TILE
HALF_ROWS = TILE // 2
HALF_KEYS = HIST + HALF_ROWS
PEN_LANE_A = HEAD_DIM
PEN_LANE_B = 0
BAND = (PAST_CHUNKS + 1) * CHUNK
ROLL_W = 1024
N_PAIRS = ML_W // LANES

AQ, AK, AV, AG = 0, 384, 768, 1152
MQ, MK, MV, MO, MG = 1536, 1920, 2304, 2688, 3072
GQ, GK, GV, GG = 3456, 3584, 3712, 3968
SM = 4224
D_INP = 4352
IP_BLK = 256
W_MAIN = 3072
IP_PER_HEAD, IP_AFTER_ATT, IP_PER_ML_PAIR = 1, 1, 1
SM_I = GLA_RANK
SM_F = GLA_RANK + ML_HEADS

VMEM_LIMIT = 56 * 1024 * 1024


def _log_sigmoid(x):
    return jnp.minimum(x, 0.0) - jnp.log1p(jnp.exp(-jnp.abs(x)))


def _silu(x):
    return x * jax.nn.sigmoid(x)


def _dot_nt(a, b):
    return lax.dot_general(a, b, (((1,), (1,)), ((), ())), preferred_element_type=F32)


def _dot_tn(a, b):
    return lax.dot_general(a, b, (((0,), (0,)), ((), ())), preferred_element_type=F32)


def _dot(a, b):
    return jnp.dot(a, b, preferred_element_type=F32)


def _pair_rms(hcur, gain, m0):
    sq = hcur * hcur
    ms0 = jnp.sum(jnp.where(m0, sq, 0.0), axis=1, keepdims=True)
    ms1 = jnp.sum(jnp.where(m0, 0.0, sq), axis=1, keepdims=True)
    ms = jnp.where(m0, ms0, ms1) * (1.0 / HEAD_DIM)
    return hcur * lax.rsqrt(ms + EPS) * gain


def _step(i, zc, zn, xn_ref, xc_ref, ng_ref, w_cols, gb_ref, conv_ref, wal_ref, bal_ref,
          mlg_ref, glg_ref, wout_ref, fg_ref, o_ref,
          kbuf, vbuf, kold, vold, bias_ref, cbuf, ml_c, ml_n, ml_m, gla_s, y_ref, sc_ref, *, final_norm):
    lane = lax.broadcasted_iota(jnp.int32, (1, LANES), 1)
    m0 = lane < HEAD_DIM
    m1 = jnp.logical_not(m0)

    xn = xn_ref[...]
    hn = xn * lax.rsqrt(jnp.sum(xn * xn, axis=1, keepdims=True) * (1.0 / D_MODEL) + EPS) * ng_ref[...]
    hn_b = hn.astype(BF16)
    ip_next = [0]

    def ip(count):
        for _ in range(count):
            n0 = ip_next[0] * IP_BLK
            if n0 < D_INP:
                zn[:, n0:n0 + IP_BLK] = _dot(hn_b, w_cols(n0))
                ip_next[0] += 1

    kbuf[:, 0:HIST] = kold[:, TILE:KEYS]
    vbuf[0:HIST, :] = vold[TILE:KEYS, :]
    frow = lax.broadcasted_iota(jnp.int32, (LANES, 1), 0) < HEAD_DIM
    for p in range(N_PAIRS):
        c0 = p * LANES
        kn_t = zc[:, AK + c0:AK + c0 + LANES].T
        vn = zc[:, AV + c0:AV + c0 + LANES]
        kbuf[c0:c0 + LANES, HIST:KEYS] = jnp.where(frow, kn_t, 0.0).astype(BF16)
        kbuf[ATT_W + c0:ATT_W + c0 + LANES, HIST:KEYS] = jnp.where(frow, 0.0, kn_t).astype(BF16)
        vbuf[HIST:KEYS, c0:c0 + LANES] = jnp.where(m0, vn, 1.0).astype(BF16)
        vbuf[HIST:KEYS, ATT_W + c0:ATT_W + c0 + LANES] = jnp.where(m0, 1.0, vn).astype(BF16)

    def att_scores(u):
        h, half = u // 2, u % 2
        c0 = (h // 2) * LANES
        cb = (h % 2) * ATT_W + c0
        r0, k0 = half * HALF_ROWS, half * HALF_ROWS
        q2 = zc[r0:r0 + HALF_ROWS, AQ + c0:AQ + c0 + LANES] * (HEAD_DIM ** -0.5 * LOG2E)
        if h % 2 == 0:
            qm = jnp.where(m0, q2, jnp.where(lane == PEN_LANE_A, 1.0, 0.0))
        else:
            qm = jnp.where(m1, q2, jnp.where(lane == PEN_LANE_B, 1.0, 0.0))
        return (_dot(qm.astype(BF16), kbuf[cb:cb + LANES, k0:k0 + HALF_KEYS])
                + bias_ref[h, r0:r0 + HALF_ROWS, k0:k0 + HALF_KEYS])

    def att_out(u, s):
        h, half = u // 2, u % 2
        cb = (h % 2) * ATT_W + (h // 2) * LANES
        k0 = half * HALF_ROWS
        pe = jnp.exp2(s - jnp.max(s, axis=1, keepdims=True))
        return _dot(pe.astype(BF16), vbuf[k0:k0 + HALF_KEYS, cb:cb + LANES])

    n_units = 2 * ML_HEADS
    s_cur = att_scores(0)
    outs = []
    for u in range(n_units):
        s_nxt = att_scores(u + 1) if u + 1 < n_units else None
        if u % 2 == 1:
            ip(IP_PER_HEAD)
        outs.append(att_out(u, s_cur))
        s_cur = s_nxt
        if u % 4 == 3:
            c0 = (u // 4) * LANES
            r_a = jnp.concatenate(outs[u - 3:u - 1], axis=0)
            r_b = jnp.concatenate(outs[u - 1:u + 1], axis=0)
            den = pltpu.roll(jnp.where(m0, r_b, r_a), HEAD_DIM, 1)
            att = jnp.where(m0, r_a, r_b) / den
            y_ref[:, c0:c0 + LANES] = att * _silu(zc[:, AG + c0:AG + c0 + LANES])
    ip(IP_AFTER_ATT)

    cbuf[8:8 + TILE, :] = zc[:, MQ:MQ + 2 * ML_W]
    for c0 in range(0, 2 * ML_W, LANES):
        acc = cbuf[8:8 + TILE, c0:c0 + LANES] * conv_ref[CONV_W - 1:CONV_W, c0:c0 + LANES]
        for sft in range(1, CONV_W):
            acc = acc + cbuf[8 - sft:8 - sft + TILE, c0:c0 + LANES] * \
                conv_ref[CONV_W - 1 - sft:CONV_W - sft, c0:c0 + LANES]
        act = _silu(acc)
        if c0 >= ML_W:
            act = act * (HEAD_DIM ** -0.5)
        zc[:, MQ + c0:MQ + c0 + LANES] = act
    cbuf[0:8, :] = cbuf[TILE:TILE + 8, :]

    small = zc[:, SM:SM + LANES]
    pre = small + gb_ref[...]
    lf = _log_sigmoid(pre)
    la = _log_sigmoid(_dot(small.astype(BF16), wal_ref[...]) + bal_ref[...]) * (1.0 / GLA_TAU)
    tr = lax.broadcasted_iota(jnp.int32, (TILE, TILE), 0)
    tc = lax.broadcasted_iota(jnp.int32, (TILE, TILE), 1)
    tri = jnp.where(tr >= tc, 1.0, 0.0).astype(BF16)
    terms = []
    for v in (lf, la):
        hi = v.astype(BF16)
        r1 = v - hi.astype(F32)
        mid = r1.astype(BF16)
        terms += [hi, mid, (r1 - mid.astype(F32)).astype(BF16)]
    cs = _dot(tri, jnp.concatenate(terms, axis=1))
    bml = (cs[:, 2 * LANES:3 * LANES] + cs[:, LANES:2 * LANES]) + cs[:, 0:LANES]
    bla = (cs[:, 5 * LANES:6 * LANES] + cs[:, 4 * LANES:5 * LANES]) + cs[:, 3 * LANES:4 * LANES]
    bgl = jnp.concatenate(
        [bla[0:CHUNK]] + [bla[c * CHUNK:(c + 1) * CHUNK] - bla[c * CHUNK - 1:c * CHUNK]
                          for c in range(1, NCH)], axis=0)
    pre_t = pre.T
    bml_t = bml.T
    bgl_t = bgl.T
    ip(1)

    prow = lax.broadcasted_iota(jnp.int32, (LANES, LANES), 0)
    pcol = lax.broadcasted_iota(jnp.int32, (LANES, LANES), 1)
    pair_diag = (prow < HEAD_DIM) == (pcol < HEAD_DIM)
    prow_first = prow[:, 0:1] < HEAD_DIM

    ml_q, ml_k, ml_vb, ml_vv, ml_s, ml_ni, ml_ct = [], [], [], [], [], [], []
    for p in range(N_PAIRS):
        c0 = p * LANES
        q2 = zc[:, MQ + c0:MQ + c0 + LANES]
        k2 = zc[:, MK + c0:MK + c0 + LANES]
        v2 = zc[:, MV + c0:MV + c0 + LANES]
        q2b = q2.astype(BF16)
        kk = jnp.concatenate([jnp.where(m0, k2, 0.0), jnp.where(m0, 0.0, k2)], axis=0).astype(BF16)
        ct = ml_c[p]
        ml_q.append(q2)
        ml_k.append(k2)
        ml_vb.append(v2.astype(BF16))
        ml_vv.append(jnp.concatenate([jnp.where(m0, v2, 0.0), jnp.where(m0, 0.0, v2)],
                                     axis=0).astype(BF16))
        ml_ct.append(ct)
        ml_s.append(_dot_nt(q2b, kk))
        ml_ni.append(_dot(q2b, ct.astype(BF16)))
    ip(2)

    for p in range(N_PAIRS):
        c0 = p * LANES
        q2, k2, s_pair = ml_q[p], ml_k[p], ml_s[p]
        nvec = ml_n[p][0:1, :]
        qn = q2 * nvec
        iws, dens, wcols, solds = [], [], [], []
        for e in range(2):
            h = 2 * p + e
            mprev = ml_m[h][0:1, :]
            bt_all = jnp.broadcast_to(bml[:, SM_F + h:SM_F + h + 1], (TILE, LANES))
            it_all = jnp.broadcast_to(pre[:, SM_I + h:SM_I + h + 1], (TILE, LANES))
            bj = bml_t[SM_F + h:SM_F + h + 1, :]
            ij = pre_t[SM_I + h:SM_I + h + 1, :]
            den_inter = jnp.sum(jnp.where(m0 if e == 0 else m1, qn, 0.0), axis=1, keepdims=True)
            iw_blocks, den_blocks = [], []
            for rb in range(NCH):
                r0 = rb * CHUNK
                ncol = LANES * ((r0 + CHUNK + LANES - 1) // LANES)
                rr = lax.broadcasted_iota(jnp.int32, (CHUNK, ncol), 0) + r0
                kc = lax.broadcasted_iota(jnp.int32, (CHUNK, ncol), 1)
                btc = bt_all[r0:r0 + CHUNK, 0:1]
                log_d = jnp.where(rr >= kc, btc - bj[:, :ncol] + ij[:, :ncol], NEG)
                inter_log = btc + mprev[:, 0:1]
                m_row = jnp.maximum(inter_log, jnp.max(log_d, axis=1, keepdims=True))
                sc = s_pair[r0:r0 + CHUNK, e * TILE:e * TILE + ncol] * jnp.exp(log_d - m_row)
                sc_ref[p, r0:r0 + CHUNK, e * TILE:e * TILE + ncol] = sc.astype(BF16)
                if ncol < TILE:
                    sc_ref[p, r0:r0 + CHUNK, e * TILE + ncol:(e + 1) * TILE] = jnp.zeros(
                        (CHUNK, TILE - ncol), BF16)
                inter_w = jnp.exp(inter_log - m_row)
                den = jnp.sum(sc, axis=1, keepdims=True) + inter_w * den_inter[r0:r0 + CHUNK]
                den_blocks.append(jnp.maximum(jnp.abs(den), jnp.exp(-m_row)))
                iw_blocks.append(inter_w)
            iws.append(jnp.concatenate(iw_blocks, axis=0))
            dens.append(jnp.concatenate(den_blocks, axis=0))
            g_row = bt_all[TILE - 1:TILE, :]
            a_row = g_row[:, 0:1] - bj + ij
            m_new = jnp.maximum(g_row + mprev, jnp.max(a_row, axis=1, keepdims=True))
            wcols.append(jnp.exp(g_row - bt_all + it_all - m_new))
            solds.append(jnp.exp(g_row + mprev - m_new))
            ml_m[h] = jnp.broadcast_to(m_new, (8, LANES))
        num = _dot(sc_ref[p], ml_vv[p]) + jnp.where(m0, iws[0], iws[1]) * ml_ni[p]
        hcur = num / jnp.where(m0, dens[0], dens[1])
        y_ref[:, ATT_W + c0:ATT_W + c0 + LANES] = (
            jax.nn.sigmoid(zc[:, MO + c0:MO + c0 + LANES])
            * _pair_rms(hcur, mlg_ref[:, c0:c0 + LANES], m0)
            * _silu(zc[:, MG + c0:MG + c0 + LANES]))
        kw = k2 * jnp.where(m0, wcols[0], wcols[1])
        upd = _dot_tn(kw.astype(BF16), ml_vb[p])
        sold_col = jnp.where(prow_first, solds[0][:, 0:1], solds[1][:, 0:1])
        ml_c[p] = jnp.where(pair_diag, sold_col * ml_ct[p] + upd, 0.0)
        nvec = jnp.where(m0, solds[0], solds[1]) * nvec + jnp.sum(kw, axis=0, keepdims=True)
        ml_n[p] = jnp.broadcast_to(nvec, (8, LANES))
        ip(IP_PER_ML_PAIR)

    grow = lax.broadcasted_iota(jnp.int32, (LANES, GLA_VW), 0)
    gcol = lax.broadcasted_iota(jnp.int32, (LANES, GLA_VW), 1)
    gla_diag = (grow >> 5) == (gcol >> 6)
    khead = lane >> 5
    vhead = lax.broadcasted_iota(jnp.int32, (1, GLA_VW), 1) >> 6
    arow = lax.broadcasted_iota(jnp.int32, (CHUNK, GLA_VW), 0)
    acol = lax.broadcasted_iota(jnp.int32, (CHUNK, GLA_VW), 1)
    causal4 = arow >= (acol & (CHUNK - 1))
    n_gh = GLA_KW // GLA_DK
    yc0 = ATT_W + ML_W
    g_a, g_upd, g_qi, g_vvg, g_dec = [], [], [], [], []
    for c in range(NCH):
        r0 = c * CHUNK
        bc = bgl[r0:r0 + CHUNK, :]
        bref = bc[CHUNK // 2 - 1:CHUNK // 2, :]
        btot = bc[CHUNK - 1:CHUNK, :]
        gq = zc[r0:r0 + CHUNK, GQ:GQ + GLA_KW] * (GLA_DK ** -0.5)
        gk = zc[r0:r0 + CHUNK, GK:GK + GLA_KW]
        gv = zc[r0:r0 + CHUNK, GV:GV + GLA_VW]
        qe = (gq * jnp.exp(bc - bref)).astype(BF16)
        ke = gk * jnp.exp(bref - bc)
        kek = jnp.concatenate([jnp.where(khead == h, ke, 0.0) for h in range(n_gh)],
                              axis=0).astype(BF16)
        g_vvg.append(jnp.concatenate([jnp.where(vhead == h, gv, 0.0) for h in range(n_gh)],
                                     axis=0).astype(BF16))
        g_qi.append((gq * jnp.exp(bc)).astype(BF16))
        kd = (gk * jnp.exp(btot - bc)).astype(BF16)
        g_dec.append(jnp.exp(bgl_t[:, r0 + CHUNK - 1:r0 + CHUNK]))
        g_a.append(_dot_nt(qe, kek))
        g_upd.append(_dot_tn(kd, gv.astype(BF16)))
    ip(2)
    s_all = gla_s[...]
    for c in range(NCH):
        r0 = c * CHUNK
        a = jnp.where(causal4, g_a[c], 0.0)
        o = _dot(g_qi[c], s_all.astype(BF16)) + _dot(a.astype(BF16), g_vvg[c])
        y_ref[r0:r0 + CHUNK, yc0:yc0 + GLA_VW] = o
        s_all = jnp.where(gla_diag, g_dec[c] * s_all + g_upd[c], 0.0)
        ip(1)
    gla_s[...] = s_all

    for hp in range(GLA_VW // LANES):
        c0 = hp * LANES
        hcur = y_ref[:, yc0 + c0:yc0 + c0 + LANES]
        y_ref[:, yc0 + c0:yc0 + c0 + LANES] = (
            _pair_rms(hcur, glg_ref[:, c0:c0 + LANES], m0) * _silu(zc[:, GG + c0:GG + c0 + LANES]))
    ip(D_INP // IP_BLK)

    xo = xc_ref[...] + _dot(y_ref[...].astype(BF16), wout_ref[...])
    if final_norm:
        xo = xo * lax.rsqrt(jnp.sum(xo * xo, axis=1, keepdims=True) * (1.0 / D_MODEL) + EPS) * fg_ref[...]
    o_ref[...] = xo


def _project(x, ng_ref, w_cols, z_out):
    hn = x * lax.rsqrt(jnp.sum(x * x, axis=1, keepdims=True) * (1.0 / D_MODEL) + EPS) * ng_ref[...]
    hn_b = hn.astype(BF16)
    for n0 in range(0, D_INP, IP_BLK):
        z_out[:, n0:n0 + IP_BLK] = _dot(hn_b, w_cols(n0))


def _layer_kernel(xa_ref, xn_ref, ng_ref, win_ref, wtl_ref, gb_ref, conv_ref, wal_ref, bal_ref, gp_ref,
                  mlg_ref, glg_ref, wout_ref, fg_ref,
                  o_ref,
                  z_a, z_b, kb, vb, bias_ref, cbuf, ml_c, ml_n, ml_m, gla_s, y_ref, sc_ref, wrm,
                  *, final_norm, tiles_per_seq):
    g = pl.program_id(0)
    i = lax.rem(2 * g, tiles_per_seq)

    def w_rows(n0):
        if n0 < W_MAIN:
            return win_ref[n0:n0 + IP_BLK, :]
        return wtl_ref[n0 - W_MAIN:n0 - W_MAIN + IP_BLK, :]

    def w_cols(n0):
        return wrm[:, n0:n0 + IP_BLK]

    @pl.when(g == 0)
    def _first_step():
        r = lax.broadcasted_iota(jnp.int32, (TILE, KEYS), 0)
        m = lax.broadcasted_iota(jnp.int32, (TILE, KEYS), 1)
        jj = m - ((r >> 6) << 6)
        in_band = (jj >= 0) & (jj < BAND)
        for h in range(ML_HEADS):
            row = jnp.broadcast_to(gp_ref[h:h + 1, :], (TILE, ROLL_W))
            rolled = pltpu.roll(row, KEYS, 1, stride=1, stride_axis=0)
            bias_ref[h] = jnp.where(in_band, rolled[:, :KEYS] * LOG2E, NEG)
        for n0 in range(0, D_INP, IP_BLK):
            wrm[:, n0:n0 + IP_BLK] = w_rows(n0).astype(F32).T.astype(BF16)
        _project(xa_ref[0:TILE, :], ng_ref, w_cols, z_a)

    @pl.when(i == 0)
    def _reset():
        kr = lax.broadcasted_iota(jnp.int32, (2 * ATT_W, 1), 0)
        pen_row = (kr & (LANES - 1)) == jnp.where(kr < ATT_W, PEN_LANE_A, PEN_LANE_B)
        pen_init = jnp.broadcast_to(jnp.where(pen_row, NEG, 0.0), kb.shape[1:]).astype(kb.dtype)
        for slot in range(2):
            kb[slot] = pen_init
            vb[slot] = jnp.zeros(vb.shape[1:], vb.dtype)
        cbuf[0:8, :] = jnp.zeros((8, cbuf.shape[1]), F32)
        ml_c[...] = jnp.zeros(ml_c.shape, F32)
        ml_n[...] = jnp.zeros(ml_n.shape, F32)
        ml_m[...] = jnp.zeros(ml_m.shape, F32)
        gla_s[...] = jnp.zeros(gla_s.shape, F32)

    step = functools.partial(
        _step, ng_ref=ng_ref, w_cols=w_cols, gb_ref=gb_ref,
        conv_ref=conv_ref, wal_ref=wal_ref, bal_ref=bal_ref, mlg_ref=mlg_ref, glg_ref=glg_ref,
        wout_ref=wout_ref, fg_ref=fg_ref, bias_ref=bias_ref,
        cbuf=cbuf, ml_c=ml_c, ml_n=ml_n, ml_m=ml_m, gla_s=gla_s, y_ref=y_ref, sc_ref=sc_ref,
        final_norm=final_norm)

    lo = pl.ds(0, TILE)
    hi = pl.ds(TILE, TILE)
    step(i, z_a, z_b, xn_ref=xa_ref.at[hi], xc_ref=xa_ref.at[lo], o_ref=o_ref.at[lo],
         kbuf=kb.at[0], vbuf=vb.at[0], kold=kb.at[1], vold=vb.at[1])
    step(i + 1, z_b, z_a, xn_ref=xn_ref, xc_ref=xa_ref.at[hi], o_ref=o_ref.at[hi],
         kbuf=kb.at[1], vbuf=vb.at[1], kold=kb.at[0], vold=vb.at[0])


def _const_spec(shape):
    nd = len(shape)
    return pl.BlockSpec(shape, lambda g, _nd=nd: (0,) * _nd, pipeline_mode=pl.Buffered(1))


def _layer_call(x2d, prm, final_g, final_norm, tiles_per_seq):
    n_tiles = x2d.shape[0] // TILE
    n_steps = n_tiles // 2
    args = (x2d, x2d, prm["ng"], prm["win"], prm["wtl"], prm["gb"], prm["conv"], prm["wal"], prm["bal"], prm["gp"],
            prm["mlg"], prm["glg"], prm["wout"], final_g)
    cur_spec = pl.BlockSpec((2 * TILE, D_MODEL), lambda g: (g, 0))
    nxt_spec = pl.BlockSpec((TILE, D_MODEL), lambda g: (jnp.minimum(2 * g + 2, n_tiles - 1), 0))
    in_specs = [cur_spec, nxt_spec] + [_const_spec(a.shape) for a in args[2:]]
    scratch = [
        pltpu.VMEM((TILE, D_INP), F32),
        pltpu.VMEM((TILE, D_INP), F32),
        pltpu.VMEM((2, 2 * ATT_W, KEYS), BF16),
        pltpu.VMEM((2, KEYS, 2 * ATT_W), BF16),
        pltpu.VMEM((ML_HEADS, TILE, KEYS), F32),
        pltpu.VMEM((TILE + 16, 2 * ML_W), F32),
        pltpu.VMEM((N_PAIRS, LANES, LANES), F32),
        pltpu.VMEM((N_PAIRS, 8, LANES), F32),
        pltpu.VMEM((ML_HEADS, 8, LANES), F32),
        pltpu.VMEM((LANES, GLA_VW), F32),
        pltpu.VMEM((TILE, D_MODEL), F32),
        pltpu.VMEM((N_PAIRS, TILE, 2 * TILE), BF16),
        pltpu.VMEM((D_MODEL, D_INP), BF16),
    ]
    return pl.pallas_call(
        functools.partial(_layer_kernel, final_norm=final_norm, tiles_per_seq=tiles_per_seq),
        out_shape=jax.ShapeDtypeStruct(x2d.shape, x2d.dtype),
        grid=(n_steps,),
        in_specs=in_specs,
        out_specs=cur_spec,
        scratch_shapes=scratch,
        compiler_params=pltpu.CompilerParams(
            dimension_semantics=("arbitrary",),
            vmem_limit_bytes=VMEM_LIMIT),
        name="hybrid_layer_final" if final_norm else "hybrid_layer",
    )(*args)


def _prep_layer(norm_g, w_in_t, b_gates, conv_w, w_alpha, b_alpha, rel_bias, ml_norm_g, gla_norm_g, w_out):
    pad = jnp.zeros((D_INP - SM - GLA_RANK - 2 * ML_HEADS, D_MODEL), w_in_t.dtype)
    win = w_in_t[:W_MAIN].astype(BF16)
    wtl = jnp.concatenate([w_in_t[3084:3980], w_in_t[3996:4252], w_in_t[3980:3996],
                           w_in_t[3072:3084], pad], axis=0).astype(BF16)
    gb = jnp.zeros((1, LANES), F32).at[0, SM_I:SM_I + 2 * ML_HEADS].set(b_gates)
    wal = jnp.zeros((LANES, GLA_KW), F32).at[:GLA_RANK, :].set(w_alpha).astype(BF16)
    nh = rel_bias.shape[0]
    gp = jnp.concatenate([
        jnp.broadcast_to(rel_bias[:, 2 * REL_CLIP:], (nh, KEYS - REL_CLIP + 1)),
        rel_bias[:, 2 * REL_CLIP - 1:0:-1],
        jnp.broadcast_to(rel_bias[:, :1], (nh, ROLL_W - KEYS - REL_CLIP)),
    ], axis=1)
    return dict(ng=norm_g.reshape(1, D_MODEL), win=win, wtl=wtl, gb=gb, conv=conv_w, wal=wal,
                bal=b_alpha.reshape(1, GLA_KW), gp=gp, mlg=ml_norm_g.reshape(1, ML_W),
                glg=gla_norm_g.reshape(1, GLA_VW), wout=w_out.astype(BF16))


def kernel(x, norm_g, w_in, b_gates, conv_w, w_alpha, b_alpha, rel_bias, ml_norm_g, gla_norm_g, w_out, final_g):
    depth = norm_g.shape[0]
    bsz, seq, _ = x.shape
    fg = final_g.reshape(1, D_MODEL)
    w_in_t = jnp.transpose(w_in, (2, 0, 1))
    x2d = x.reshape(bsz * seq, D_MODEL)
    for l in range(depth):
        prm = _prep_layer(norm_g[l], w_in_t[:, l, :], b_gates[l], conv_w[l], w_alpha[l], b_alpha[l], rel_bias[l],
                          ml_norm_g[l], gla_norm_g[l], w_out[l])
        x2d = _layer_call(x2d, prm, fg, final_norm=(l == depth - 1), tiles_per_seq=seq // TILE)
    return x2d.reshape(bsz, seq, D_MODEL)
```

```python
import functools

import jax
import jax.numpy as jnp
from jax import lax
from jax.experimental import pallas as pl
from jax.experimental.pallas import tpu as pltpu

F32 = jnp.float32
BF16 = jnp.bfloat16

D_MODEL = 1024
CHUNK = 64
HEAD_DIM = 64
ATT_W = 384
ML_W = 384
GLA_KW = 128
GLA_VW = 256
GLA_DK = 32
GLA_RANK = 16
GLA_TAU = 16.0
ML_HEADS = 6
PAST_CHUNKS = 8
REL_CLIP = 128
CONV_W = 4
EPS = 1e-6
NEG = -1e30
LOG2E = 1.4426950408889634

LANES = 128
TILE = 256
NCH = TILE // CHUNK
HIST = PAST_CHUNKS * CHUNK
KEYS = HIST + TILE
HALF_ROWS = TILE // 2
HALF_KEYS = HIST + HALF_ROWS
PEN_LANE_A = HEAD_DIM
PEN_LANE_B = 0
BAND = (PAST_CHUNKS + 1) * CHUNK
ROLL_W = 1024
N_PAIRS = ML_W // LANES

AQ, AK, AV, AG = 0, 384, 768, 1152
MQ, MK, MV, MO, MG = 1536, 1920, 2304, 2688, 3072
GQ, GK, GV, GG = 3456, 3584, 3712, 3968
SM = 4224
D_INP = 4352
IP_BLK = 256
W_MAIN = 3072
IP_PER_HEAD, IP_AFTER_ATT, IP_PER_ML_PAIR = 1, 1, 1
SM_I = GLA_RANK
SM_F = GLA_RANK + ML_HEADS

VMEM_LIMIT = 56 * 1024 * 1024


def _log_sigmoid(x):
    return jnp.minimum(x, 0.0) - jnp.log1p(jnp.exp(-jnp.abs(x)))


def _silu(x):
    return x * jax.nn.sigmoid(x)


def _dot_nt(a, b):
    return lax.dot_general(a, b, (((1,), (1,)), ((), ())), preferred_element_type=F32)


def _dot_tn(a, b):
    return lax.dot_general(a, b, (((0,), (0,)), ((), ())), preferred_element_type=F32)


def _dot(a, b):
    return jnp.dot(a, b, preferred_element_type=F32)


def _pair_rms(hcur, gain, m0):
    sq = hcur * hcur
    ms0 = jnp.sum(jnp.where(m0, sq, 0.0), axis=1, keepdims=True)
    ms1 = jnp.sum(jnp.where(m0, 0.0, sq), axis=1, keepdims=True)
    ms = jnp.where(m0, ms0, ms1) * (1.0 / HEAD_DIM)
    return hcur * lax.rsqrt(ms + EPS) * gain


def _step(i, zc, zn, xn_ref, xc_ref, ng_ref, w_cols, gb_ref, conv_ref, wal_ref, bal_ref,
          mlg_ref, glg_ref, wout_ref, fg_ref, o_ref,
          kbuf, vbuf, kold, vold, bias_ref, cbuf, ml_c, ml_n, ml_m, gla_s, y_ref, sc_ref, *, final_norm):
    lane = lax.broadcasted_iota(jnp.int32, (1, LANES), 1)
    m0 = lane < HEAD_DIM
    m1 = jnp.logical_not(m0)

    xn = xn_ref[...]
    hn = xn * lax.rsqrt(jnp.sum(xn * xn, axis=1, keepdims=True) * (1.0 / D_MODEL) + EPS) * ng_ref[...]
    hn_b = hn.astype(BF16)
    ip_next = [0]

    def ip(count):
        for _ in range(count):
            n0 = ip_next[0] * IP_BLK
            if n0 < D_INP:
                zn[:, n0:n0 + IP_BLK] = _dot(hn_b, w_cols(n0))
                ip_next[0] += 1

    kbuf[:, 0:HIST] = kold[:, TILE:KEYS]
    vbuf[0:HIST, :] = vold[TILE:KEYS, :]
    frow = lax.broadcasted_iota(jnp.int32, (LANES, 1), 0) < HEAD_DIM
    for p in range(N_PAIRS):
        c0 = p * LANES
        kn_t = zc[:, AK + c0:AK + c0 + LANES].T
        vn = zc[:, AV + c0:AV + c0 + LANES]
        kbuf[c0:c0 + LANES, HIST:KEYS] = jnp.where(frow, kn_t, 0.0).astype(BF16)
        kbuf[ATT_W + c0:ATT_W + c0 + LANES, HIST:KEYS] = jnp.where(frow, 0.0, kn_t).astype(BF16)
        vbuf[HIST:KEYS, c0:c0 + LANES] = jnp.where(m0, vn, 1.0).astype(BF16)
        vbuf[HIST:KEYS, ATT_W + c0:ATT_W + c0 + LANES] = jnp.where(m0, 1.0, vn).astype(BF16)

    def att_scores(u):
        h, half = u // 2, u % 2
        c0 = (h // 2) * LANES
        cb = (h % 2) * ATT_W + c0
        r0, k0 = half * HALF_ROWS, half * HALF_ROWS
        q2 = zc[r0:r0 + HALF_ROWS, AQ + c0:AQ + c0 + LANES] * (HEAD_DIM ** -0.5 * LOG2E)
        if h % 2 == 0:
            qm = jnp.where(m0, q2, jnp.where(lane == PEN_LANE_A, 1.0, 0.0))
        else:
            qm = jnp.where(m1, q2, jnp.where(lane == PEN_LANE_B, 1.0, 0.0))
        return (_dot(qm.astype(BF16), kbuf[cb:cb + LANES, k0:k0 + HALF_KEYS])
                + bias_ref[h, r0:r0 + HALF_ROWS, k0:k0 + HALF_KEYS])

    def att_out(u, s):
        h, half = u // 2, u % 2
        cb = (h % 2) * ATT_W + (h // 2) * LANES
        k0 = half * HALF_ROWS
        pe = jnp.exp2(s - jnp.max(s, axis=1, keepdims=True))
        return _dot(pe.astype(BF16), vbuf[k0:k0 + HALF_KEYS, cb:cb + LANES])

    n_units = 2 * ML_HEADS
    s_cur = att_scores(0)
    outs = []
    for u in range(n_units):
        s_nxt = att_scores(u + 1) if u + 1 < n_units else None
        if u % 2 == 1:
            ip(IP_PER_HEAD)
        outs.append(att_out(u, s_cur))
        s_cur = s_nxt
        if u % 4 == 3:
            c0 = (u // 4) * LANES
            r_a = jnp.concatenate(outs[u - 3:u - 1], axis=0)
            r_b = jnp.concatenate(outs[u - 1:u + 1], axis=0)
            den = pltpu.roll(jnp.where(m0, r_b, r_a), HEAD_DIM, 1)
            att = jnp.where(m0, r_a, r_b) / den
            y_ref[:, c0:c0 + LANES] = att * _silu(zc[:, AG + c0:AG + c0 + LANES])
    ip(IP_AFTER_ATT)

    cbuf[8:8 + TILE, :] = zc[:, MQ:MQ + 2 * ML_W]
    for c0 in range(0, 2 * ML_W, LANES):
        acc = cbuf[8:8 + TILE, c0:c0 + LANES] * conv_ref[CONV_W - 1:CONV_W, c0:c0 + LANES]
        for sft in range(1, CONV_W):
            acc = acc + cbuf[8 - sft:8 - sft + TILE, c0:c0 + LANES] * \
                conv_ref[CONV_W - 1 - sft:CONV_W - sft, c0:c0 + LANES]
        act = _silu(acc)
        if c0 >= ML_W:
            act = act * (HEAD_DIM ** -0.5)
        zc[:, MQ + c0:MQ + c0 + LANES] = act
    cbuf[0:8, :] = cbuf[TILE:TILE + 8, :]

    small = zc[:, SM:SM + LANES]
    pre = small + gb_ref[...]
    lf = _log_sigmoid(pre)
    la = _log_sigmoid(_dot(small.astype(BF16), wal_ref[...]) + bal_ref[...]) * (1.0 / GLA_TAU)
    tr = lax.broadcasted_iota(jnp.int32, (TILE, TILE), 0)
    tc = lax.broadcasted_iota(jnp.int32, (TILE, TILE), 1)
    tri = jnp.where(tr >= tc, 1.0, 0.0).astype(BF16)
    terms = []
    for v in (lf, la):
        hi = v.astype(BF16)
        r1 = v - hi.astype(F32)
        mid = r1.astype(BF16)
        terms += [hi, mid, (r1 - mid.astype(F32)).astype(BF16)]
    cs = _dot(tri, jnp.concatenate(terms, axis=1))
    bml = (cs[:, 2 * LANES:3 * LANES] + cs[:, LANES:2 * LANES]) + cs[:, 0:LANES]
    bla = (cs[:, 5 * LANES:6 * LANES] + cs[:, 4 * LANES:5 * LANES]) + cs[:, 3 * LANES:4 * LANES]
    bgl = jnp.concatenate(
        [bla[0:CHUNK]] + [bla[c * CHUNK:(c + 1) * CHUNK] - bla[c * CHUNK - 1:c * CHUNK]
                          for c in range(1, NCH)], axis=0)
    pre_t = pre.T
    bml_t = bml.T
    bgl_t = bgl.T
    ip(1)

    prow = lax.broadcasted_iota(jnp.int32, (LANES, LANES), 0)
    pcol = lax.broadcasted_iota(jnp.int32, (LANES, LANES), 1)
    pair_diag = (prow < HEAD_DIM) == (pcol < HEAD_DIM)
    prow_first = prow[:, 0:1] < HEAD_DIM

    ml_kt, ml_vb, ml_vv, ml_s, ml_ni, ml_ct, ml_nc = [], [], [], [], [], [], []
    for p in range(N_PAIRS):
        c0 = p * LANES
        q2 = zc[:, MQ + c0:MQ + c0 + LANES]
        k2 = zc[:, MK + c0:MK + c0 + LANES]
        v2 = zc[:, MV + c0:MV + c0 + LANES]
        q2b = q2.astype(BF16)
        k2_t = k2.T
        kk_t = jnp.concatenate([jnp.where(frow, k2_t, 0.0), jnp.where(frow, 0.0, k2_t)],
                               axis=1).astype(BF16)
        ct = ml_c[p]
        ncol = ml_n[p][:, 0:1]
        n_ext = jnp.where(lane == 0, jnp.where(frow, ncol, 0.0),
                          jnp.where(lane == 1, jnp.where(frow, 0.0, ncol), 0.0))
        ml_kt.append(k2_t)
        ml_nc.append(ncol)
        ml_vb.append(v2.astype(BF16))
        ml_vv.append(jnp.concatenate([jnp.where(m0, v2, 0.0), jnp.where(m0, 0.0, v2)],
                                     axis=0).astype(BF16))
        ml_ct.append(ct)
        ml_s.append(_dot(q2b, kk_t))
        ml_ni.append(_dot(q2b, jnp.concatenate([ct, n_ext], axis=1).astype(BF16)))
    ip(2)

    for p in range(N_PAIRS):
        c0 = p * LANES
        s_pair = ml_s[p]
        iws, dens, wrows, solds = [], [], [], []
        for e in range(2):
            h = 2 * p + e
            mprev = ml_m[h][0:1, :]
            bt_all = jnp.broadcast_to(bml[:, SM_F + h:SM_F + h + 1], (TILE, LANES))
            bj = bml_t[SM_F + h:SM_F + h + 1, :]
            ij = pre_t[SM_I + h:SM_I + h + 1, :]
            den_inter = ml_ni[p][:, LANES + e:LANES + e + 1]
            iw_blocks, den_blocks = [], []
            for rb in range(NCH):
                r0 = rb * CHUNK
                ncol = LANES * ((r0 + CHUNK + LANES - 1) // LANES)
                rr = lax.broadcasted_iota(jnp.int32, (CHUNK, ncol), 0) + r0
                kc = lax.broadcasted_iota(jnp.int32, (CHUNK, ncol), 1)
                btc = bt_all[r0:r0 + CHUNK, 0:1]
                log_d = jnp.where(rr >= kc, btc - bj[:, :ncol] + ij[:, :ncol], NEG)
                inter_log = btc + mprev[:, 0:1]
                m_row = jnp.maximum(inter_log, jnp.max(log_d, axis=1, keepdims=True))
                sc = s_pair[r0:r0 + CHUNK, e * TILE:e * TILE + ncol] * jnp.exp(log_d - m_row)
                sc_ref[p, r0:r0 + CHUNK, e * TILE:e * TILE + ncol] = sc.astype(BF16)
                if ncol < TILE:
                    sc_ref[p, r0:r0 + CHUNK, e * TILE + ncol:(e + 1) * TILE] = jnp.zeros(
                        (CHUNK, TILE - ncol), BF16)
                inter_w = jnp.exp(inter_log - m_row)
                den = jnp.sum(sc, axis=1, keepdims=True) + inter_w * den_inter[r0:r0 + CHUNK]
                den_blocks.append(jnp.maximum(jnp.abs(den), jnp.exp(-m_row)))
                iw_blocks.append(inter_w)
            iws.append(jnp.concatenate(iw_blocks, axis=0))
            dens.append(jnp.concatenate(den_blocks, axis=0))
            g_row = bt_all[TILE - 1:TILE, :]
            a_row = g_row[:, 0:1] - bj + ij
            m_new = jnp.maximum(g_row + mprev, jnp.max(a_row, axis=1, keepdims=True))
            wrows.append(jnp.exp(a_row - m_new[:, 0:1]))
            solds.append(jnp.exp(g_row + mprev - m_new))
            ml_m[h] = jnp.broadcast_to(m_new, (8, LANES))
        num = _dot(sc_ref[p], ml_vv[p]) + jnp.where(m0, iws[0], iws[1]) * ml_ni[p][:, :LANES]
        hcur = num / jnp.where(m0, dens[0], dens[1])
        y_ref[:, ATT_W + c0:ATT_W + c0 + LANES] = (
            jax.nn.sigmoid(zc[:, MO + c0:MO + c0 + LANES])
            * _pair_rms(hcur, mlg_ref[:, c0:c0 + LANES], m0)
            * _silu(zc[:, MG + c0:MG + c0 + LANES]))
        kw_t = ml_kt[p] * jnp.where(frow, wrows[0], wrows[1])
        upd = _dot(kw_t.astype(BF16), ml_vb[p])
        sold_col = jnp.where(prow_first, solds[0][:, 0:1], solds[1][:, 0:1])
        ml_c[p] = jnp.where(pair_diag, sold_col * ml_ct[p] + upd, 0.0)
        ncol = sold_col * ml_nc[p] + jnp.sum(kw_t, axis=1, keepdims=True)
        ml_n[p] = jnp.broadcast_to(ncol, (LANES, LANES))
        ip(IP_PER_ML_PAIR)

    grow = lax.broadcasted_iota(jnp.int32, (LANES, GLA_VW), 0)
    gcol = lax.broadcasted_iota(jnp.int32, (LANES, GLA_VW), 1)
    gla_diag = (grow >> 5) == (gcol >> 6)
    khead = lane >> 5
    vhead = lax.broadcasted_iota(jnp.int32, (1, GLA_VW), 1) >> 6
    arow = lax.broadcasted_iota(jnp.int32, (CHUNK, GLA_VW), 0)
    acol = lax.broadcasted_iota(jnp.int32, (CHUNK, GLA_VW), 1)
    causal4 = arow >= (acol & (CHUNK - 1))
    n_gh = GLA_KW // GLA_DK
    yc0 = ATT_W + ML_W
    g_a, g_upd, g_qi, g_vvg, g_dec = [], [], [], [], []
    for c in range(NCH):
        r0 = c * CHUNK
        bc = bgl[r0:r0 + CHUNK, :]
        bref = bc[CHUNK // 2 - 1:CHUNK // 2, :]
        btot = bc[CHUNK - 1:CHUNK, :]
        gq = zc[r0:r0 + CHUNK, GQ:GQ + GLA_KW] * (GLA_DK ** -0.5)
        gk = zc[r0:r0 + CHUNK, GK:GK + GLA_KW]
        gv = zc[r0:r0 + CHUNK, GV:GV + GLA_VW]
        qe = (gq * jnp.exp(bc - bref)).astype(BF16)
        ke = gk * jnp.exp(bref - bc)
        kek = jnp.concatenate([jnp.where(khead == h, ke, 0.0) for h in range(n_gh)],
                              axis=0).astype(BF16)
        g_vvg.append(jnp.concatenate([jnp.where(vhead == h, gv, 0.0) for h in range(n_gh)],
                                     axis=0).astype(BF16))
        g_qi.append((gq * jnp.exp(bc)).astype(BF16))
        kd = (gk * jnp.exp(btot - bc)).astype(BF16)
        g_dec.append(jnp.exp(bgl_t[:, r0 + CHUNK - 1:r0 + CHUNK]))
        g_a.append(_dot_nt(qe, kek))
        g_upd.append(_dot_tn(kd, gv.astype(BF16)))
    ip(2)
    s_all = gla_s[...]
    for c in range(NCH):
        r0 = c * CHUNK
        a = jnp.where(causal4, g_a[c], 0.0)
        o = _dot(g_qi[c], s_all.astype(BF16)) + _dot(a.astype(BF16), g_vvg[c])
        y_ref[r0:r0 + CHUNK, yc0:yc0 + GLA_VW] = o
        s_all = jnp.where(gla_diag, g_dec[c] * s_all + g_upd[c], 0.0)
        ip(1)
    gla_s[...] = s_all

    for hp in range(GLA_VW // LANES):
        c0 = hp * LANES
        hcur = y_ref[:, yc0 + c0:yc0 + c0 + LANES]
        y_ref[:, yc0 + c0:yc0 + c0 + LANES] = (
            _pair_rms(hcur, glg_ref[:, c0:c0 + LANES], m0) * _silu(zc[:, GG + c0:GG + c0 + LANES]))
    ip(D_INP // IP_BLK)

    xo = xc_ref[...] + _dot(y_ref[...].astype(BF16), wout_ref[...])
    if final_norm:
        xo = xo * lax.rsqrt(jnp.sum(xo * xo, axis=1, keepdims=True) * (1.0 / D_MODEL) + EPS) * fg_ref[...]
    o_ref[...] = xo


def _project(x, ng_ref, w_cols, z_out):
    hn = x * lax.rsqrt(jnp.sum(x * x, axis=1, keepdims=True) * (1.0 / D_MODEL) + EPS) * ng_ref[...]
    hn_b = hn.astype(BF16)
    for n0 in range(0, D_INP, IP_BLK):
        z_out[:, n0:n0 + IP_BLK] = _dot(hn_b, w_cols(n0))


def _layer_kernel(xa_ref, xn_ref, ng_ref, win_ref, wtl_ref, gb_ref, conv_ref, wal_ref, bal_ref, gp_ref,
                  mlg_ref, glg_ref, wout_ref, fg_ref,
                  o_ref,
                  z_a, z_b, kb, vb, bias_ref, cbuf, ml_c, ml_n, ml_m, gla_s, y_ref, sc_ref, wrm,
                  *, final_norm, tiles_per_seq):
    g = pl.program_id(0)
    i = lax.rem(2 * g, tiles_per_seq)

    def w_rows(n0):
        if n0 < W_MAIN:
            return win_ref[n0:n0 + IP_BLK, :]
        return wtl_ref[n0 - W_MAIN:n0 - W_MAIN + IP_BLK, :]

    def w_cols(n0):
        return wrm[:, n0:n0 + IP_BLK]

    @pl.when(g == 0)
    def _first_step():
        r = lax.broadcasted_iota(jnp.int32, (TILE, KEYS), 0)
        m = lax.broadcasted_iota(jnp.int32, (TILE, KEYS), 1)
        jj = m - ((r >> 6) << 6)
        in_band = (jj >= 0) & (jj < BAND)
        for h in range(ML_HEADS):
            row = jnp.broadcast_to(gp_ref[h:h + 1, :], (TILE, ROLL_W))
            rolled = pltpu.roll(row, KEYS, 1, stride=1, stride_axis=0)
            bias_ref[h] = jnp.where(in_band, rolled[:, :KEYS] * LOG2E, NEG)
        for n0 in range(0, D_INP, IP_BLK):
            wrm[:, n0:n0 + IP_BLK] = w_rows(n0).astype(F32).T.astype(BF16)
        _project(xa_ref[0:TILE, :], ng_ref, w_cols, z_a)

    @pl.when(i == 0)
    def _reset():
        kr = lax.broadcasted_iota(jnp.int32, (2 * ATT_W, 1), 0)
        pen_row = (kr & (LANES - 1)) == jnp.where(kr < ATT_W, PEN_LANE_A, PEN_LANE_B)
        pen_init = jnp.broadcast_to(jnp.where(pen_row, NEG, 0.0), kb.shape[1:]).astype(kb.dtype)
        for slot in range(2):
            kb[slot] = pen_init
            vb[slot] = jnp.zeros(vb.shape[1:], vb.dtype)
        cbuf[0:8, :] = jnp.zeros((8, cbuf.shape[1]), F32)
        ml_c[...] = jnp.zeros(ml_c.shape, F32)
        ml_n[...] = jnp.zeros(ml_n.shape, F32)
        ml_m[...] = jnp.zeros(ml_m.shape, F32)
        gla_s[...] = jnp.zeros(gla_s.shape, F32)

    step = functools.partial(
        _step, ng_ref=ng_ref, w_cols=w_cols, gb_ref=gb_ref,
        conv_ref=conv_ref, wal_ref=wal_ref, bal_ref=bal_ref, mlg_ref=mlg_ref, glg_ref=glg_ref,
        wout_ref=wout_ref, fg_ref=fg_ref, bias_ref=bias_ref,
        cbuf=cbuf, ml_c=ml_c, ml_n=ml_n, ml_m=ml_m, gla_s=gla_s, y_ref=y_ref, sc_ref=sc_ref,
        final_norm=final_norm)

    lo = pl.ds(0, TILE)
    hi = pl.ds(TILE, TILE)
    step(i, z_a, z_b, xn_ref=xa_ref.at[hi], xc_ref=xa_ref.at[lo], o_ref=o_ref.at[lo],
         kbuf=kb.at[0], vbuf=vb.at[0], kold=kb.at[1], vold=vb.at[1])
    step(i + 1, z_b, z_a, xn_ref=xn_ref, xc_ref=xa_ref.at[hi], o_ref=o_ref.at[hi],
         kbuf=kb.at[1], vbuf=vb.at[1], kold=kb.at[0], vold=vb.at[0])


def _const_spec(shape):
    nd = len(shape)
    return pl.BlockSpec(shape, lambda g, _nd=nd: (0,) * _nd, pipeline_mode=pl.Buffered(1))


def _layer_call(x2d, prm, final_g, final_norm, tiles_per_seq):
    n_tiles = x2d.shape[0] // TILE
    n_steps = n_tiles // 2
    args = (x2d, x2d, prm["ng"], prm["win"], prm["wtl"], prm["gb"], prm["conv"], prm["wal"], prm["bal"], prm["gp"],
            prm["mlg"], prm["glg"], prm["wout"], final_g)
    cur_spec = pl.BlockSpec((2 * TILE, D_MODEL), lambda g: (g, 0))
    nxt_spec = pl.BlockSpec((TILE, D_MODEL), lambda g: (jnp.minimum(2 * g + 2, n_tiles - 1), 0))
    in_specs = [cur_spec, nxt_spec] + [_const_spec(a.shape) for a in args[2:]]
    scratch = [
        pltpu.VMEM((TILE, D_INP), F32),
        pltpu.VMEM((TILE, D_INP), F32),
        pltpu.VMEM((2, 2 * ATT_W, KEYS), BF16),
        pltpu.VMEM((2, KEYS, 2 * ATT_W), BF16),
        pltpu.VMEM((ML_HEADS, TILE, KEYS), F32),
        pltpu.VMEM((TILE + 16, 2 * ML_W), F32),
        pltpu.VMEM((N_PAIRS, LANES, LANES), F32),
        pltpu.VMEM((N_PAIRS, LANES, LANES), F32),
        pltpu.VMEM((ML_HEADS, 8, LANES), F32),
        pltpu.VMEM((LANES, GLA_VW), F32),
        pltpu.VMEM((TILE, D_MODEL), F32),
        pltpu.VMEM((N_PAIRS, TILE, 2 * TILE), BF16),
        pltpu.VMEM((D_MODEL, D_INP), BF16),
    ]
    return pl.pallas_call(
        functools.partial(_layer_kernel, final_norm=final_norm, tiles_per_seq=tiles_per_seq),
        out_shape=jax.ShapeDtypeStruct(x2d.shape, x2d.dtype),
        grid=(n_steps,),
        in_specs=in_specs,
        out_specs=cur_spec,
        scratch_shapes=scratch,
        compiler_params=pltpu.CompilerParams(
            dimension_semantics=("arbitrary",),
            vmem_limit_bytes=VMEM_LIMIT),
        name="hybrid_layer_final" if final_norm else "hybrid_layer",
    )(*args)


def _prep_layer(norm_g, w_in_t, b_gates, conv_w, w_alpha, b_alpha, rel_bias, ml_norm_g, gla_norm_g, w_out):
    pad = jnp.zeros((D_INP - SM - GLA_RANK - 2 * ML_HEADS, D_MODEL), w_in_t.dtype)
    win = w_in_t[:W_MAIN].astype(BF16)
    wtl = jnp.concatenate([w_in_t[3084:3980], w_in_t[3996:4252], w_in_t[3980:3996],
                           w_in_t[3072:3084], pad], axis=0).astype(BF16)
    gb = jnp.zeros((1, LANES), F32).at[0, SM_I:SM_I + 2 * ML_HEADS].set(b_gates)
    wal = jnp.zeros((LANES, GLA_KW), F32).at[:GLA_RANK, :].set(w_alpha).astype(BF16)
    nh = rel_bias.shape[0]
    gp = jnp.concatenate([
        jnp.broadcast_to(rel_bias[:, 2 * REL_CLIP:], (nh, KEYS - REL_CLIP + 1)),
        rel_bias[:, 2 * REL_CLIP - 1:0:-1],
        jnp.broadcast_to(rel_bias[:, :1], (nh, ROLL_W - KEYS - REL_CLIP)),
    ], axis=1)
    return dict(ng=norm_g.reshape(1, D_MODEL), win=win, wtl=wtl, gb=gb, conv=conv_w, wal=wal,
                bal=b_alpha.reshape(1, GLA_KW), gp=gp, mlg=ml_norm_g.reshape(1, ML_W),
                glg=gla_norm_g.reshape(1, GLA_VW), wout=w_out.astype(BF16))


def kernel(x, norm_g, w_in, b_gates, conv_w, w_alpha, b_alpha, rel_bias, ml_norm_g, gla_norm_g, w_out, final_g):
    depth = norm_g.shape[0]
    bsz, seq, _ = x.shape
    fg = final_g.reshape(1, D_MODEL)
    w_in_t = jnp.transpose(w_in, (2, 0, 1))
    x2d = x.reshape(bsz * seq, D_MODEL)
    for l in range(depth):
        prm = _prep_layer(norm_g[l], w_in_t[:, l, :], b_gates[l], conv_w[l], w_alpha[l], b_alpha[l], rel_bias[l],
                          ml_norm_g[l], gla_norm_g[l], w_out[l])
        x2d = _layer_call(x2d, prm, fg, final_norm=(l == depth - 1), tiles_per_seq=seq // TILE)
    return x2d.reshape(bsz, seq, D_MODEL)
```

```python
import functools

import jax
import jax.numpy as jnp
from jax import lax
from jax.experimental import pallas as pl
from jax.experimental.pallas import tpu as pltpu

F32 = jnp.float32
BF16 = jnp.bfloat16

D_MODEL = 1024
CHUNK = 64
HEAD_DIM = 64
ATT_W = 384
ML_W = 384
GLA_KW = 128
GLA_VW = 256
GLA_DK = 32
GLA_RANK = 16
GLA_TAU = 16.0
ML_HEADS = 6
PAST_CHUNKS = 8
REL_CLIP = 128
CONV_W = 4
EPS = 1e-6
NEG = -1e30
LOG2E = 1.4426950408889634

LANES = 128
TILE = 256
NCH = TILE // CHUNK
HIST = PAST_CHUNKS * CHUNK
KEYS = HIST + TILE
HALF_ROWS = TILE // 2
HALF_KEYS = HIST + HALF_ROWS
PEN_LANE_A = HEAD_DIM
PEN_LANE_B = 0
BAND = (PAST_CHUNKS + 1) * CHUNK
ROLL_W = 1024
N_PAIRS = ML_W // LANES

AQ, AK, AV, AG = 0, 384, 768, 1152
MQ, MK, MV, MO, MG = 1536, 1920, 2304, 2688, 3072
GQ, GK, GV, GG = 3456, 3584, 3712, 3968
SM = 4224
D_INP = 4352
IP_BLK = 256
W_MAIN = 3072
IP_PER_HEAD, IP_AFTER_ATT, IP_PER_ML_PAIR = 1, 1, 1
SM_I = GLA_RANK
SM_F = GLA_RANK + ML_HEADS

VMEM_LIMIT = 56 * 1024 * 1024


def _log_sigmoid(x):
    return jnp.minimum(x, 0.0) - jnp.log1p(jnp.exp(-jnp.abs(x)))


def _silu(x):
    return x * jax.nn.sigmoid(x)


def _dot_nt(a, b):
    return lax.dot_general(a, b, (((1,), (1,)), ((), ())), preferred_element_type=F32)


def _dot_tn(a, b):
    return lax.dot_general(a, b, (((0,), (0,)), ((), ())), preferred_element_type=F32)


def _dot(a, b):
    return jnp.dot(a, b, preferred_element_type=F32)


def _pair_rms(hcur, gain, m0):
    sq = hcur * hcur
    ms0 = jnp.sum(jnp.where(m0, sq, 0.0), axis=1, keepdims=True)
    ms1 = jnp.sum(jnp.where(m0, 0.0, sq), axis=1, keepdims=True)
    ms = jnp.where(m0, ms0, ms1) * (1.0 / HEAD_DIM)
    return hcur * lax.rsqrt(ms + EPS) * gain


def _step(i, zc, zn, xn_ref, xc_ref, ng_ref, w_cols, gb_ref, conv_ref, wal_ref, bal_ref,
          mlg_ref, glg_ref, wout_ref, fg_ref, o_ref,
          kbuf, vbuf, kold, vold, bias_ref, cbuf, ml_c, ml_n, ml_m, gla_s, y_ref, sc_ref, *, final_norm):
    lane = lax.broadcasted_iota(jnp.int32, (1, LANES), 1)
    m0 = lane < HEAD_DIM
    m1 = jnp.logical_not(m0)

    xn = xn_ref[...]
    hn = xn * lax.rsqrt(jnp.sum(xn * xn, axis=1, keepdims=True) * (1.0 / D_MODEL) + EPS) * ng_ref[...]
    hn_b = hn.astype(BF16)
    ip_next = [0]

    def ip(count):
        for _ in range(count):
            n0 = ip_next[0] * IP_BLK
            if n0 < D_INP:
                zn[:, n0:n0 + IP_BLK] = _dot(hn_b, w_cols(n0))
                ip_next[0] += 1

    kbuf[:, 0:HIST] = kold[:, TILE:KEYS]
    vbuf[0:HIST, :] = vold[TILE:KEYS, :]
    frow = lax.broadcasted_iota(jnp.int32, (LANES, 1), 0) < HEAD_DIM
    for p in range(N_PAIRS):
        c0 = p * LANES
        kn_t = zc[:, AK + c0:AK + c0 + LANES].T
        vn = zc[:, AV + c0:AV + c0 + LANES]
        kbuf[c0:c0 + LANES, HIST:KEYS] = jnp.where(frow, kn_t, 0.0).astype(BF16)
        kbuf[ATT_W + c0:ATT_W + c0 + LANES, HIST:KEYS] = jnp.where(frow, 0.0, kn_t).astype(BF16)
        vbuf[HIST:KEYS, c0:c0 + LANES] = jnp.where(m0, vn, 1.0).astype(BF16)
        vbuf[HIST:KEYS, ATT_W + c0:ATT_W + c0 + LANES] = jnp.where(m0, 1.0, vn).astype(BF16)

    def att_scores(u):
        h, half = u // 2, u % 2
        c0 = (h // 2) * LANES
        cb = (h % 2) * ATT_W + c0
        r0, k0 = half * HALF_ROWS, half * HALF_ROWS
        q2 = zc[r0:r0 + HALF_ROWS, AQ + c0:AQ + c0 + LANES] * (HEAD_DIM ** -0.5 * LOG2E)
        if h % 2 == 0:
            qm = jnp.where(m0, q2, jnp.where(lane == PEN_LANE_A, 1.0, 0.0))
        else:
            qm = jnp.where(m1, q2, jnp.where(lane == PEN_LANE_B, 1.0, 0.0))
        return (_dot(qm.astype(BF16), kbuf[cb:cb + LANES, k0:k0 + HALF_KEYS])
                + bias_ref[h, r0:r0 + HALF_ROWS, k0:k0 + HALF_KEYS])

    def att_out(u, s):
        h, half = u // 2, u % 2
        cb = (h % 2) * ATT_W + (h // 2) * LANES
        k0 = half * HALF_ROWS
        pe = jnp.exp2(s - jnp.max(s, axis=1, keepdims=True))
        return _dot(pe.astype(BF16), vbuf[k0:k0 + HALF_KEYS, cb:cb + LANES])

    n_units = 2 * ML_HEADS
    s_cur = att_scores(0)
    outs = []
    for u in range(n_units):
        s_nxt = att_scores(u + 1) if u + 1 < n_units else None
        if u % 2 == 1:
            ip(IP_PER_HEAD)
        outs.append(att_out(u, s_cur))
        s_cur = s_nxt
        if u % 4 == 3:
            c0 = (u // 4) * LANES
            r_a = jnp.concatenate(outs[u - 3:u - 1], axis=0)
            r_b = jnp.concatenate(outs[u - 1:u + 1], axis=0)
            den = pltpu.roll(jnp.where(m0, r_b, r_a), HEAD_DIM, 1)
            att = jnp.where(m0, r_a, r_b) / den
            y_ref[:, c0:c0 + LANES] = att * _silu(zc[:, AG + c0:AG + c0 + LANES])
    ip(IP_AFTER_ATT)

    cbuf[8:8 + TILE, :] = zc[:, MQ:MQ + 2 * ML_W]
    for c0 in range(0, 2 * ML_W, LANES):
        acc = cbuf[8:8 + TILE, c0:c0 + LANES] * conv_ref[CONV_W - 1:CONV_W, c0:c0 + LANES]
        for sft in range(1, CONV_W):
            acc = acc + cbuf[8 - sft:8 - sft + TILE, c0:c0 + LANES] * \
                conv_ref[CONV_W - 1 - sft:CONV_W - sft, c0:c0 + LANES]
        act = _silu(acc)
        if c0 >= ML_W:
            act = act * (HEAD_DIM ** -0.5)
        zc[:, MQ + c0:MQ + c0 + LANES] = act
    cbuf[0:8, :] = cbuf[TILE:TILE + 8, :]

    small = zc[:, SM:SM + LANES]
    pre = small + gb_ref[...]
    lf = _log_sigmoid(pre)
    la = _log_sigmoid(_dot(small.astype(BF16), wal_ref[...]) + bal_ref[...]) * (1.0 / GLA_TAU)
    tr = lax.broadcasted_iota(jnp.int32, (TILE, TILE), 0)
    tc = lax.broadcasted_iota(jnp.int32, (TILE, TILE), 1)
    tri = jnp.where(tr >= tc, 1.0, 0.0).astype(BF16)
    terms = []
    for v in (lf, la):
        hi = v.astype(BF16)
        r1 = v - hi.astype(F32)
        mid = r1.astype(BF16)
        terms += [hi, mid, (r1 - mid.astype(F32)).astype(BF16)]
    cs = _dot(tri, jnp.concatenate(terms, axis=1))
    bml = (cs[:, 2 * LANES:3 * LANES] + cs[:, LANES:2 * LANES]) + cs[:, 0:LANES]
    bla = (cs[:, 5 * LANES:6 * LANES] + cs[:, 4 * LANES:5 * LANES]) + cs[:, 3 * LANES:4 * LANES]
    bgl = jnp.concatenate(
        [bla[0:CHUNK]] + [bla[c * CHUNK:(c + 1) * CHUNK] - bla[c * CHUNK - 1:c * CHUNK]
                          for c in range(1, NCH)], axis=0)
    pre_t = pre.T
    bml_t = bml.T
    bgl_t = bgl.T
    ip(1)

    prow = lax.broadcasted_iota(jnp.int32, (LANES, LANES), 0)
    pcol = lax.broadcasted_iota(jnp.int32, (LANES, LANES), 1)
    pair_diag = (prow < HEAD_DIM) == (pcol < HEAD_DIM)
    prow_first = prow[:, 0:1] < HEAD_DIM

    ml_kt, ml_vb, ml_vv, ml_s, ml_ni, ml_ct, ml_nc = [], [], [], [], [], [], []
    for p in range(N_PAIRS):
        c0 = p * LANES
        q2 = zc[:, MQ + c0:MQ + c0 + LANES]
        k2 = zc[:, MK + c0:MK + c0 + LANES]
        v2 = zc[:, MV + c0:MV + c0 + LANES]
        q2b = q2.astype(BF16)
        k2_t = k2.T
        kk_t = jnp.concatenate([jnp.where(frow, k2_t, 0.0), jnp.where(frow, 0.0, k2_t)],
                               axis=1).astype(BF16)
        ct = ml_c[p]
        ncol = ml_n[p][:, 0:1]
        n_ext = jnp.where(lane == 0, jnp.where(frow, ncol, 0.0),
                          jnp.where(lane == 1, jnp.where(frow, 0.0, ncol), 0.0))
        ml_kt.append(k2_t)
        ml_nc.append(ncol)
        ml_vb.append(v2.astype(BF16))
        ml_vv.append(jnp.concatenate([jnp.where(m0, v2, 0.0), jnp.where(m0, 0.0, v2)],
                                     axis=0).astype(BF16))
        ml_ct.append(ct)
        ml_s.append(_dot(q2b, kk_t))
        ml_ni.append(_dot(q2b, jnp.concatenate([ct, n_ext], axis=1).astype(BF16)))
    ip(2)

    for p in range(N_PAIRS):
        c0 = p * LANES
        s_pair = ml_s[p]
        iws, dens, wrows, solds = [], [], [], []
        for e in range(2):
            h = 2 * p + e
            mprev = ml_m[h][0:1, :]
            bt_all = jnp.broadcast_to(bml[:, SM_F + h:SM_F + h + 1], (TILE, LANES))
            bj = bml_t[SM_F + h:SM_F + h + 1, :]
            ij = pre_t[SM_I + h:SM_I + h + 1, :]
            den_inter = ml_ni[p][:, LANES + e:LANES + e + 1]
            iw_blocks, den_blocks = [], []
            for rb in range(NCH):
                r0 = rb * CHUNK
                ncol = LANES * ((r0 + CHUNK + LANES - 1) // LANES)
                rr = lax.broadcasted_iota(jnp.int32, (CHUNK, ncol), 0) + r0
                kc = lax.broadcasted_iota(jnp.int32, (CHUNK, ncol), 1)
                btc = bt_all[r0:r0 + CHUNK, 0:1]
                log_d = jnp.where(rr >= kc, btc - bj[:, :ncol] + ij[:, :ncol], NEG)
                inter_log = btc + mprev[:, 0:1]
                m_row = jnp.maximum(inter_log, jnp.max(log_d, axis=1, keepdims=True))
                sc = s_pair[r0:r0 + CHUNK, e * TILE:e * TILE + ncol] * jnp.exp(log_d - m_row)
                sc_ref[p, r0:r0 + CHUNK, e * TILE:e * TILE + ncol] = sc.astype(BF16)
                if ncol < TILE:
                    sc_ref[p, r0:r0 + CHUNK, e * TILE + ncol:(e + 1) * TILE] = jnp.zeros(
                        (CHUNK, TILE - ncol), BF16)
                inter_w = jnp.exp(inter_log - m_row)
                den = jnp.sum(sc, axis=1, keepdims=True) + inter_w * den_inter[r0:r0 + CHUNK]
                den_blocks.append(jnp.maximum(jnp.abs(den), jnp.exp(-m_row)))
                iw_blocks.append(inter_w)
            iws.append(jnp.concatenate(iw_blocks, axis=0))
            dens.append(jnp.concatenate(den_blocks, axis=0))
            g_row = bt_all[TILE - 1:TILE, :]
            a_row = g_row[:, 0:1] - bj + ij
            m_new = jnp.maximum(g_row + mprev, jnp.max(a_row, axis=1, keepdims=True))
            wrows.append(jnp.exp(a_row - m_new[:, 0:1]))
            solds.append(jnp.exp(g_row + mprev - m_new))
            ml_m[h] = jnp.broadcast_to(m_new, (8, LANES))
        num = _dot(sc_ref[p], ml_vv[p]) + jnp.where(m0, iws[0], iws[1]) * ml_ni[p][:, :LANES]
        hcur = num / jnp.where(m0, dens[0], dens[1])
        y_ref[:, ATT_W + c0:ATT_W + c0 + LANES] = (
            jax.nn.sigmoid(zc[:, MO + c0:MO + c0 + LANES])
            * _pair_rms(hcur, mlg_ref[:, c0:c0 + LANES], m0)
            * _silu(zc[:, MG + c0:MG + c0 + LANES]))
        kw_t = ml_kt[p] * jnp.where(frow, wrows[0], wrows[1])
        upd = _dot(kw_t.astype(BF16), ml_vb[p])
        sold_col = jnp.where(prow_first, solds[0][:, 0:1], solds[1][:, 0:1])
        ml_c[p] = jnp.where(pair_diag, sold_col * ml_ct[p] + upd, 0.0)
        ncol = sold_col * ml_nc[p] + jnp.sum(kw_t, axis=1, keepdims=True)
        ml_n[p] = jnp.broadcast_to(ncol, (LANES, LANES))
        ip(IP_PER_ML_PAIR)

    grow = lax.broadcasted_iota(jnp.int32, (LANES, GLA_VW), 0)
    gcol = lax.broadcasted_iota(jnp.int32, (LANES, GLA_VW), 1)
    gla_diag = (grow >> 5) == (gcol >> 6)
    khead = lane >> 5
    vhead = lax.broadcasted_iota(jnp.int32, (1, GLA_VW), 1) >> 6
    arow = lax.broadcasted_iota(jnp.int32, (CHUNK, GLA_VW), 0)
    acol = lax.broadcasted_iota(jnp.int32, (CHUNK, GLA_VW), 1)
    causal4 = arow >= (acol & (CHUNK - 1))
    n_gh = GLA_KW // GLA_DK
    yc0 = ATT_W + ML_W
    g_a, g_upd, g_qi, g_vvg, g_dec = [], [], [], [], []
    for c in range(NCH):
        r0 = c * CHUNK
        bc = bgl[r0:r0 + CHUNK, :]
        bref = bc[CHUNK // 2 - 1:CHUNK // 2, :]
        btot = bc[CHUNK - 1:CHUNK, :]
        gq = zc[r0:r0 + CHUNK, GQ:GQ + GLA_KW] * (GLA_DK ** -0.5)
        gk = zc[r0:r0 + CHUNK, GK:GK + GLA_KW]
        gv = zc[r0:r0 + CHUNK, GV:GV + GLA_VW]
        qe = (gq * jnp.exp(bc - bref)).astype(BF16)
        ke = gk * jnp.exp(bref - bc)
        kek = jnp.concatenate([jnp.where(khead == h, ke, 0.0) for h in range(n_gh)],
                              axis=0).astype(BF16)
        g_vvg.append(jnp.concatenate([jnp.where(vhead == h, gv, 0.0) for h in range(n_gh)],
                                     axis=0).astype(BF16))
        g_qi.append((gq * jnp.exp(bc)).astype(BF16))
        kd = (gk * jnp.exp(btot - bc)).astype(BF16)
        g_dec.append(jnp.exp(bgl_t[:, r0 + CHUNK - 1:r0 + CHUNK]))
        g_a.append(_dot_nt(qe, kek))
        g_upd.append(_dot_tn(kd, gv.astype(BF16)))
    ip(2)
    s_all = gla_s[...]
    for c in range(NCH):
        r0 = c * CHUNK
        a = jnp.where(causal4, g_a[c], 0.0)
        o = _dot(g_qi[c], s_all.astype(BF16)) + _dot(a.astype(BF16), g_vvg[c])
        y_ref[r0:r0 + CHUNK, yc0:yc0 + GLA_VW] = o
        s_all = jnp.where(gla_diag, g_dec[c] * s_all + g_upd[c], 0.0)
        ip(1)
    gla_s[...] = s_all

    for hp in range(GLA_VW // LANES):
        c0 = hp * LANES
        hcur = y_ref[:, yc0 + c0:yc0 + c0 + LANES]
        y_ref[:, yc0 + c0:yc0 + c0 + LANES] = (
            _pair_rms(hcur, glg_ref[:, c0:c0 + LANES], m0) * _silu(zc[:, GG + c0:GG + c0 + LANES]))
    ip(D_INP // IP_BLK)

    xo = xc_ref[...] + _dot(y_ref[...].astype(BF16), wout_ref[...])
    if final_norm:
        xo = xo * lax.rsqrt(jnp.sum(xo * xo, axis=1, keepdims=True) * (1.0 / D_MODEL) + EPS) * fg_ref[...]
    o_ref[...] = xo


def _project(x, ng_ref, w_cols, z_out):
    hn = x * lax.rsqrt(jnp.sum(x * x, axis=1, keepdims=True) * (1.0 / D_MODEL) + EPS) * ng_ref[...]
    hn_b = hn.astype(BF16)
    for n0 in range(0, D_INP, IP_BLK):
        z_out[:, n0:n0 + IP_BLK] = _dot(hn_b, w_cols(n0))


def _layer_kernel(xa_ref, xn_ref, ng_ref, win_ref, wtl_ref, gb_ref, conv_ref, wal_ref, bal_ref, gp_ref,
                  mlg_ref, glg_ref, wout_ref, fg_ref,
                  o_ref,
                  z_a, z_b, kb, vb, bias_ref, cbuf, ml_c, ml_n, ml_m, gla_s, y_ref, sc_ref, wrm,
                  *, final_norm, tiles_per_seq):
    g = pl.program_id(0)
    i = lax.rem(2 * g, tiles_per_seq)

    def w_rows(n0):
        if n0 < W_MAIN:
            return win_ref[n0:n0 + IP_BLK, :]
        return wtl_ref[n0 - W_MAIN:n0 - W_MAIN + IP_BLK, :]

    def w_cols(n0):
        return wrm[:, n0:n0 + IP_BLK]

    @pl.when(g == 0)
    def _first_step():
        r = lax.broadcasted_iota(jnp.int32, (TILE, KEYS), 0)
        m = lax.broadcasted_iota(jnp.int32, (TILE, KEYS), 1)
        jj = m - ((r >> 6) << 6)
        in_band = (jj >= 0) & (jj < BAND)
        for h in range(ML_HEADS):
            row = jnp.broadcast_to(gp_ref[h:h + 1, :], (TILE, ROLL_W))
            rolled = pltpu.roll(row, KEYS, 1, stride=1, stride_axis=0)
            bias_ref[h] = jnp.where(in_band, rolled[:, :KEYS] * LOG2E, NEG)
        for n0 in range(0, D_INP, IP_BLK):
            wrm[:, n0:n0 + IP_BLK] = w_rows(n0).astype(F32).T.astype(BF16)
        _project(xa_ref[0:TILE, :], ng_ref, w_cols, z_a)

    @pl.when(i == 0)
    def _reset():
        kr = lax.broadcasted_iota(jnp.int32, (2 * ATT_W, 1), 0)
        pen_row = (kr & (LANES - 1)) == jnp.where(kr < ATT_W, PEN_LANE_A, PEN_LANE_B)
        pen_init = jnp.broadcast_to(jnp.where(pen_row, NEG, 0.0), kb.shape[1:]).astype(kb.dtype)
        for slot in range(2):
            kb[slot] = pen_init
            vb[slot] = jnp.zeros(vb.shape[1:], vb.dtype)
        cbuf[0:8, :] = jnp.zeros((8, cbuf.shape[1]), F32)
        ml_c[...] = jnp.zeros(ml_c.shape, F32)
        ml_n[...] = jnp.zeros(ml_n.shape, F32)
        ml_m[...] = jnp.zeros(ml_m.shape, F32)
        gla_s[...] = jnp.zeros(gla_s.shape, F32)

    step = functools.partial(
        _step, ng_ref=ng_ref, w_cols=w_cols, gb_ref=gb_ref,
        conv_ref=conv_ref, wal_ref=wal_ref, bal_ref=bal_ref, mlg_ref=mlg_ref, glg_ref=glg_ref,
        wout_ref=wout_ref, fg_ref=fg_ref, bias_ref=bias_ref,
        cbuf=cbuf, ml_c=ml_c, ml_n=ml_n, ml_m=ml_m, gla_s=gla_s, y_ref=y_ref, sc_ref=sc_ref,
        final_norm=final_norm)

    lo = pl.ds(0, TILE)
    hi = pl.ds(TILE, TILE)
    step(i, z_a, z_b, xn_ref=xa_ref.at[hi], xc_ref=xa_ref.at[lo], o_ref=o_ref.at[lo],
         kbuf=kb.at[0], vbuf=vb.at[0], kold=kb.at[1], vold=vb.at[1])
    step(i + 1, z_b, z_a, xn_ref=xn_ref, xc_ref=xa_ref.at[hi], o_ref=o_ref.at[hi],
         kbuf=kb.at[1], vbuf=vb.at[1], kold=kb.at[0], vold=vb.at[0])


def _layer_spec(layer, arr, rows=None):
    shape = arr.shape[1:] if rows is None else (rows,) + arr.shape[2:]
    nd = len(shape)
    return pl.BlockSpec((None,) + shape, lambda g, _l=layer, _nd=nd: (_l,) + (0,) * _nd,
                        pipeline_mode=pl.Buffered(1))


def _layer_call(x2d, layer, prm, final_norm, tiles_per_seq):
    n_tiles = x2d.shape[0] // TILE
    n_steps = n_tiles // 2
    names = ("ng", "win", "wtl", "gb", "conv", "wal", "bal", "gp", "mlg", "glg", "wout")
    args = (x2d, x2d) + tuple(prm[k] for k in names) + (prm["fg"],)
    cur_spec = pl.BlockSpec((2 * TILE, D_MODEL), lambda g: (g, 0))
    nxt_spec = pl.BlockSpec((TILE, D_MODEL), lambda g: (jnp.minimum(2 * g + 2, n_tiles - 1), 0))
    in_specs = [cur_spec, nxt_spec]
    in_specs += [_layer_spec(layer, prm[k], rows=W_MAIN if k == "win" else None) for k in names]
    in_specs += [pl.BlockSpec(prm["fg"].shape, lambda g: (0, 0), pipeline_mode=pl.Buffered(1))]
    scratch = [
        pltpu.VMEM((TILE, D_INP), F32),
        pltpu.VMEM((TILE, D_INP), F32),
        pltpu.VMEM((2, 2 * ATT_W, KEYS), BF16),
        pltpu.VMEM((2, KEYS, 2 * ATT_W), BF16),
        pltpu.VMEM((ML_HEADS, TILE, KEYS), F32),
        pltpu.VMEM((TILE + 16, 2 * ML_W), F32),
        pltpu.VMEM((N_PAIRS, LANES, LANES), F32),
        pltpu.VMEM((N_PAIRS, LANES, LANES), F32),
        pltpu.VMEM((ML_HEADS, 8, LANES), F32),
        pltpu.VMEM((LANES, GLA_VW), F32),
        pltpu.VMEM((TILE, D_MODEL), F32),
        pltpu.VMEM((N_PAIRS, TILE, 2 * TILE), BF16),
        pltpu.VMEM((D_MODEL, D_INP), BF16),
    ]
    return pl.pallas_call(
        functools.partial(_layer_kernel, final_norm=final_norm, tiles_per_seq=tiles_per_seq),
        out_shape=jax.ShapeDtypeStruct(x2d.shape, x2d.dtype),
        grid=(n_steps,),
        in_specs=in_specs,
        out_specs=cur_spec,
        scratch_shapes=scratch,
        compiler_params=pltpu.CompilerParams(
            dimension_semantics=("arbitrary",),
            vmem_limit_bytes=VMEM_LIMIT),
        name="hybrid_layer_final" if final_norm else "hybrid_layer",
    )(*args)


def _prep_params(norm_g, w_in, b_gates, conv_w, w_alpha, b_alpha, rel_bias, ml_norm_g, gla_norm_g, w_out, final_g):
    depth = norm_g.shape[0]
    w_t = jnp.transpose(w_in, (0, 2, 1)).astype(BF16)
    pad = jnp.zeros((depth, D_INP - SM - GLA_RANK - 2 * ML_HEADS, D_MODEL), BF16)
    wtl = jnp.concatenate([w_t[:, 3084:3980], w_t[:, 3996:4252], w_t[:, 3980:3996],
                           w_t[:, 3072:3084], pad], axis=1)
    gb = jnp.pad(b_gates, ((0, 0), (SM_I, LANES - SM_I - 2 * ML_HEADS))).reshape(depth, 1, LANES)
    wal = jnp.pad(w_alpha, ((0, 0), (0, LANES - GLA_RANK), (0, 0))).astype(BF16)
    nh = rel_bias.shape[1]
    gp = jnp.concatenate([
        jnp.broadcast_to(rel_bias[:, :, 2 * REL_CLIP:], (depth, nh, KEYS - REL_CLIP + 1)),
        rel_bias[:, :, 2 * REL_CLIP - 1:0:-1],
        jnp.broadcast_to(rel_bias[:, :, :1], (depth, nh, ROLL_W - KEYS - REL_CLIP)),
    ], axis=2)
    return dict(ng=norm_g.reshape(depth, 1, D_MODEL), win=w_t, wtl=wtl, gb=gb, conv=conv_w, wal=wal,
                bal=b_alpha.reshape(depth, 1, GLA_KW), gp=gp, mlg=ml_norm_g.reshape(depth, 1, ML_W),
                glg=gla_norm_g.reshape(depth, 1, GLA_VW), wout=w_out.astype(BF16),
                fg=final_g.reshape(1, D_MODEL))


def kernel(x, norm_g, w_in, b_gates, conv_w, w_alpha, b_alpha, rel_bias, ml_norm_g, gla_norm_g, w_out, final_g):
    depth = norm_g.shape[0]
    bsz, seq, _ = x.shape
    prm = _prep_params(norm_g, w_in, b_gates, conv_w, w_alpha, b_alpha, rel_bias, ml_norm_g, gla_norm_g,
                       w_out, final_g)
    x2d = x.reshape(bsz * seq, D_MODEL)
    for l in range(depth):
        x2d = _layer_call(x2d, l, prm, final_norm=(l == depth - 1), tiles_per_seq=seq // TILE)
    return x2d.reshape(bsz, seq, D_MODEL)
```

```python
import functools

import jax
import jax.numpy as jnp
from jax import lax
from jax.experimental import pallas as pl
from jax.experimental.pallas import tpu as pltpu

F32 = jnp.float32
BF16 = jnp.bfloat16

D_MODEL = 1024
CHUNK = 64
HEAD_DIM = 64
ATT_W = 384
ML_W = 384
GLA_KW = 128
GLA_VW = 256
GLA_DK = 32
GLA_RANK = 16
GLA_TAU = 16.0
ML_HEADS = 6
PAST_CHUNKS = 8
REL_CLIP = 128
CONV_W = 4
EPS = 1e-6
NEG = -1e30
LOG2E = 1.4426950408889634

LANES = 128
TILE = 256
NCH = TILE // CHUNK
HIST = PAST_CHUNKS * CHUNK
KEYS = HIST + TILE
HALF_ROWS = TILE // 2
HALF_KEYS = HIST + HALF_ROWS
PEN_LANE_A = HEAD_DIM
PEN_LANE_B = 0
BAND = (PAST_CHUNKS + 1) * CHUNK
ROLL_W = 1024
N_PAIRS = ML_W // LANES

AQ, AK, AV, AG = 0, 384, 768, 1152
MQ, MK, MV, MO, MG = 1536, 1920, 2304, 2688, 3072
GQ, GK, GV, GG = 3456, 3584, 3712, 3968
SM = 4224
D_INP = 4352
IP_BLK = 256
W_MAIN = 3072
IP_PER_HEAD, IP_AFTER_ATT, IP_PER_ML_PAIR = 1, 1, 1
SM_I = GLA_RANK
SM_F = GLA_RANK + ML_HEADS

VMEM_LIMIT = 56 * 1024 * 1024


def _log_sigmoid(x):
    return jnp.minimum(x, 0.0) - jnp.log1p(jnp.exp(-jnp.abs(x)))


def _silu(x):
    return x * jax.nn.sigmoid(x)


def _dot(a, b):
    return jnp.dot(a, b, preferred_element_type=F32)


def _pair_rms(hcur, gain, m0):
    sq = hcur * hcur
    ms0 = jnp.sum(jnp.where(m0, sq, 0.0), axis=1, keepdims=True)
    ms1 = jnp.sum(jnp.where(m0, 0.0, sq), axis=1, keepdims=True)
    ms = jnp.where(m0, ms0, ms1) * (1.0 / HEAD_DIM)
    return hcur * lax.rsqrt(ms + EPS) * gain


def _step(i, zc, zn, xn_ref, xc_ref, ng_ref, w_cols, gb_ref, conv_ref, wal_ref, bal_ref,
          mlg_ref, glg_ref, wout_ref, fg_ref, o_ref,
          kbuf, vbuf, kold, vold, bias_ref, cbuf, ml_c, ml_n, ml_m, gla_s, y_ref, sc_ref, *, final_norm):
    lane = lax.broadcasted_iota(jnp.int32, (1, LANES), 1)
    m0 = lane < HEAD_DIM
    m1 = jnp.logical_not(m0)

    xn = xn_ref[...]
    hn = xn * lax.rsqrt(jnp.sum(xn * xn, axis=1, keepdims=True) * (1.0 / D_MODEL) + EPS) * ng_ref[...]
    hn_b = hn.astype(BF16)
    ip_next = [0]

    def ip(count):
        for _ in range(count):
            n0 = ip_next[0] * IP_BLK
            if n0 < D_INP:
                zn[:, n0:n0 + IP_BLK] = _dot(hn_b, w_cols(n0))
                ip_next[0] += 1

    kbuf[:, 0:HIST] = kold[:, TILE:KEYS]
    vbuf[0:HIST, :] = vold[TILE:KEYS, :]
    frow = lax.broadcasted_iota(jnp.int32, (LANES, 1), 0) < HEAD_DIM
    for p in range(N_PAIRS):
        c0 = p * LANES
        kn_t = zc[:, AK + c0:AK + c0 + LANES].T
        vn = zc[:, AV + c0:AV + c0 + LANES]
        kbuf[c0:c0 + LANES, HIST:KEYS] = jnp.where(frow, kn_t, 0.0).astype(BF16)
        kbuf[ATT_W + c0:ATT_W + c0 + LANES, HIST:KEYS] = jnp.where(frow, 0.0, kn_t).astype(BF16)
        vbuf[HIST:KEYS, c0:c0 + LANES] = jnp.where(m0, vn, 1.0).astype(BF16)
        vbuf[HIST:KEYS, ATT_W + c0:ATT_W + c0 + LANES] = jnp.where(m0, 1.0, vn).astype(BF16)

    def att_scores(u):
        h, half = u // 2, u % 2
        c0 = (h // 2) * LANES
        cb = (h % 2) * ATT_W + c0
        r0, k0 = half * HALF_ROWS, half * HALF_ROWS
        q2 = zc[r0:r0 + HALF_ROWS, AQ + c0:AQ + c0 + LANES] * (HEAD_DIM ** -0.5 * LOG2E)
        if h % 2 == 0:
            qm = jnp.where(m0, q2, jnp.where(lane == PEN_LANE_A, 1.0, 0.0))
        else:
            qm = jnp.where(m1, q2, jnp.where(lane == PEN_LANE_B, 1.0, 0.0))
        return (_dot(qm.astype(BF16), kbuf[cb:cb + LANES, k0:k0 + HALF_KEYS])
                + bias_ref[h, r0:r0 + HALF_ROWS, k0:k0 + HALF_KEYS])

    def att_out(u, s):
        h, half = u // 2, u % 2
        cb = (h % 2) * ATT_W + (h // 2) * LANES
        k0 = half * HALF_ROWS
        pe = jnp.exp2(s - jnp.max(s, axis=1, keepdims=True))
        return _dot(pe.astype(BF16), vbuf[k0:k0 + HALF_KEYS, cb:cb + LANES])

    n_units = 2 * ML_HEADS
    s_cur = att_scores(0)
    outs = []
    for u in range(n_units):
        s_nxt = att_scores(u + 1) if u + 1 < n_units else None
        if u % 2 == 1:
            ip(IP_PER_HEAD)
        outs.append(att_out(u, s_cur))
        s_cur = s_nxt
        if u % 4 == 3:
            c0 = (u // 4) * LANES
            r_a = jnp.concatenate(outs[u - 3:u - 1], axis=0)
            r_b = jnp.concatenate(outs[u - 1:u + 1], axis=0)
            den = pltpu.roll(jnp.where(m0, r_b, r_a), HEAD_DIM, 1)
            att = jnp.where(m0, r_a, r_b) / den
            y_ref[:, c0:c0 + LANES] = att * _silu(zc[:, AG + c0:AG + c0 + LANES])
    ip(IP_AFTER_ATT)

    cbuf[8:8 + TILE, :] = zc[:, MQ:MQ + 2 * ML_W]
    for c0 in range(0, 2 * ML_W, LANES):
        acc = cbuf[8:8 + TILE, c0:c0 + LANES] * conv_ref[CONV_W - 1:CONV_W, c0:c0 + LANES]
        for sft in range(1, CONV_W):
            acc = acc + cbuf[8 - sft:8 - sft + TILE, c0:c0 + LANES] * \
                conv_ref[CONV_W - 1 - sft:CONV_W - sft, c0:c0 + LANES]
        act = _silu(acc)
        if c0 >= ML_W:
            act = act * (HEAD_DIM ** -0.5)
        zc[:, MQ + c0:MQ + c0 + LANES] = act
    cbuf[0:8, :] = cbuf[TILE:TILE + 8, :]

    small = zc[:, SM:SM + LANES]
    pre = small + gb_ref[...]
    lf = _log_sigmoid(pre)
    la = _log_sigmoid(_dot(small.astype(BF16), wal_ref[...]) + bal_ref[...]) * (1.0 / GLA_TAU)
    tr = lax.broadcasted_iota(jnp.int32, (TILE, TILE), 0)
    tc = lax.broadcasted_iota(jnp.int32, (TILE, TILE), 1)
    tri = jnp.where(tr >= tc, 1.0, 0.0).astype(BF16)
    terms = []
    for v in (lf, la):
        hi = v.astype(BF16)
        r1 = v - hi.astype(F32)
        mid = r1.astype(BF16)
        terms += [hi, mid, (r1 - mid.astype(F32)).astype(BF16)]
    cs = _dot(tri, jnp.concatenate(terms, axis=1))
    bml = (cs[:, 2 * LANES:3 * LANES] + cs[:, LANES:2 * LANES]) + cs[:, 0:LANES]
    bla = (cs[:, 5 * LANES:6 * LANES] + cs[:, 4 * LANES:5 * LANES]) + cs[:, 3 * LANES:4 * LANES]
    bgl = jnp.concatenate(
        [bla[0:CHUNK]] + [bla[c * CHUNK:(c + 1) * CHUNK] - bla[c * CHUNK - 1:c * CHUNK]
                          for c in range(1, NCH)], axis=0)
    pre_t = pre.T
    bml_t = bml.T
    bgl_t = bgl.T
    ip(1)

    prow = lax.broadcasted_iota(jnp.int32, (LANES, LANES), 0)
    pcol = lax.broadcasted_iota(jnp.int32, (LANES, LANES), 1)
    pair_diag = (prow < HEAD_DIM) == (pcol < HEAD_DIM)
    prow_first = prow[:, 0:1] < HEAD_DIM

    ml_kt, ml_vb, ml_vv, ml_s, ml_ni, ml_ct, ml_nc = [], [], [], [], [], [], []
    for p in range(N_PAIRS):
        c0 = p * LANES
        q2 = zc[:, MQ + c0:MQ + c0 + LANES]
        k2 = zc[:, MK + c0:MK + c0 + LANES]
        v2 = zc[:, MV + c0:MV + c0 + LANES]
        q2b = q2.astype(BF16)
        k2_t = k2.T
        kk_t = jnp.concatenate([jnp.where(frow, k2_t, 0.0), jnp.where(frow, 0.0, k2_t)],
                               axis=1).astype(BF16)
        ct = ml_c[p]
        ncol = ml_n[p][:, 0:1]
        n_ext = jnp.where(lane == 0, jnp.where(frow, ncol, 0.0),
                          jnp.where(lane == 1, jnp.where(frow, 0.0, ncol), 0.0))
        ml_kt.append(k2_t)
        ml_nc.append(ncol)
        ml_vb.append(v2.astype(BF16))
        ml_vv.append(jnp.concatenate([jnp.where(m0, v2, 0.0), jnp.where(m0, 0.0, v2)],
                                     axis=0).astype(BF16))
        ml_ct.append(ct)
        ml_s.append(_dot(q2b, kk_t))
        ml_ni.append(_dot(q2b, jnp.concatenate([ct, n_ext], axis=1).astype(BF16)))
    ip(2)

    for p in range(N_PAIRS):
        c0 = p * LANES
        s_pair = ml_s[p]
        iws, dens, wrows, solds = [], [], [], []
        for e in range(2):
            h = 2 * p + e
            mprev = ml_m[h][0:1, :]
            bt_all = jnp.broadcast_to(bml[:, SM_F + h:SM_F + h + 1], (TILE, LANES))
            bj = bml_t[SM_F + h:SM_F + h + 1, :]
            ij = pre_t[SM_I + h:SM_I + h + 1, :]
            den_inter = ml_ni[p][:, LANES + e:LANES + e + 1]
            iw_blocks, den_blocks = [], []
            for rb in range(NCH):
                r0 = rb * CHUNK
                ncol = LANES * ((r0 + CHUNK + LANES - 1) // LANES)
                rr = lax.broadcasted_iota(jnp.int32, (CHUNK, ncol), 0) + r0
                kc = lax.broadcasted_iota(jnp.int32, (CHUNK, ncol), 1)
                btc = bt_all[r0:r0 + CHUNK, 0:1]
                log_d = jnp.where(rr >= kc, btc - bj[:, :ncol] + ij[:, :ncol], NEG)
                inter_log = btc + mprev[:, 0:1]
                m_row = jnp.maximum(inter_log, jnp.max(log_d, axis=1, keepdims=True))
                sc = s_pair[r0:r0 + CHUNK, e * TILE:e * TILE + ncol] * jnp.exp(log_d - m_row)
                sc_ref[p, r0:r0 + CHUNK, e * TILE:e * TILE + ncol] = sc.astype(BF16)
                if ncol < TILE:
                    sc_ref[p, r0:r0 + CHUNK, e * TILE + ncol:(e + 1) * TILE] = jnp.zeros(
                        (CHUNK, TILE - ncol), BF16)
                inter_w = jnp.exp(inter_log - m_row)
                den = jnp.sum(sc, axis=1, keepdims=True) + inter_w * den_inter[r0:r0 + CHUNK]
                den_blocks.append(jnp.maximum(jnp.abs(den), jnp.exp(-m_row)))
                iw_blocks.append(inter_w)
            iws.append(jnp.concatenate(iw_blocks, axis=0))
            dens.append(jnp.concatenate(den_blocks, axis=0))
            g_row = bt_all[TILE - 1:TILE, :]
            a_row = g_row[:, 0:1] - bj + ij
            m_new = jnp.maximum(g_row + mprev, jnp.max(a_row, axis=1, keepdims=True))
            wrows.append(jnp.exp(a_row - m_new[:, 0:1]))
            solds.append(jnp.exp(g_row + mprev - m_new))
            ml_m[h] = jnp.broadcast_to(m_new, (8, LANES))
        num = _dot(sc_ref[p], ml_vv[p]) + jnp.where(m0, iws[0], iws[1]) * ml_ni[p][:, :LANES]
        hcur = num / jnp.where(m0, dens[0], dens[1])
        y_ref[:, ATT_W + c0:ATT_W + c0 + LANES] = (
            jax.nn.sigmoid(zc[:, MO + c0:MO + c0 + LANES])
            * _pair_rms(hcur, mlg_ref[:, c0:c0 + LANES], m0)
            * _silu(zc[:, MG + c0:MG + c0 + LANES]))
        kw_t = ml_kt[p] * jnp.where(frow, wrows[0], wrows[1])
        upd = _dot(kw_t.astype(BF16), ml_vb[p])
        sold_col = jnp.where(prow_first, solds[0][:, 0:1], solds[1][:, 0:1])
        ml_c[p] = jnp.where(pair_diag, sold_col * ml_ct[p] + upd, 0.0)
        ncol = sold_col * ml_nc[p] + jnp.sum(kw_t, axis=1, keepdims=True)
        ml_n[p] = jnp.broadcast_to(ncol, (LANES, LANES))
        ip(IP_PER_ML_PAIR)

    grow = lax.broadcasted_iota(jnp.int32, (LANES, GLA_VW), 0)
    gcol = lax.broadcasted_iota(jnp.int32, (LANES, GLA_VW), 1)
    gla_diag = (grow >> 5) == (gcol >> 6)
    vhead = lax.broadcasted_iota(jnp.int32, (1, GLA_VW), 1) >> 6
    arow = lax.broadcasted_iota(jnp.int32, (CHUNK, GLA_VW), 0)
    acol = lax.broadcasted_iota(jnp.int32, (CHUNK, GLA_VW), 1)
    causal4 = arow >= (acol & (CHUNK - 1))
    n_gh = GLA_KW // GLA_DK
    yc0 = ATT_W + ML_W
    g_a, g_upd, g_qi, g_vvg, g_dec, g_qe, ke_rows, kd_rows = [], [], [], [], [], [], [], []
    for c in range(NCH):
        r0 = c * CHUNK
        bc = bgl[r0:r0 + CHUNK, :]
        bref = bc[CHUNK // 2 - 1:CHUNK // 2, :]
        btot = bc[CHUNK - 1:CHUNK, :]
        gq = zc[r0:r0 + CHUNK, GQ:GQ + GLA_KW] * (GLA_DK ** -0.5)
        gk = zc[r0:r0 + CHUNK, GK:GK + GLA_KW]
        gv = zc[r0:r0 + CHUNK, GV:GV + GLA_VW]
        g_qe.append((gq * jnp.exp(bc - bref)).astype(BF16))
        ke_rows.append(gk * jnp.exp(bref - bc))
        kd_rows.append(gk * jnp.exp(btot - bc))
        g_vvg.append(jnp.concatenate([jnp.where(vhead == h, gv, 0.0) for h in range(n_gh)],
                                     axis=0).astype(BF16))
        g_qi.append((gq * jnp.exp(bc)).astype(BF16))
        g_dec.append(jnp.exp(bgl_t[:, r0 + CHUNK - 1:r0 + CHUNK]))
    ke_t = jnp.concatenate(ke_rows, axis=0).T
    kd_t = jnp.concatenate(kd_rows, axis=0).T
    gvb_all = zc[:, GV:GV + GLA_VW].astype(BF16)
    tchunk = lax.broadcasted_iota(jnp.int32, (LANES, TILE), 1) >> 6
    frow_head = prow >> 5
    lane_hi = pcol >= CHUNK
    head_lo = jnp.where(lane_hi, 1, 0)
    for c in range(NCH):
        v = ke_t[:, (c // 2) * LANES:(c // 2 + 1) * LANES]
        rolled = pltpu.roll(v, CHUNK, 1)
        x = jnp.where(lane_hi, rolled, v) if c % 2 == 0 else jnp.where(lane_hi, v, rolled)
        kek_t = jnp.concatenate([jnp.where(frow_head == head_lo, x, 0.0),
                                 jnp.where(frow_head == head_lo + 2, x, 0.0)],
                                axis=1).astype(BF16)
        g_a.append(_dot(g_qe[c], kek_t))
        kd_c = jnp.where(tchunk == c, kd_t, 0.0).astype(BF16)
        g_upd.append(_dot(kd_c, gvb_all))
    ip(2)
    s_all = gla_s[...]
    for c in range(NCH):
        r0 = c * CHUNK
        a = jnp.where(causal4, g_a[c], 0.0)
        o = _dot(g_qi[c], s_all.astype(BF16)) + _dot(a.astype(BF16), g_vvg[c])
        y_ref[r0:r0 + CHUNK, yc0:yc0 + GLA_VW] = o
        s_all = jnp.where(gla_diag, g_dec[c] * s_all + g_upd[c], 0.0)
        ip(1)
    gla_s[...] = s_all

    for hp in range(GLA_VW // LANES):
        c0 = hp * LANES
        hcur = y_ref[:, yc0 + c0:yc0 + c0 + LANES]
        y_ref[:, yc0 + c0:yc0 + c0 + LANES] = (
            _pair_rms(hcur, glg_ref[:, c0:c0 + LANES], m0) * _silu(zc[:, GG + c0:GG + c0 + LANES]))
    ip(D_INP // IP_BLK)

    xo = xc_ref[...] + _dot(y_ref[...].astype(BF16), wout_ref[...])
    if final_norm:
        xo = xo * lax.rsqrt(jnp.sum(xo * xo, axis=1, keepdims=True) * (1.0 / D_MODEL) + EPS) * fg_ref[...]
    o_ref[...] = xo


def _project(x, ng_ref, w_cols, z_out):
    hn = x * lax.rsqrt(jnp.sum(x * x, axis=1, keepdims=True) * (1.0 / D_MODEL) + EPS) * ng_ref[...]
    hn_b = hn.astype(BF16)
    for n0 in range(0, D_INP, IP_BLK):
        z_out[:, n0:n0 + IP_BLK] = _dot(hn_b, w_cols(n0))


def _layer_kernel(xa_ref, xn_ref, ng_ref, win_ref, wtl_ref, gb_ref, conv_ref, wal_ref, bal_ref, gp_ref,
                  mlg_ref, glg_ref, wout_ref, fg_ref,
                  o_ref,
                  z_a, z_b, kb, vb, bias_ref, cbuf, ml_c, ml_n, ml_m, gla_s, y_ref, sc_ref, wrm,
                  *, final_norm, tiles_per_seq):
    g = pl.program_id(0)
    i = lax.rem(2 * g, tiles_per_seq)

    def w_rows(n0):
        if n0 < W_MAIN:
            return win_ref[n0:n0 + IP_BLK, :]
        return wtl_ref[n0 - W_MAIN:n0 - W_MAIN + IP_BLK, :]

    def w_cols(n0):
        return wrm[:, n0:n0 + IP_BLK]

    @pl.when(g == 0)
    def _first_step():
        r = lax.broadcasted_iota(jnp.int32, (TILE, KEYS), 0)
        m = lax.broadcasted_iota(jnp.int32, (TILE, KEYS), 1)
        jj = m - ((r >> 6) << 6)
        in_band = (jj >= 0) & (jj < BAND)
        for h in range(ML_HEADS):
            row = jnp.broadcast_to(gp_ref[h:h + 1, :], (TILE, ROLL_W))
            rolled = pltpu.roll(row, KEYS, 1, stride=1, stride_axis=0)
            bias_ref[h] = jnp.where(in_band, rolled[:, :KEYS] * LOG2E, NEG)
        for n0 in range(0, D_INP, IP_BLK):
            wrm[:, n0:n0 + IP_BLK] = w_rows(n0).astype(F32).T.astype(BF16)
        _project(xa_ref[0:TILE, :], ng_ref, w_cols, z_a)

    @pl.when(i == 0)
    def _reset():
        kr = lax.broadcasted_iota(jnp.int32, (2 * ATT_W, 1), 0)
        pen_row = (kr & (LANES - 1)) == jnp.where(kr < ATT_W, PEN_LANE_A, PEN_LANE_B)
        pen_init = jnp.broadcast_to(jnp.where(pen_row, NEG, 0.0), kb.shape[1:]).astype(kb.dtype)
        for slot in range(2):
            kb[slot] = pen_init
            vb[slot] = jnp.zeros(vb.shape[1:], vb.dtype)
        cbuf[0:8, :] = jnp.zeros((8, cbuf.shape[1]), F32)
        ml_c[...] = jnp.zeros(ml_c.shape, F32)
        ml_n[...] = jnp.zeros(ml_n.shape, F32)
        ml_m[...] = jnp.zeros(ml_m.shape, F32)
        gla_s[...] = jnp.zeros(gla_s.shape, F32)

    step = functools.partial(
        _step, ng_ref=ng_ref, w_cols=w_cols, gb_ref=gb_ref,
        conv_ref=conv_ref, wal_ref=wal_ref, bal_ref=bal_ref, mlg_ref=mlg_ref, glg_ref=glg_ref,
        wout_ref=wout_ref, fg_ref=fg_ref, bias_ref=bias_ref,
        cbuf=cbuf, ml_c=ml_c, ml_n=ml_n, ml_m=ml_m, gla_s=gla_s, y_ref=y_ref, sc_ref=sc_ref,
        final_norm=final_norm)

    lo = pl.ds(0, TILE)
    hi = pl.ds(TILE, TILE)
    step(i, z_a, z_b, xn_ref=xa_ref.at[hi], xc_ref=xa_ref.at[lo], o_ref=o_ref.at[lo],
         kbuf=kb.at[0], vbuf=vb.at[0], kold=kb.at[1], vold=vb.at[1])
    step(i + 1, z_b, z_a, xn_ref=xn_ref, xc_ref=xa_ref.at[hi], o_ref=o_ref.at[hi],
         kbuf=kb.at[1], vbuf=vb.at[1], kold=kb.at[0], vold=vb.at[0])


def _layer_spec(layer, arr, rows=None):
    shape = arr.shape[1:] if rows is None else (rows,) + arr.shape[2:]
    nd = len(shape)
    return pl.BlockSpec((None,) + shape, lambda g, _l=layer, _nd=nd: (_l,) + (0,) * _nd,
                        pipeline_mode=pl.Buffered(1))


def _layer_call(x2d, layer, prm, final_norm, tiles_per_seq):
    n_tiles = x2d.shape[0] // TILE
    n_steps = n_tiles // 2
    names = ("ng", "win", "wtl", "gb", "conv", "wal", "bal", "gp", "mlg", "glg", "wout")
    args = (x2d, x2d) + tuple(prm[k] for k in names) + (prm["fg"],)
    cur_spec = pl.BlockSpec((2 * TILE, D_MODEL), lambda g: (g, 0))
    nxt_spec = pl.BlockSpec((TILE, D_MODEL), lambda g: (jnp.minimum(2 * g + 2, n_tiles - 1), 0))
    in_specs = [cur_spec, nxt_spec]
    in_specs += [_layer_spec(layer, prm[k], rows=W_MAIN if k == "win" else None) for k in names]
    in_specs += [pl.BlockSpec(prm["fg"].shape, lambda g: (0, 0), pipeline_mode=pl.Buffered(1))]
    scratch = [
        pltpu.VMEM((TILE, D_INP), F32),
        pltpu.VMEM((TILE, D_INP), F32),
        pltpu.VMEM((2, 2 * ATT_W, KEYS), BF16),
        pltpu.VMEM((2, KEYS, 2 * ATT_W), BF16),
        pltpu.VMEM((ML_HEADS, TILE, KEYS), F32),
        pltpu.VMEM((TILE + 16, 2 * ML_W), F32),
        pltpu.VMEM((N_PAIRS, LANES, LANES), F32),
        pltpu.VMEM((N_PAIRS, LANES, LANES), F32),
        pltpu.VMEM((ML_HEADS, 8, LANES), F32),
        pltpu.VMEM((LANES, GLA_VW), F32),
        pltpu.VMEM((TILE, D_MODEL), F32),
        pltpu.VMEM((N_PAIRS, TILE, 2 * TILE), BF16),
        pltpu.VMEM((D_MODEL, D_INP), BF16),
    ]
    return pl.pallas_call(
        functools.partial(_layer_kernel, final_norm=final_norm, tiles_per_seq=tiles_per_seq),
        out_shape=jax.ShapeDtypeStruct(x2d.shape, x2d.dtype),
        grid=(n_steps,),
        in_specs=in_specs,
        out_specs=cur_spec,
        scratch_shapes=scratch,
        compiler_params=pltpu.CompilerParams(
            dimension_semantics=("arbitrary",),
            vmem_limit_bytes=VMEM_LIMIT),
        name="hybrid_layer_final" if final_norm else "hybrid_layer",
    )(*args)


def _prep_params(norm_g, w_in, b_gates, conv_w, w_alpha, b_alpha, rel_bias, ml_norm_g, gla_norm_g, w_out, final_g):
    depth = norm_g.shape[0]
    w_t = jnp.transpose(w_in, (0, 2, 1)).astype(BF16)
    pad = jnp.zeros((depth, D_INP - SM - GLA_RANK - 2 * ML_HEADS, D_MODEL), BF16)
    wtl = jnp.concatenate([w_t[:, 3084:3980], w_t[:, 3996:4252], w_t[:, 3980:3996],
                           w_t[:, 3072:3084], pad], axis=1)
    gb = jnp.pad(b_gates, ((0, 0), (SM_I, LANES - SM_I - 2 * ML_HEADS))).reshape(depth, 1, LANES)
    wal = jnp.pad(w_alpha, ((0, 0), (0, LANES - GLA_RANK), (0, 0))).astype(BF16)
    nh = rel_bias.shape[1]
    gp = jnp.concatenate([
        jnp.broadcast_to(rel_bias[:, :, 2 * REL_CLIP:], (depth, nh, KEYS - REL_CLIP + 1)),
        rel_bias[:, :, 2 * REL_CLIP - 1:0:-1],
        jnp.broadcast_to(rel_bias[:, :, :1], (depth, nh, ROLL_W - KEYS - REL_CLIP)),
    ], axis=2)
    return dict(ng=norm_g.reshape(depth, 1, D_MODEL), win=w_t, wtl=wtl, gb=gb, conv=conv_w, wal=wal,
                bal=b_alpha.reshape(depth, 1, GLA_KW), gp=gp, mlg=ml_norm_g.reshape(depth, 1, ML_W),
                glg=gla_norm_g.reshape(depth, 1, GLA_VW), wout=w_out.astype(BF16),
                fg=final_g.reshape(1, D_MODEL))


def kernel(x, norm_g, w_in, b_gates, conv_w, w_alpha, b_alpha, rel_bias, ml_norm_g, gla_norm_g, w_out, final_g):
    depth = norm_g.shape[0]
    bsz, seq, _ = x.shape
    prm = _prep_params(norm_g, w_in, b_gates, conv_w, w_alpha, b_alpha, rel_bias, ml_norm_g, gla_norm_g,
                       w_out, final_g)
    x2d = x.reshape(bsz * seq, D_MODEL)
    for l in range(depth):
        x2d = _layer_call(x2d, l, prm, final_norm=(l == depth - 1), tiles_per_seq=seq // TILE)
    return x2d.reshape(bsz, seq, D_MODEL)
```

```python
import functools

import jax
import jax.numpy as jnp
from jax import lax
from jax.experimental import pallas as pl
from jax.experimental.pallas import tpu as pltpu

F32 = jnp.float32
BF16 = jnp.bfloat16

D_MODEL = 1024
CHUNK = 64
HEAD_DIM = 64
ATT_W = 384
ML_W = 384
GLA_KW = 128
GLA_VW = 256
GLA_DK = 32
GLA_RANK = 16
GLA_TAU = 16.0
ML_HEADS = 6
PAST_CHUNKS = 8
REL_CLIP = 128
CONV_W = 4
EPS = 1e-6
NEG = -1e30
LOG2E = 1.4426950408889634

LANES = 128
TILE = 256
NCH = TILE // CHUNK
HIST = PAST_CHUNKS * CHUNK
KEYS = HIST + TILE
HALF_ROWS = TILE // 2
HALF_KEYS = HIST + HALF_ROWS
PEN_LANE_A = HEAD_DIM
PEN_LANE_B = 0
BAND = (PAST_CHUNKS + 1) * CHUNK
ROLL_W = 1024
N_PAIRS = ML_W // LANES

AQ, AK, AV, AG = 0, 384, 768, 1152
MQ, MK, MV, MO, MG = 1536, 1920, 2304, 2688, 3072
GQ, GK, GV, GG = 3456, 3584, 3712, 3968
SM = 4224
D_INP = 4352
IP_BLK = 256
W_MAIN = 3072
IP_PER_HEAD, IP_AFTER_ATT, IP_PER_ML_PAIR = 1, 1, 1
SM_I = GLA_RANK
SM_F = GLA_RANK + ML_HEADS

VMEM_LIMIT = 56 * 1024 * 1024


def _log_sigmoid(x):
    return jnp.minimum(x, 0.0) - jnp.log1p(jnp.exp(-jnp.abs(x)))


def _silu(x):
    return x * jax.nn.sigmoid(x)


def _dot_nt(a, b):
    return lax.dot_general(a, b, (((1,), (1,)), ((), ())), preferred_element_type=F32)


def _dot_tn(a, b):
    return lax.dot_general(a, b, (((0,), (0,)), ((), ())), preferred_element_type=F32)


def _dot(a, b):
    return jnp.dot(a, b, preferred_element_type=F32)


def _pair_rms(hcur, gain, m0):
    sq = hcur * hcur
    ms0 = jnp.sum(jnp.where(m0, sq, 0.0), axis=1, keepdims=True)
    ms1 = jnp.sum(jnp.where(m0, 0.0, sq), axis=1, keepdims=True)
    ms = jnp.where(m0, ms0, ms1) * (1.0 / HEAD_DIM)
    return hcur * lax.rsqrt(ms + EPS) * gain


def _step(i, zc, zn, xn_ref, xc_ref, ng_ref, w_cols, gb_ref, conv_ref, wal_ref, bal_ref,
          mlg_ref, glg_ref, wout_ref, fg_ref, o_ref,
          kbuf, vbuf, kold, vold, bias_ref, cbuf, ml_c, ml_n, ml_m, gla_s, y_ref, sc_ref, yb_ref,
          *, final_norm, pre_fillers=(), defer_out=False):
    lane = lax.broadcasted_iota(jnp.int32, (1, LANES), 1)
    m0 = lane < HEAD_DIM
    m1 = jnp.logical_not(m0)

    xn = xn_ref[...]
    hn = xn * lax.rsqrt(jnp.sum(xn * xn, axis=1, keepdims=True) * (1.0 / D_MODEL) + EPS) * ng_ref[...]
    hn_b = hn.astype(BF16)
    ip_next = [0]

    fill_q = list(pre_fillers)

    def ip(count):
        for _ in range(count):
            n0 = ip_next[0] * IP_BLK
            if fill_q:
                fill_q.pop(0)()
            elif n0 < D_INP:
                zn[:, n0:n0 + IP_BLK] = _dot(hn_b, w_cols(n0))
                ip_next[0] += 1

    kbuf[:, 0:HIST] = kold[:, TILE:KEYS]
    vbuf[0:HIST, :] = vold[TILE:KEYS, :]
    frow = lax.broadcasted_iota(jnp.int32, (LANES, 1), 0) < HEAD_DIM
    for p in range(N_PAIRS):
        c0 = p * LANES
        kn_t = zc[:, AK + c0:AK + c0 + LANES].T
        vn = zc[:, AV + c0:AV + c0 + LANES]
        kbuf[c0:c0 + LANES, HIST:KEYS] = jnp.where(frow, kn_t, 0.0).astype(BF16)
        kbuf[ATT_W + c0:ATT_W + c0 + LANES, HIST:KEYS] = jnp.where(frow, 0.0, kn_t).astype(BF16)
        vbuf[HIST:KEYS, c0:c0 + LANES] = jnp.where(m0, vn, 1.0).astype(BF16)
        vbuf[HIST:KEYS, ATT_W + c0:ATT_W + c0 + LANES] = jnp.where(m0, 1.0, vn).astype(BF16)

    def att_scores(u):
        h, half = u // 2, u % 2
        c0 = (h // 2) * LANES
        cb = (h % 2) * ATT_W + c0
        r0, k0 = half * HALF_ROWS, half * HALF_ROWS
        q2 = zc[r0:r0 + HALF_ROWS, AQ + c0:AQ + c0 + LANES] * (HEAD_DIM ** -0.5 * LOG2E)
        if h % 2 == 0:
            qm = jnp.where(m0, q2, jnp.where(lane == PEN_LANE_A, 1.0, 0.0))
        else:
            qm = jnp.where(m1, q2, jnp.where(lane == PEN_LANE_B, 1.0, 0.0))
        return (_dot(qm.astype(BF16), kbuf[cb:cb + LANES, k0:k0 + HALF_KEYS])
                + bias_ref[h, r0:r0 + HALF_ROWS, k0:k0 + HALF_KEYS])

    def att_out(u, s):
        h, half = u // 2, u % 2
        cb = (h % 2) * ATT_W + (h // 2) * LANES
        k0 = half * HALF_ROWS
        pe = jnp.exp2(s - jnp.max(s, axis=1, keepdims=True))
        return _dot(pe.astype(BF16), vbuf[k0:k0 + HALF_KEYS, cb:cb + LANES])

    n_units = 2 * ML_HEADS
    s_cur = att_scores(0)
    outs = []
    for u in range(n_units):
        s_nxt = att_scores(u + 1) if u + 1 < n_units else None
        if u % 2 == 1:
            ip(IP_PER_HEAD)
        outs.append(att_out(u, s_cur))
        s_cur = s_nxt
        if u % 4 == 3:
            c0 = (u // 4) * LANES
            r_a = jnp.concatenate(outs[u - 3:u - 1], axis=0)
            r_b = jnp.concatenate(outs[u - 1:u + 1], axis=0)
            den = pltpu.roll(jnp.where(m0, r_b, r_a), HEAD_DIM, 1)
            att = jnp.where(m0, r_a, r_b) / den
            y_ref[:, c0:c0 + LANES] = att * _silu(zc[:, AG + c0:AG + c0 + LANES])
    ip(IP_AFTER_ATT)

    cbuf[8:8 + TILE, :] = zc[:, MQ:MQ + 2 * ML_W]
    for c0 in range(0, 2 * ML_W, LANES):
        acc = cbuf[8:8 + TILE, c0:c0 + LANES] * conv_ref[CONV_W - 1:CONV_W, c0:c0 + LANES]
        for sft in range(1, CONV_W):
            acc = acc + cbuf[8 - sft:8 - sft + TILE, c0:c0 + LANES] * \
                conv_ref[CONV_W - 1 - sft:CONV_W - sft, c0:c0 + LANES]
        act = _silu(acc)
        if c0 >= ML_W:
            act = act * (HEAD_DIM ** -0.5)
        zc[:, MQ + c0:MQ + c0 + LANES] = act
    cbuf[0:8, :] = cbuf[TILE:TILE + 8, :]

    small = zc[:, SM:SM + LANES]
    pre = small + gb_ref[...]
    lf = _log_sigmoid(pre)
    la = _log_sigmoid(_dot(small.astype(BF16), wal_ref[...]) + bal_ref[...]) * (1.0 / GLA_TAU)
    tr = lax.broadcasted_iota(jnp.int32, (TILE, TILE), 0)
    tc = lax.broadcasted_iota(jnp.int32, (TILE, TILE), 1)
    tri = jnp.where(tr >= tc, 1.0, 0.0).astype(BF16)
    terms = []
    for v in (lf, la):
        hi = v.astype(BF16)
        r1 = v - hi.astype(F32)
        mid = r1.astype(BF16)
        terms += [hi, mid, (r1 - mid.astype(F32)).astype(BF16)]
    cs = _dot(tri, jnp.concatenate(terms, axis=1))
    bml = (cs[:, 2 * LANES:3 * LANES] + cs[:, LANES:2 * LANES]) + cs[:, 0:LANES]
    bla = (cs[:, 5 * LANES:6 * LANES] + cs[:, 4 * LANES:5 * LANES]) + cs[:, 3 * LANES:4 * LANES]
    bgl = jnp.concatenate(
        [bla[0:CHUNK]] + [bla[c * CHUNK:(c + 1) * CHUNK] - bla[c * CHUNK - 1:c * CHUNK]
                          for c in range(1, NCH)], axis=0)
    pre_t = pre.T
    bml_t = bml.T
    bgl_t = bgl.T
    ip(1)

    prow = lax.broadcasted_iota(jnp.int32, (LANES, LANES), 0)
    pcol = lax.broadcasted_iota(jnp.int32, (LANES, LANES), 1)
    pair_diag = (prow < HEAD_DIM) == (pcol < HEAD_DIM)
    prow_first = prow[:, 0:1] < HEAD_DIM

    ml_kt, ml_vb, ml_vv, ml_s, ml_ni, ml_ct, ml_nc = [], [], [], [], [], [], []
    for p in range(N_PAIRS):
        c0 = p * LANES
        q2 = zc[:, MQ + c0:MQ + c0 + LANES]
        k2 = zc[:, MK + c0:MK + c0 + LANES]
        v2 = zc[:, MV + c0:MV + c0 + LANES]
        q2b = q2.astype(BF16)
        k2_t = k2.T
        kk_t = jnp.concatenate([jnp.where(frow, k2_t, 0.0), jnp.where(frow, 0.0, k2_t)],
                               axis=1).astype(BF16)
        ct = ml_c[p]
        ncol = ml_n[p][:, 0:1]
        n_ext = jnp.where(lane == 0, jnp.where(frow, ncol, 0.0),
                          jnp.where(lane == 1, jnp.where(frow, 0.0, ncol), 0.0))
        ml_kt.append(k2_t)
        ml_nc.append(ncol)
        ml_vb.append(v2.astype(BF16))
        ml_vv.append(jnp.concatenate([jnp.where(m0, v2, 0.0), jnp.where(m0, 0.0, v2)],
                                     axis=0).astype(BF16))
        ml_ct.append(ct)
        ml_s.append(_dot(q2b, kk_t))
        ml_ni.append(_dot(q2b, jnp.concatenate([ct, n_ext], axis=1).astype(BF16)))
    ip(2)

    for p in range(N_PAIRS):
        c0 = p * LANES
        s_pair = ml_s[p]
        iws, dens, wrows, solds = [], [], [], []
        for e in range(2):
            h = 2 * p + e
            mprev = ml_m[h][0:1, :]
            bt_all = jnp.broadcast_to(bml[:, SM_F + h:SM_F + h + 1], (TILE, LANES))
            bj = bml_t[SM_F + h:SM_F + h + 1, :]
            ij = pre_t[SM_I + h:SM_I + h + 1, :]
            den_inter = ml_ni[p][:, LANES + e:LANES + e + 1]
            iw_blocks, den_blocks = [], []
            for rb in range(NCH):
                r0 = rb * CHUNK
                ncol = LANES * ((r0 + CHUNK + LANES - 1) // LANES)
                rr = lax.broadcasted_iota(jnp.int32, (CHUNK, ncol), 0) + r0
                kc = lax.broadcasted_iota(jnp.int32, (CHUNK, ncol), 1)
                btc = bt_all[r0:r0 + CHUNK, 0:1]
                log_d = jnp.where(rr >= kc, btc - bj[:, :ncol] + ij[:, :ncol], NEG)
                inter_log = btc + mprev[:, 0:1]
                m_row = jnp.maximum(inter_log, jnp.max(log_d, axis=1, keepdims=True))
                sc = s_pair[r0:r0 + CHUNK, e * TILE:e * TILE + ncol] * jnp.exp(log_d - m_row)
                sc_ref[p, r0:r0 + CHUNK, e * TILE:e * TILE + ncol] = sc.astype(BF16)
                if ncol < TILE:
                    sc_ref[p, r0:r0 + CHUNK, e * TILE + ncol:(e + 1) * TILE] = jnp.zeros(
                        (CHUNK, TILE - ncol), BF16)
                inter_w = jnp.exp(inter_log - m_row)
                den = jnp.sum(sc, axis=1, keepdims=True) + inter_w * den_inter[r0:r0 + CHUNK]
                den_blocks.append(jnp.maximum(jnp.abs(den), jnp.exp(-m_row)))
                iw_blocks.append(inter_w)
            iws.append(jnp.concatenate(iw_blocks, axis=0))
            dens.append(jnp.concatenate(den_blocks, axis=0))
            g_row = bt_all[TILE - 1:TILE, :]
            a_row = g_row[:, 0:1] - bj + ij
            m_new = jnp.maximum(g_row + mprev, jnp.max(a_row, axis=1, keepdims=True))
            wrows.append(jnp.exp(a_row - m_new[:, 0:1]))
            solds.append(jnp.exp(g_row + mprev - m_new))
            ml_m[h] = jnp.broadcast_to(m_new, (8, LANES))
        num = _dot(sc_ref[p], ml_vv[p]) + jnp.where(m0, iws[0], iws[1]) * ml_ni[p][:, :LANES]
        hcur = num / jnp.where(m0, dens[0], dens[1])
        y_ref[:, ATT_W + c0:ATT_W + c0 + LANES] = (
            jax.nn.sigmoid(zc[:, MO + c0:MO + c0 + LANES])
            * _pair_rms(hcur, mlg_ref[:, c0:c0 + LANES], m0)
            * _silu(zc[:, MG + c0:MG + c0 + LANES]))
        kw_t = ml_kt[p] * jnp.where(frow, wrows[0], wrows[1])
        upd = _dot(kw_t.astype(BF16), ml_vb[p])
        sold_col = jnp.where(prow_first, solds[0][:, 0:1], solds[1][:, 0:1])
        ml_c[p] = jnp.where(pair_diag, sold_col * ml_ct[p] + upd, 0.0)
        ncol = sold_col * ml_nc[p] + jnp.sum(kw_t, axis=1, keepdims=True)
        ml_n[p] = jnp.broadcast_to(ncol, (LANES, LANES))
        ip(IP_PER_ML_PAIR)

    grow = lax.broadcasted_iota(jnp.int32, (LANES, GLA_VW), 0)
    gcol = lax.broadcasted_iota(jnp.int32, (LANES, GLA_VW), 1)
    gla_diag = (grow >> 5) == (gcol >> 6)
    khead = lane >> 5
    vhead = lax.broadcasted_iota(jnp.int32, (1, GLA_VW), 1) >> 6
    arow = lax.broadcasted_iota(jnp.int32, (CHUNK, GLA_VW), 0)
    acol = lax.broadcasted_iota(jnp.int32, (CHUNK, GLA_VW), 1)
    causal4 = arow >= (acol & (CHUNK - 1))
    n_gh = GLA_KW // GLA_DK
    yc0 = ATT_W + ML_W
    g_a, g_upd, g_qi, g_vvg, g_dec = [], [], [], [], []
    for c in range(NCH):
        r0 = c * CHUNK
        bc = bgl[r0:r0 + CHUNK, :]
        bref = bc[CHUNK // 2 - 1:CHUNK // 2, :]
        btot = bc[CHUNK - 1:CHUNK, :]
        gq = zc[r0:r0 + CHUNK, GQ:GQ + GLA_KW] * (GLA_DK ** -0.5)
        gk = zc[r0:r0 + CHUNK, GK:GK + GLA_KW]
        gv = zc[r0:r0 + CHUNK, GV:GV + GLA_VW]
        qe = (gq * jnp.exp(bc - bref)).astype(BF16)
        ke = gk * jnp.exp(bref - bc)
        kek = jnp.concatenate([jnp.where(khead == h, ke, 0.0) for h in range(n_gh)],
                              axis=0).astype(BF16)
        g_vvg.append(jnp.concatenate([jnp.where(vhead == h, gv, 0.0) for h in range(n_gh)],
                                     axis=0).astype(BF16))
        g_qi.append((gq * jnp.exp(bc)).astype(BF16))
        kd = (gk * jnp.exp(btot - bc)).astype(BF16)
        g_dec.append(jnp.exp(bgl_t[:, r0 + CHUNK - 1:r0 + CHUNK]))
        g_a.append(_dot_nt(qe, kek))
        g_upd.append(_dot_tn(kd, gv.astype(BF16)))
    ip(2)
    s_all = gla_s[...]
    for c in range(NCH):
        r0 = c * CHUNK
        a = jnp.where(causal4, g_a[c], 0.0)
        o = _dot(g_qi[c], s_all.astype(BF16)) + _dot(a.astype(BF16), g_vvg[c])
        y_ref[r0:r0 + CHUNK, yc0:yc0 + GLA_VW] = o
        s_all = jnp.where(gla_diag, g_dec[c] * s_all + g_upd[c], 0.0)
        ip(1)
    gla_s[...] = s_all

    for hp in range(GLA_VW // LANES):
        c0 = hp * LANES
        hcur = y_ref[:, yc0 + c0:yc0 + c0 + LANES]
        y_ref[:, yc0 + c0:yc0 + c0 + LANES] = (
            _pair_rms(hcur, glg_ref[:, c0:c0 + LANES], m0) * _silu(zc[:, GG + c0:GG + c0 + LANES]))
    ip(len(fill_q) + D_INP // IP_BLK)

    if not defer_out:
        xo = xc_ref[...] + _dot(y_ref[...].astype(BF16), wout_ref[...])
        if final_norm:
            xo = xo * lax.rsqrt(jnp.sum(xo * xo, axis=1, keepdims=True) * (1.0 / D_MODEL) + EPS) * fg_ref[...]
        o_ref[...] = xo
        return []

    yb_ref[...] = y_ref[...].astype(BF16)
    ssq = []

    def out_block(nb):
        cols = slice(nb * IP_BLK, (nb + 1) * IP_BLK)
        xo = xc_ref[:, cols] + _dot(yb_ref[...], wout_ref[:, cols])
        o_ref[:, cols] = xo
        if final_norm:
            ssq.append(jnp.sum(xo * xo, axis=1, keepdims=True))

    def out_norm():
        tot = ssq[0]
        for part in ssq[1:]:
            tot = tot + part
        o_ref[...] = o_ref[...] * lax.rsqrt(tot * (1.0 / D_MODEL) + EPS) * fg_ref[...]

    thunks = [functools.partial(out_block, nb) for nb in range(D_MODEL // IP_BLK)]
    return thunks + ([out_norm] if final_norm else [])


def _project(x, ng_ref, w_cols, z_out):
    hn = x * lax.rsqrt(jnp.sum(x * x, axis=1, keepdims=True) * (1.0 / D_MODEL) + EPS) * ng_ref[...]
    hn_b = hn.astype(BF16)
    for n0 in range(0, D_INP, IP_BLK):
        z_out[:, n0:n0 + IP_BLK] = _dot(hn_b, w_cols(n0))


def _layer_kernel(xa_ref, xn_ref, ng_ref, win_ref, wtl_ref, gb_ref, conv_ref, wal_ref, bal_ref, gp_ref,
                  mlg_ref, glg_ref, wout_ref, fg_ref,
                  o_ref,
                  z_a, z_b, kb, vb, bias_ref, cbuf, ml_c, ml_n, ml_m, gla_s, y_ref, sc_ref, wrm, yb_ref,
                  *, final_norm, tiles_per_seq):
    g = pl.program_id(0)
    i = lax.rem(2 * g, tiles_per_seq)

    def w_rows(n0):
        if n0 < W_MAIN:
            return win_ref[n0:n0 + IP_BLK, :]
        return wtl_ref[n0 - W_MAIN:n0 - W_MAIN + IP_BLK, :]

    def w_cols(n0):
        return wrm[:, n0:n0 + IP_BLK]

    @pl.when(g == 0)
    def _first_step():
        r = lax.broadcasted_iota(jnp.int32, (TILE, KEYS), 0)
        m = lax.broadcasted_iota(jnp.int32, (TILE, KEYS), 1)
        jj = m - ((r >> 6) << 6)
        in_band = (jj >= 0) & (jj < BAND)
        for h in range(ML_HEADS):
            row = jnp.broadcast_to(gp_ref[h:h + 1, :], (TILE, ROLL_W))
            rolled = pltpu.roll(row, KEYS, 1, stride=1, stride_axis=0)
            bias_ref[h] = jnp.where(in_band, rolled[:, :KEYS] * LOG2E, NEG)
        for n0 in range(0, D_INP, IP_BLK):
            wrm[:, n0:n0 + IP_BLK] = w_rows(n0).astype(F32).T.astype(BF16)
        _project(xa_ref[0:TILE, :], ng_ref, w_cols, z_a)

    @pl.when(i == 0)
    def _reset():
        kr = lax.broadcasted_iota(jnp.int32, (2 * ATT_W, 1), 0)
        pen_row = (kr & (LANES - 1)) == jnp.where(kr < ATT_W, PEN_LANE_A, PEN_LANE_B)
        pen_init = jnp.broadcast_to(jnp.where(pen_row, NEG, 0.0), kb.shape[1:]).astype(kb.dtype)
        for slot in range(2):
            kb[slot] = pen_init
            vb[slot] = jnp.zeros(vb.shape[1:], vb.dtype)
        cbuf[0:8, :] = jnp.zeros((8, cbuf.shape[1]), F32)
        ml_c[...] = jnp.zeros(ml_c.shape, F32)
        ml_n[...] = jnp.zeros(ml_n.shape, F32)
        ml_m[...] = jnp.zeros(ml_m.shape, F32)
        gla_s[...] = jnp.zeros(gla_s.shape, F32)

    step = functools.partial(
        _step, ng_ref=ng_ref, w_cols=w_cols, gb_ref=gb_ref,
        conv_ref=conv_ref, wal_ref=wal_ref, bal_ref=bal_ref, mlg_ref=mlg_ref, glg_ref=glg_ref,
        wout_ref=wout_ref, fg_ref=fg_ref, bias_ref=bias_ref,
        cbuf=cbuf, ml_c=ml_c, ml_n=ml_n, ml_m=ml_m, gla_s=gla_s, y_ref=y_ref, sc_ref=sc_ref,
        yb_ref=yb_ref, final_norm=final_norm)

    lo = pl.ds(0, TILE)
    hi = pl.ds(TILE, TILE)
    left_over = step(i, z_a, z_b, xn_ref=xa_ref.at[hi], xc_ref=xa_ref.at[lo], o_ref=o_ref.at[lo],
                     kbuf=kb.at[0], vbuf=vb.at[0], kold=kb.at[1], vold=vb.at[1], defer_out=True)
    step(i + 1, z_b, z_a, xn_ref=xn_ref, xc_ref=xa_ref.at[hi], o_ref=o_ref.at[hi],
         kbuf=kb.at[1], vbuf=vb.at[1], kold=kb.at[0], vold=vb.at[0], pre_fillers=left_over)


def _layer_spec(layer, arr, rows=None):
    shape = arr.shape[1:] if rows is None else (rows,) + arr.shape[2:]
    nd = len(shape)
    return pl.BlockSpec((None,) + shape, lambda g, _l=layer, _nd=nd: (_l,) + (0,) * _nd,
                        pipeline_mode=pl.Buffered(1))


def _layer_call(x2d, layer, prm, final_norm, tiles_per_seq):
    n_tiles = x2d.shape[0] // TILE
    n_steps = n_tiles // 2
    names = ("ng", "win", "wtl", "gb", "conv", "wal", "bal", "gp", "mlg", "glg", "wout")
    args = (x2d, x2d) + tuple(prm[k] for k in names) + (prm["fg"],)
    cur_spec = pl.BlockSpec((2 * TILE, D_MODEL), lambda g: (g, 0))
    nxt_spec = pl.BlockSpec((TILE, D_MODEL), lambda g: (jnp.minimum(2 * g + 2, n_tiles - 1), 0))
    in_specs = [cur_spec, nxt_spec]
    in_specs += [_layer_spec(layer, prm[k], rows=W_MAIN if k == "win" else None) for k in names]
    in_specs += [pl.BlockSpec(prm["fg"].shape, lambda g: (0, 0), pipeline_mode=pl.Buffered(1))]
    scratch = [
        pltpu.VMEM((TILE, D_INP), F32),
        pltpu.VMEM((TILE, D_INP), F32),
        pltpu.VMEM((2, 2 * ATT_W, KEYS), BF16),
        pltpu.VMEM((2, KEYS, 2 * ATT_W), BF16),
        pltpu.VMEM((ML_HEADS, TILE, KEYS), F32),
        pltpu.VMEM((TILE + 16, 2 * ML_W), F32),
        pltpu.VMEM((N_PAIRS, LANES, LANES), F32),
        pltpu.VMEM((N_PAIRS, LANES, LANES), F32),
        pltpu.VMEM((ML_HEADS, 8, LANES), F32),
        pltpu.VMEM((LANES, GLA_VW), F32),
        pltpu.VMEM((TILE, D_MODEL), F32),
        pltpu.VMEM((N_PAIRS, TILE, 2 * TILE), BF16),
        pltpu.VMEM((D_MODEL, D_INP), BF16),
        pltpu.VMEM((TILE, D_MODEL), BF16),
    ]
    return pl.pallas_call(
        functools.partial(_layer_kernel, final_norm=final_norm, tiles_per_seq=tiles_per_seq),
        out_shape=jax.ShapeDtypeStruct(x2d.shape, x2d.dtype),
        grid=(n_steps,),
        in_specs=in_specs,
        out_specs=cur_spec,
        scratch_shapes=scratch,
        compiler_params=pltpu.CompilerParams(
            dimension_semantics=("arbitrary",),
            vmem_limit_bytes=VMEM_LIMIT),
        name="hybrid_layer_final" if final_norm else "hybrid_layer",
    )(*args)


def _prep_params(norm_g, w_in, b_gates, conv_w, w_alpha, b_alpha, rel_bias, ml_norm_g, gla_norm_g, w_out, final_g):
    depth = norm_g.shape[0]
    w_t = jnp.transpose(w_in, (0, 2, 1)).astype(BF16)
    pad = jnp.zeros((depth, D_INP - SM - GLA_RANK - 2 * ML_HEADS, D_MODEL), BF16)
    wtl = jnp.concatenate([w_t[:, 3084:3980], w_t[:, 3996:4252], w_t[:, 3980:3996],
                           w_t[:, 3072:3084], pad], axis=1)
    gb = jnp.pad(b_gates, ((0, 0), (SM_I, LANES - SM_I - 2 * ML_HEADS))).reshape(depth, 1, LANES)
    wal = jnp.pad(w_alpha, ((0, 0), (0, LANES - GLA_RANK), (0, 0))).astype(BF16)
    nh = rel_bias.shape[1]
    gp = jnp.concatenate([
        jnp.broadcast_to(rel_bias[:, :, 2 * REL_CLIP:], (depth, nh, KEYS - REL_CLIP + 1)),
        rel_bias[:, :, 2 * REL_CLIP - 1:0:-1],
        jnp.broadcast_to(rel_bias[:, :, :1], (depth, nh, ROLL_W - KEYS - REL_CLIP)),
    ], axis=2)
    return dict(ng=norm_g.reshape(depth, 1, D_MODEL), win=w_t, wtl=wtl, gb=gb, conv=conv_w, wal=wal,
                bal=b_alpha.reshape(depth, 1, GLA_KW), gp=gp, mlg=ml_norm_g.reshape(depth, 1, ML_W),
                glg=gla_norm_g.reshape(depth, 1, GLA_VW), wout=w_out.astype(BF16),
                fg=final_g.reshape(1, D_MODEL))


def kernel(x, norm_g, w_in, b_gates, conv_w, w_alpha, b_alpha, rel_bias, ml_norm_g, gla_norm_g, w_out, final_g):
    depth = norm_g.shape[0]
    bsz, seq, _ = x.shape
    prm = _prep_params(norm_g, w_in, b_gates, conv_w, w_alpha, b_alpha, rel_bias, ml_norm_g, gla_norm_g,
                       w_out, final_g)
    x2d = x.reshape(bsz * seq, D_MODEL)
    for l in range(depth):
        x2d = _layer_call(x2d, l, prm, final_norm=(l == depth - 1), tiles_per_seq=seq // TILE)
    return x2d.reshape(bsz, seq, D_MODEL)
```

```python
import functools

import jax
import jax.numpy as jnp
from jax import lax
from jax.experimental import pallas as pl
from jax.experimental.pallas import tpu as pltpu

F32 = jnp.float32
BF16 = jnp.bfloat16

D_MODEL = 1024
CHUNK = 64
HEAD_DIM = 64
ATT_W = 384
ML_W = 384
GLA_KW = 128
GLA_VW = 256
GLA_DK = 32
GLA_RANK = 16
GLA_TAU = 16.0
ML_HEADS = 6
PAST_CHUNKS = 8
REL_CLIP = 128
CONV_W = 4
EPS = 1e-6
NEG = -1e30
LOG2E = 1.4426950408889634

LANES = 128
TILE = 256
NCH = TILE // CHUNK
HIST = PAST_CHUNKS * CHUNK
KEYS = HIST + TILE
HALF_ROWS = TILE // 2
HALF_KEYS = HIST + HALF_ROWS
PEN_LANE_A = HEAD_DIM
PEN_LANE_B = 0
BAND = (PAST_CHUNKS + 1) * CHUNK
ROLL_W = 1024
N_PAIRS = ML_W // LANES

AQ, AK, AV, AG = 0, 384, 768, 1152
MQ, MK, MV, MO, MG = 1536, 1920, 2304, 2688, 3072
GQ, GK, GV, GG = 3456, 3584, 3712, 3968
SM = 4224
D_INP = 4352
IP_BLK = 256
W_MAIN = 3072
IP_PER_HEAD, IP_AFTER_ATT, IP_PER_ML_PAIR = 1, 1, 1
SM_I = GLA_RANK
SM_F = GLA_RANK + ML_HEADS

VMEM_LIMIT = 60 * 1024 * 1024


def _log_sigmoid(x):
    return jnp.minimum(x, 0.0) - jnp.log1p(jnp.exp(-jnp.abs(x)))


def _silu(x):
    return x * jax.nn.sigmoid(x)


def _dot_nt(a, b):
    return lax.dot_general(a, b, (((1,), (1,)), ((), ())), preferred_element_type=F32)


def _dot_tn(a, b):
    return lax.dot_general(a, b, (((0,), (0,)), ((), ())), preferred_element_type=F32)


def _dot(a, b):
    return jnp.dot(a, b, preferred_element_type=F32)


def _pair_rms(hcur, gain, m0):
    sq = hcur * hcur
    ms0 = jnp.sum(jnp.where(m0, sq, 0.0), axis=1, keepdims=True)
    ms1 = jnp.sum(jnp.where(m0, 0.0, sq), axis=1, keepdims=True)
    ms = jnp.where(m0, ms0, ms1) * (1.0 / HEAD_DIM)
    return hcur * lax.rsqrt(ms + EPS) * gain


def _step(i, zc, zn, hn_in, hn_out, xh_ref, xc_ref, ng_ref, w_cols, gb_ref, conv_ref, wal_ref, bal_ref,
          mlg_ref, glg_ref, wout_ref, fg_ref, o_ref,
          kbuf, vbuf, kold, vold, bias_ref, cbuf, ml_c, ml_n, ml_m, gla_s, y_ref, sc_ref, *, final_norm):
    lane = lax.broadcasted_iota(jnp.int32, (1, LANES), 1)
    m0 = lane < HEAD_DIM
    m1 = jnp.logical_not(m0)

    hn_b = hn_in[...]
    ip_next = [0]

    def ip(count):
        for _ in range(count):
            n0 = ip_next[0] * IP_BLK
            if n0 < D_INP:
                zn[:, n0:n0 + IP_BLK] = _dot(hn_b, w_cols(n0))
                ip_next[0] += 1

    kbuf[:, 0:HIST] = kold[:, TILE:KEYS]
    vbuf[0:HIST, :] = vold[TILE:KEYS, :]
    frow = lax.broadcasted_iota(jnp.int32, (LANES, 1), 0) < HEAD_DIM
    for p in range(N_PAIRS):
        c0 = p * LANES
        kn_t = zc[:, AK + c0:AK + c0 + LANES].T
        vn = zc[:, AV + c0:AV + c0 + LANES]
        kbuf[c0:c0 + LANES, HIST:KEYS] = jnp.where(frow, kn_t, 0.0).astype(BF16)
        kbuf[ATT_W + c0:ATT_W + c0 + LANES, HIST:KEYS] = jnp.where(frow, 0.0, kn_t).astype(BF16)
        vbuf[HIST:KEYS, c0:c0 + LANES] = jnp.where(m0, vn, 1.0).astype(BF16)
        vbuf[HIST:KEYS, ATT_W + c0:ATT_W + c0 + LANES] = jnp.where(m0, 1.0, vn).astype(BF16)

    def att_scores(u):
        h, half = u // 2, u % 2
        c0 = (h // 2) * LANES
        cb = (h % 2) * ATT_W + c0
        r0, k0 = half * HALF_ROWS, half * HALF_ROWS
        q2 = zc[r0:r0 + HALF_ROWS, AQ + c0:AQ + c0 + LANES] * (HEAD_DIM ** -0.5 * LOG2E)
        if h % 2 == 0:
            qm = jnp.where(m0, q2, jnp.where(lane == PEN_LANE_A, 1.0, 0.0))
        else:
            qm = jnp.where(m1, q2, jnp.where(lane == PEN_LANE_B, 1.0, 0.0))
        return (_dot(qm.astype(BF16), kbuf[cb:cb + LANES, k0:k0 + HALF_KEYS])
                + bias_ref[h, r0:r0 + HALF_ROWS, k0:k0 + HALF_KEYS])

    def att_out(u, s):
        h, half = u // 2, u % 2
        cb = (h % 2) * ATT_W + (h // 2) * LANES
        k0 = half * HALF_ROWS
        pe = jnp.exp2(s - jnp.max(s, axis=1, keepdims=True))
        return _dot(pe.astype(BF16), vbuf[k0:k0 + HALF_KEYS, cb:cb + LANES])

    n_units = 2 * ML_HEADS
    s_cur = att_scores(0)
    outs = []
    for u in range(n_units):
        s_nxt = att_scores(u + 1) if u + 1 < n_units else None
        if u % 2 == 1:
            ip(IP_PER_HEAD)
        outs.append(att_out(u, s_cur))
        s_cur = s_nxt
        if u % 4 == 3:
            c0 = (u // 4) * LANES
            r_a = jnp.concatenate(outs[u - 3:u - 1], axis=0)
            r_b = jnp.concatenate(outs[u - 1:u + 1], axis=0)
            den = pltpu.roll(jnp.where(m0, r_b, r_a), HEAD_DIM, 1)
            att = jnp.where(m0, r_a, r_b) / den
            y_ref[:, c0:c0 + LANES] = att * _silu(zc[:, AG + c0:AG + c0 + LANES])
    ip(IP_AFTER_ATT)

    cbuf[8:8 + TILE, :] = zc[:, MQ:MQ + 2 * ML_W]
    for c0 in range(0, 2 * ML_W, LANES):
        acc = cbuf[8:8 + TILE, c0:c0 + LANES] * conv_ref[CONV_W - 1:CONV_W, c0:c0 + LANES]
        for sft in range(1, CONV_W):
            acc = acc + cbuf[8 - sft:8 - sft + TILE, c0:c0 + LANES] * \
                conv_ref[CONV_W - 1 - sft:CONV_W - sft, c0:c0 + LANES]
        act = _silu(acc)
        if c0 >= ML_W:
            act = act * (HEAD_DIM ** -0.5)
        zc[:, MQ + c0:MQ + c0 + LANES] = act
    cbuf[0:8, :] = cbuf[TILE:TILE + 8, :]

    small = zc[:, SM:SM + LANES]
    pre = small + gb_ref[...]
    lf = _log_sigmoid(pre)
    la = _log_sigmoid(_dot(small.astype(BF16), wal_ref[...]) + bal_ref[...]) * (1.0 / GLA_TAU)
    tr = lax.broadcasted_iota(jnp.int32, (TILE, TILE), 0)
    tc = lax.broadcasted_iota(jnp.int32, (TILE, TILE), 1)
    tri = jnp.where(tr >= tc, 1.0, 0.0).astype(BF16)
    terms = []
    for v in (lf, la):
        hi = v.astype(BF16)
        r1 = v - hi.astype(F32)
        mid = r1.astype(BF16)
        terms += [hi, mid, (r1 - mid.astype(F32)).astype(BF16)]
    cs = _dot(tri, jnp.concatenate(terms, axis=1))
    bml = (cs[:, 2 * LANES:3 * LANES] + cs[:, LANES:2 * LANES]) + cs[:, 0:LANES]
    bla = (cs[:, 5 * LANES:6 * LANES] + cs[:, 4 * LANES:5 * LANES]) + cs[:, 3 * LANES:4 * LANES]
    bgl = jnp.concatenate(
        [bla[0:CHUNK]] + [bla[c * CHUNK:(c + 1) * CHUNK] - bla[c * CHUNK - 1:c * CHUNK]
                          for c in range(1, NCH)], axis=0)
    pre_t = pre.T
    bml_t = bml.T
    bgl_t = bgl.T
    ip(1)

    prow = lax.broadcasted_iota(jnp.int32, (LANES, LANES), 0)
    pcol = lax.broadcasted_iota(jnp.int32, (LANES, LANES), 1)
    pair_diag = (prow < HEAD_DIM) == (pcol < HEAD_DIM)
    prow_first = prow[:, 0:1] < HEAD_DIM

    ml_kt, ml_vb, ml_vv, ml_s, ml_ni, ml_ct, ml_nc = [], [], [], [], [], [], []
    for p in range(N_PAIRS):
        c0 = p * LANES
        q2 = zc[:, MQ + c0:MQ + c0 + LANES]
        k2 = zc[:, MK + c0:MK + c0 + LANES]
        v2 = zc[:, MV + c0:MV + c0 + LANES]
        q2b = q2.astype(BF16)
        k2_t = k2.T
        kk_t = jnp.concatenate([jnp.where(frow, k2_t, 0.0), jnp.where(frow, 0.0, k2_t)],
                               axis=1).astype(BF16)
        ct = ml_c[p]
        ncol = ml_n[p][:, 0:1]
        n_ext = jnp.where(lane == 0, jnp.where(frow, ncol, 0.0),
                          jnp.where(lane == 1, jnp.where(frow, 0.0, ncol), 0.0))
        ml_kt.append(k2_t)
        ml_nc.append(ncol)
        ml_vb.append(v2.astype(BF16))
        ml_vv.append(jnp.concatenate([jnp.where(m0, v2, 0.0), jnp.where(m0, 0.0, v2)],
                                     axis=0).astype(BF16))
        ml_ct.append(ct)
        ml_s.append(_dot(q2b, kk_t))
        ml_ni.append(_dot(q2b, jnp.concatenate([ct, n_ext], axis=1).astype(BF16)))
    ip(2)

    for p in range(N_PAIRS):
        c0 = p * LANES
        s_pair = ml_s[p]
        iws, dens, wrows, solds = [], [], [], []
        for e in range(2):
            h = 2 * p + e
            mprev = ml_m[h][0:1, :]
            bt_all = jnp.broadcast_to(bml[:, SM_F + h:SM_F + h + 1], (TILE, LANES))
            bj = bml_t[SM_F + h:SM_F + h + 1, :]
            ij = pre_t[SM_I + h:SM_I + h + 1, :]
            den_inter = ml_ni[p][:, LANES + e:LANES + e + 1]
            iw_blocks, den_blocks = [], []
            for rb in range(NCH):
                r0 = rb * CHUNK
                ncol = LANES * ((r0 + CHUNK + LANES - 1) // LANES)
                rr = lax.broadcasted_iota(jnp.int32, (CHUNK, ncol), 0) + r0
                kc = lax.broadcasted_iota(jnp.int32, (CHUNK, ncol), 1)
                btc = bt_all[r0:r0 + CHUNK, 0:1]
                log_d = jnp.where(rr >= kc, btc - bj[:, :ncol] + ij[:, :ncol], NEG)
                inter_log = btc + mprev[:, 0:1]
                m_row = jnp.maximum(inter_log, jnp.max(log_d, axis=1, keepdims=True))
                sc = s_pair[r0:r0 + CHUNK, e * TILE:e * TILE + ncol] * jnp.exp(log_d - m_row)
                sc_ref[p, r0:r0 + CHUNK, e * TILE:e * TILE + ncol] = sc.astype(BF16)
                if ncol < TILE:
                    sc_ref[p, r0:r0 + CHUNK, e * TILE + ncol:(e + 1) * TILE] = jnp.zeros(
                        (CHUNK, TILE - ncol), BF16)
                inter_w = jnp.exp(inter_log - m_row)
                den = jnp.sum(sc, axis=1, keepdims=True) + inter_w * den_inter[r0:r0 + CHUNK]
                den_blocks.append(jnp.maximum(jnp.abs(den), jnp.exp(-m_row)))
                iw_blocks.append(inter_w)
            iws.append(jnp.concatenate(iw_blocks, axis=0))
            dens.append(jnp.concatenate(den_blocks, axis=0))
            g_row = bt_all[TILE - 1:TILE, :]
            a_row = g_row[:, 0:1] - bj + ij
            m_new = jnp.maximum(g_row + mprev, jnp.max(a_row, axis=1, keepdims=True))
            wrows.append(jnp.exp(a_row - m_new[:, 0:1]))
            solds.append(jnp.exp(g_row + mprev - m_new))
            ml_m[h] = jnp.broadcast_to(m_new, (8, LANES))
        num = _dot(sc_ref[p], ml_vv[p]) + jnp.where(m0, iws[0], iws[1]) * ml_ni[p][:, :LANES]
        hcur = num / jnp.where(m0, dens[0], dens[1])
        y_ref[:, ATT_W + c0:ATT_W + c0 + LANES] = (
            jax.nn.sigmoid(zc[:, MO + c0:MO + c0 + LANES])
            * _pair_rms(hcur, mlg_ref[:, c0:c0 + LANES], m0)
            * _silu(zc[:, MG + c0:MG + c0 + LANES]))
        kw_t = ml_kt[p] * jnp.where(frow, wrows[0], wrows[1])
        upd = _dot(kw_t.astype(BF16), ml_vb[p])
        sold_col = jnp.where(prow_first, solds[0][:, 0:1], solds[1][:, 0:1])
        ml_c[p] = jnp.where(pair_diag, sold_col * ml_ct[p] + upd, 0.0)
        ncol = sold_col * ml_nc[p] + jnp.sum(kw_t, axis=1, keepdims=True)
        ml_n[p] = jnp.broadcast_to(ncol, (LANES, LANES))
        ip(IP_PER_ML_PAIR)

    grow = lax.broadcasted_iota(jnp.int32, (LANES, GLA_VW), 0)
    gcol = lax.broadcasted_iota(jnp.int32, (LANES, GLA_VW), 1)
    gla_diag = (grow >> 5) == (gcol >> 6)
    khead = lane >> 5
    vhead = lax.broadcasted_iota(jnp.int32, (1, GLA_VW), 1) >> 6
    arow = lax.broadcasted_iota(jnp.int32, (CHUNK, GLA_VW), 0)
    acol = lax.broadcasted_iota(jnp.int32, (CHUNK, GLA_VW), 1)
    causal4 = arow >= (acol & (CHUNK - 1))
    n_gh = GLA_KW // GLA_DK
    yc0 = ATT_W + ML_W
    g_a, g_upd, g_qi, g_vvg, g_dec = [], [], [], [], []
    for c in range(NCH):
        r0 = c * CHUNK
        bc = bgl[r0:r0 + CHUNK, :]
        bref = bc[CHUNK // 2 - 1:CHUNK // 2, :]
        btot = bc[CHUNK - 1:CHUNK, :]
        gq = zc[r0:r0 + CHUNK, GQ:GQ + GLA_KW] * (GLA_DK ** -0.5)
        gk = zc[r0:r0 + CHUNK, GK:GK + GLA_KW]
        gv = zc[r0:r0 + CHUNK, GV:GV + GLA_VW]
        qe = (gq * jnp.exp(bc - bref)).astype(BF16)
        ke = gk * jnp.exp(bref - bc)
        kek = jnp.concatenate([jnp.where(khead == h, ke, 0.0) for h in range(n_gh)],
                              axis=0).astype(BF16)
        g_vvg.append(jnp.concatenate([jnp.where(vhead == h, gv, 0.0) for h in range(n_gh)],
                                     axis=0).astype(BF16))
        g_qi.append((gq * jnp.exp(bc)).astype(BF16))
        kd = (gk * jnp.exp(btot - bc)).astype(BF16)
        g_dec.append(jnp.exp(bgl_t[:, r0 + CHUNK - 1:r0 + CHUNK]))
        g_a.append(_dot_nt(qe, kek))
        g_upd.append(_dot_tn(kd, gv.astype(BF16)))
    ip(2)
    s_all = gla_s[...]
    for c in range(NCH):
        r0 = c * CHUNK
        a = jnp.where(causal4, g_a[c], 0.0)
        o = _dot(g_qi[c], s_all.astype(BF16)) + _dot(a.astype(BF16), g_vvg[c])
        y_ref[r0:r0 + CHUNK, yc0:yc0 + GLA_VW] = o
        s_all = jnp.where(gla_diag, g_dec[c] * s_all + g_upd[c], 0.0)
        ip(1)
    gla_s[...] = s_all

    for hp in range(GLA_VW // LANES):
        c0 = hp * LANES
        hcur = y_ref[:, yc0 + c0:yc0 + c0 + LANES]
        y_ref[:, yc0 + c0:yc0 + c0 + LANES] = (
            _pair_rms(hcur, glg_ref[:, c0:c0 + LANES], m0) * _silu(zc[:, GG + c0:GG + c0 + LANES]))
    ip(D_INP // IP_BLK)

    hn_out[...] = _rms_bf16(xh_ref[...], ng_ref)

    xo = xc_ref[...] + _dot(y_ref[...].astype(BF16), wout_ref[...])
    if final_norm:
        xo = xo * lax.rsqrt(jnp.sum(xo * xo, axis=1, keepdims=True) * (1.0 / D_MODEL) + EPS) * fg_ref[...]
    o_ref[...] = xo


def _rms_bf16(x, ng_ref):
    hn = x * lax.rsqrt(jnp.sum(x * x, axis=1, keepdims=True) * (1.0 / D_MODEL) + EPS) * ng_ref[...]
    return hn.astype(BF16)


def _project(x, ng_ref, w_cols, z_out):
    hn_b = _rms_bf16(x, ng_ref)
    for n0 in range(0, D_INP, IP_BLK):
        z_out[:, n0:n0 + IP_BLK] = _dot(hn_b, w_cols(n0))


def _layer_kernel(xa_ref, xn_ref, ng_ref, win_ref, wtl_ref, gb_ref, conv_ref, wal_ref, bal_ref, gp_ref,
                  mlg_ref, glg_ref, wout_ref, fg_ref,
                  o_ref,
                  z_a, z_b, kb, vb, bias_ref, cbuf, ml_c, ml_n, ml_m, gla_s, y_ref, sc_ref, wrm, hnb,
                  *, final_norm, tiles_per_seq):
    g = pl.program_id(0)
    i = lax.rem(2 * g, tiles_per_seq)

    def w_rows(n0):
        if n0 < W_MAIN:
            return win_ref[n0:n0 + IP_BLK, :]
        return wtl_ref[n0 - W_MAIN:n0 - W_MAIN + IP_BLK, :]

    def w_cols(n0):
        return wrm[:, n0:n0 + IP_BLK]

    @pl.when(g == 0)
    def _first_step():
        r = lax.broadcasted_iota(jnp.int32, (TILE, KEYS), 0)
        m = lax.broadcasted_iota(jnp.int32, (TILE, KEYS), 1)
        jj = m - ((r >> 6) << 6)
        in_band = (jj >= 0) & (jj < BAND)
        for h in range(ML_HEADS):
            row = jnp.broadcast_to(gp_ref[h:h + 1, :], (TILE, ROLL_W))
            rolled = pltpu.roll(row, KEYS, 1, stride=1, stride_axis=0)
            bias_ref[h] = jnp.where(in_band, rolled[:, :KEYS] * LOG2E, NEG)
        for n0 in range(0, D_INP, IP_BLK):
            wrm[:, n0:n0 + IP_BLK] = w_rows(n0).astype(F32).T.astype(BF16)
        _project(xa_ref[0:TILE, :], ng_ref, w_cols, z_a)
        hnb[1] = _rms_bf16(xa_ref[TILE:2 * TILE, :], ng_ref)

    @pl.when(i == 0)
    def _reset():
        kr = lax.broadcasted_iota(jnp.int32, (2 * ATT_W, 1), 0)
        pen_row = (kr & (LANES - 1)) == jnp.where(kr < ATT_W, PEN_LANE_A, PEN_LANE_B)
        pen_init = jnp.broadcast_to(jnp.where(pen_row, NEG, 0.0), kb.shape[1:]).astype(kb.dtype)
        for slot in range(2):
            kb[slot] = pen_init
            vb[slot] = jnp.zeros(vb.shape[1:], vb.dtype)
        cbuf[0:8, :] = jnp.zeros((8, cbuf.shape[1]), F32)
        ml_c[...] = jnp.zeros(ml_c.shape, F32)
        ml_n[...] = jnp.zeros(ml_n.shape, F32)
        ml_m[...] = jnp.zeros(ml_m.shape, F32)
        gla_s[...] = jnp.zeros(gla_s.shape, F32)

    step = functools.partial(
        _step, ng_ref=ng_ref, w_cols=w_cols, gb_ref=gb_ref,
        conv_ref=conv_ref, wal_ref=wal_ref, bal_ref=bal_ref, mlg_ref=mlg_ref, glg_ref=glg_ref,
        wout_ref=wout_ref, fg_ref=fg_ref, bias_ref=bias_ref,
        cbuf=cbuf, ml_c=ml_c, ml_n=ml_n, ml_m=ml_m, gla_s=gla_s, y_ref=y_ref, sc_ref=sc_ref,
        final_norm=final_norm)

    lo = pl.ds(0, TILE)
    hi = pl.ds(TILE, TILE)
    step(i, z_a, z_b, hnb.at[1], hnb.at[0], xh_ref=xn_ref.at[lo], xc_ref=xa_ref.at[lo], o_ref=o_ref.at[lo],
         kbuf=kb.at[0], vbuf=vb.at[0], kold=kb.at[1], vold=vb.at[1])
    step(i + 1, z_b, z_a, hnb.at[0], hnb.at[1], xh_ref=xn_ref.at[hi], xc_ref=xa_ref.at[hi], o_ref=o_ref.at[hi],
         kbuf=kb.at[1], vbuf=vb.at[1], kold=kb.at[0], vold=vb.at[0])


def _layer_spec(layer, arr, rows=None):
    shape = arr.shape[1:] if rows is None else (rows,) + arr.shape[2:]
    nd = len(shape)
    return pl.BlockSpec((None,) + shape, lambda g, _l=layer, _nd=nd: (_l,) + (0,) * _nd,
                        pipeline_mode=pl.Buffered(1))


def _layer_call(x2d, layer, prm, final_norm, tiles_per_seq):
    n_tiles = x2d.shape[0] // TILE
    n_steps = n_tiles // 2
    names = ("ng", "win", "wtl", "gb", "conv", "wal", "bal", "gp", "mlg", "glg", "wout")
    args = (x2d, x2d) + tuple(prm[k] for k in names) + (prm["fg"],)
    cur_spec = pl.BlockSpec((2 * TILE, D_MODEL), lambda g: (g, 0))
    nxt_spec = pl.BlockSpec((2 * TILE, D_MODEL), lambda g: (jnp.minimum(g + 1, n_steps - 1), 0))
    in_specs = [cur_spec, nxt_spec]
    in_specs += [_layer_spec(layer, prm[k], rows=W_MAIN if k == "win" else None) for k in names]
    in_specs += [pl.BlockSpec(prm["fg"].shape, lambda g: (0, 0), pipeline_mode=pl.Buffered(1))]
    scratch = [
        pltpu.VMEM((TILE, D_INP), F32),
        pltpu.VMEM((TILE, D_INP), F32),
        pltpu.VMEM((2, 2 * ATT_W, KEYS), BF16),
        pltpu.VMEM((2, KEYS, 2 * ATT_W), BF16),
        pltpu.VMEM((ML_HEADS, TILE, KEYS), F32),
        pltpu.VMEM((TILE + 16, 2 * ML_W), F32),
        pltpu.VMEM((N_PAIRS, LANES, LANES), F32),
        pltpu.VMEM((N_PAIRS, LANES, LANES), F32),
        pltpu.VMEM((ML_HEADS, 8, LANES), F32),
        pltpu.VMEM((LANES, GLA_VW), F32),
        pltpu.VMEM((TILE, D_MODEL), F32),
        pltpu.VMEM((N_PAIRS, TILE, 2 * TILE), BF16),
        pltpu.VMEM((D_MODEL, D_INP), BF16),
        pltpu.VMEM((2, TILE, D_MODEL), BF16),
    ]
    return pl.pallas_call(
        functools.partial(_layer_kernel, final_norm=final_norm, tiles_per_seq=tiles_per_seq),
        out_shape=jax.ShapeDtypeStruct(x2d.shape, x2d.dtype),
        grid=(n_steps,),
        in_specs=in_specs,
        out_specs=cur_spec,
        scratch_shapes=scratch,
        compiler_params=pltpu.CompilerParams(
            dimension_semantics=("arbitrary",),
            vmem_limit_bytes=VMEM_LIMIT),
        name="hybrid_layer_final" if final_norm else "hybrid_layer",
    )(*args)


def _prep_params(norm_g, w_in, b_gates, conv_w, w_alpha, b_alpha, rel_bias, ml_norm_g, gla_norm_g, w_out, final_g):
    depth = norm_g.shape[0]
    w_t = jnp.transpose(w_in, (0, 2, 1)).astype(BF16)
    pad = jnp.zeros((depth, D_INP - SM - GLA_RANK - 2 * ML_HEADS, D_MODEL), BF16)
    wtl = jnp.concatenate([w_t[:, 3084:3980], w_t[:, 3996:4252], w_t[:, 3980:3996],
                           w_t[:, 3072:3084], pad], axis=1)
    gb = jnp.pad(b_gates, ((0, 0), (SM_I, LANES - SM_I - 2 * ML_HEADS))).reshape(depth, 1, LANES)
    wal = jnp.pad(w_alpha, ((0, 0), (0, LANES - GLA_RANK), (0, 0))).astype(BF16)
    nh = rel_bias.shape[1]
    gp = jnp.concatenate([
        jnp.broadcast_to(rel_bias[:, :, 2 * REL_CLIP:], (depth, nh, KEYS - REL_CLIP + 1)),
        rel_bias[:, :, 2 * REL_CLIP - 1:0:-1],
        jnp.broadcast_to(rel_bias[:, :, :1], (depth, nh, ROLL_W - KEYS - REL_CLIP)),
    ], axis=2)
    return dict(ng=norm_g.reshape(depth, 1, D_MODEL), win=w_t, wtl=wtl, gb=gb, conv=conv_w, wal=wal,
                bal=b_alpha.reshape(depth, 1, GLA_KW), gp=gp, mlg=ml_norm_g.reshape(depth, 1, ML_W),
                glg=gla_norm_g.reshape(depth, 1, GLA_VW), wout=w_out.astype(BF16),
                fg=final_g.reshape(1, D_MODEL))


def kernel(x, norm_g, w_in, b_gates, conv_w, w_alpha, b_alpha, rel_bias, ml_norm_g, gla_norm_g, w_out, final_g):
    depth = norm_g.shape[0]
    bsz, seq, _ = x.shape
    prm = _prep_params(norm_g, w_in, b_gates, conv_w, w_alpha, b_alpha, rel_bias, ml_norm_g, gla_norm_g,
                       w_out, final_g)
    x2d = x.reshape(bsz * seq, D_MODEL)
    for l in range(depth):
        x2d = _layer_call(x2d, l, prm, final_norm=(l == depth - 1), tiles_per_seq=seq // TILE)
    return x2d.reshape(bsz, seq, D_MODEL)
```

```python
import functools

import jax
import jax.numpy as jnp
from jax import lax
from jax.experimental import pallas as pl
from jax.experimental.pallas import tpu as pltpu

F32 = jnp.float32
BF16 = jnp.bfloat16

D_MODEL = 1024
CHUNK = 64
HEAD_DIM = 64
ATT_W = 384
ML_W = 384
GLA_KW = 128
GLA_VW = 256
GLA_DK = 32
GLA_RANK = 16
GLA_TAU = 16.0
ML_HEADS = 6
PAST_CHUNKS = 8
REL_CLIP = 128
CONV_W = 4
EPS = 1e-6
NEG = -1e30
LOG2E = 1.4426950408889634

LANES = 128
TILE = 256
NCH = TILE // CHUNK
HIST = PAST_CHUNKS * CHUNK
KEYS = HIST + TILE
HALF_ROWS = TILE // 2
HALF_KEYS = HIST + HALF_ROWS
PEN_LANE_A = HEAD_DIM
PEN_LANE_B = 0
BAND = (PAST_CHUNKS + 1) * CHUNK
ROLL_W = 1024
N_PAIRS = ML_W // LANES

AQ, AK, AV, AG = 0, 384, 768, 1152
MQ, MK, MV, MO, MG = 1536, 1920, 2304, 2688, 3072
GQ, GK, GV, GG = 3456, 3584, 3712, 3968
SM = 4224
D_INP = 4352
IP_BLK = 256
W_MAIN = 3072
IP_PER_HEAD, IP_AFTER_ATT, IP_PER_ML_PAIR = 1, 1, 1
SM_I = GLA_RANK
SM_F = GLA_RANK + ML_HEADS

VMEM_LIMIT = 56 * 1024 * 1024


def _log_sigmoid(x):
    return jnp.minimum(x, 0.0) - jnp.log1p(jnp.exp(-jnp.abs(x)))


def _silu(x):
    return x * jax.nn.sigmoid(x)


def _dot_nt(a, b):
    return lax.dot_general(a, b, (((1,), (1,)), ((), ())), preferred_element_type=F32)


def _dot_tn(a, b):
    return lax.dot_general(a, b, (((0,), (0,)), ((), ())), preferred_element_type=F32)


def _dot(a, b):
    return jnp.dot(a, b, preferred_element_type=F32)


def _pair_rms(hcur, gain, m0):
    sq = hcur * hcur
    ms0 = jnp.sum(jnp.where(m0, sq, 0.0), axis=1, keepdims=True)
    ms1 = jnp.sum(jnp.where(m0, 0.0, sq), axis=1, keepdims=True)
    ms = jnp.where(m0, ms0, ms1) * (1.0 / HEAD_DIM)
    return hcur * lax.rsqrt(ms + EPS) * gain


def _step(i, zc, zn, xn_ref, xc_ref, ng_ref, w_cols, gb_ref, conv_ref, wal_ref, bal_ref,
          mlg_ref, glg_ref, wout_ref, fg_ref, o_ref,
          kbuf, vbuf, kold, vold, bias_ref, cbuf, ml_c, ml_n, ml_m, gla_s, y_ref, sc_ref, *, final_norm):
    lane = lax.broadcasted_iota(jnp.int32, (1, LANES), 1)
    m0 = lane < HEAD_DIM
    m1 = jnp.logical_not(m0)

    xn = xn_ref[...]
    hn = xn * lax.rsqrt(jnp.sum(xn * xn, axis=1, keepdims=True) * (1.0 / D_MODEL) + EPS) * ng_ref[...]
    hn_b = hn.astype(BF16)
    ip_next = [0]

    def ip(count):
        for _ in range(count):
            n0 = ip_next[0] * IP_BLK
            if n0 < D_INP:
                zn[:, n0:n0 + IP_BLK] = _dot(hn_b, w_cols(n0))
                ip_next[0] += 1

    kbuf[:, 0:HIST] = kold[:, TILE:KEYS]
    vbuf[0:HIST, :] = vold[TILE:KEYS, :]
    frow = lax.broadcasted_iota(jnp.int32, (LANES, 1), 0) < HEAD_DIM
    for p in range(N_PAIRS):
        c0 = p * LANES
        kn_t = zc[:, AK + c0:AK + c0 + LANES].T
        vn = zc[:, AV + c0:AV + c0 + LANES]
        kbuf[c0:c0 + LANES, HIST:KEYS] = jnp.where(frow, kn_t, 0.0).astype(BF16)
        kbuf[ATT_W + c0:ATT_W + c0 + LANES, HIST:KEYS] = jnp.where(frow, 0.0, kn_t).astype(BF16)
        vbuf[HIST:KEYS, c0:c0 + LANES] = jnp.where(m0, vn, 1.0).astype(BF16)
        vbuf[HIST:KEYS, ATT_W + c0:ATT_W + c0 + LANES] = jnp.where(m0, 1.0, vn).astype(BF16)

    def att_scores(u):
        h, half = u // 2, u % 2
        c0 = (h // 2) * LANES
        cb = (h % 2) * ATT_W + c0
        r0, k0 = half * HALF_ROWS, half * HALF_ROWS
        q2 = zc[r0:r0 + HALF_ROWS, AQ + c0:AQ + c0 + LANES] * (HEAD_DIM ** -0.5 * LOG2E)
        if h % 2 == 0:
            qm = jnp.where(m0, q2, jnp.where(lane == PEN_LANE_A, 1.0, 0.0))
        else:
            qm = jnp.where(m1, q2, jnp.where(lane == PEN_LANE_B, 1.0, 0.0))
        return (_dot(qm.astype(BF16), kbuf[cb:cb + LANES, k0:k0 + HALF_KEYS])
                + bias_ref[h, r0:r0 + HALF_ROWS, k0:k0 + HALF_KEYS])

    def att_out(u, s):
        h, half = u // 2, u % 2
        cb = (h % 2) * ATT_W + (h // 2) * LANES
        k0 = half * HALF_ROWS
        pe = jnp.exp2(s - jnp.max(s, axis=1, keepdims=True))
        return _dot(pe.astype(BF16), vbuf[k0:k0 + HALF_KEYS, cb:cb + LANES])

    n_units = 2 * ML_HEADS
    s_cur = att_scores(0)
    outs = []
    for u in range(n_units):
        s_nxt = att_scores(u + 1) if u + 1 < n_units else None
        if u % 2 == 1:
            ip(IP_PER_HEAD)
        outs.append(att_out(u, s_cur))
        s_cur = s_nxt
        if u % 4 == 3:
            c0 = (u // 4) * LANES
            r_a = jnp.concatenate(outs[u - 3:u - 1], axis=0)
            r_b = jnp.concatenate(outs[u - 1:u + 1], axis=0)
            den = pltpu.roll(jnp.where(m0, r_b, r_a), HEAD_DIM, 1)
            att = jnp.where(m0, r_a, r_b) / den
            y_ref[:, c0:c0 + LANES] = att * _silu(zc[:, AG + c0:AG + c0 + LANES])
    ip(IP_AFTER_ATT)

    cbuf[8:8 + TILE, :] = zc[:, MQ:MQ + 2 * ML_W]
    for c0 in range(0, 2 * ML_W, LANES):
        acc = cbuf[8:8 + TILE, c0:c0 + LANES] * conv_ref[CONV_W - 1:CONV_W, c0:c0 + LANES]
        for sft in range(1, CONV_W):
            acc = acc + cbuf[8 - sft:8 - sft + TILE, c0:c0 + LANES] * \
                conv_ref[CONV_W - 1 - sft:CONV_W - sft, c0:c0 + LANES]
        act = _silu(acc)
        if c0 >= ML_W:
            act = act * (HEAD_DIM ** -0.5)
        zc[:, MQ + c0:MQ + c0 + LANES] = act
    cbuf[0:8, :] = cbuf[TILE:TILE + 8, :]

    small = zc[:, SM:SM + LANES]
    pre = small + gb_ref[...]
    lf = _log_sigmoid(pre)
    la = _log_sigmoid(_dot(small.astype(BF16), wal_ref[...]) + bal_ref[...]) * (1.0 / GLA_TAU)
    tr = lax.broadcasted_iota(jnp.int32, (TILE, TILE), 0)
    tc = lax.broadcasted_iota(jnp.int32, (TILE, TILE), 1)
    tri = jnp.where(tr >= tc, 1.0, 0.0).astype(BF16)
    terms = []
    for v in (lf, la):
        hi = v.astype(BF16)
        r1 = v - hi.astype(F32)
        mid = r1.astype(BF16)
        terms += [hi, mid, (r1 - mid.astype(F32)).astype(BF16)]
    cs = _dot(tri, jnp.concatenate(terms, axis=1))
    bml = (cs[:, 2 * LANES:3 * LANES] + cs[:, LANES:2 * LANES]) + cs[:, 0:LANES]
    bla = (cs[:, 5 * LANES:6 * LANES] + cs[:, 4 * LANES:5 * LANES]) + cs[:, 3 * LANES:4 * LANES]
    bgl = jnp.concatenate(
        [bla[0:CHUNK]] + [bla[c * CHUNK:(c + 1) * CHUNK] - bla[c * CHUNK - 1:c * CHUNK]
                          for c in range(1, NCH)], axis=0)
    pre_t = pre.T
    bml_t = bml.T
    bgl_t = bgl.T
    ip(1)

    prow = lax.broadcasted_iota(jnp.int32, (LANES, LANES), 0)
    pcol = lax.broadcasted_iota(jnp.int32, (LANES, LANES), 1)
    pair_diag = (prow < HEAD_DIM) == (pcol < HEAD_DIM)
    prow_first = prow[:, 0:1] < HEAD_DIM

    ml_kt, ml_vb, ml_vv, ml_s, ml_ni, ml_ct, ml_nc = [], [], [], [], [], [], []
    for p in range(N_PAIRS):
        c0 = p * LANES
        q2 = zc[:, MQ + c0:MQ + c0 + LANES]
        k2 = zc[:, MK + c0:MK + c0 + LANES]
        v2 = zc[:, MV + c0:MV + c0 + LANES]
        q2b = q2.astype(BF16)
        k2_t = k2.T
        kk_t = jnp.concatenate([jnp.where(frow, k2_t, 0.0), jnp.where(frow, 0.0, k2_t)],
                               axis=1).astype(BF16)
        ct = ml_c[p]
        ncol = ml_n[p][:, 0:1]
        n_ext = jnp.where(lane == 0, jnp.where(frow, ncol, 0.0),
                          jnp.where(lane == 1, jnp.where(frow, 0.0, ncol), 0.0))
        ml_kt.append(k2_t)
        ml_nc.append(ncol)
        ml_vb.append(v2.astype(BF16))
        ml_vv.append(jnp.concatenate([jnp.where(m0, v2, 0.0), jnp.where(m0, 0.0, v2)],
                                     axis=0).astype(BF16))
        ml_ct.append(ct)
        ml_s.append(_dot(q2b, kk_t))
        ml_ni.append(_dot(q2b, jnp.concatenate([ct, n_ext], axis=1).astype(BF16)))
    ip(2)

    for p in range(N_PAIRS):
        c0 = p * LANES
        s_pair = ml_s[p]
        iws, dens, wrows, solds = [], [], [], []
        for e in range(2):
            h = 2 * p + e
            mprev = ml_m[h][0:1, :]
            bt_all = jnp.broadcast_to(bml[:, SM_F + h:SM_F + h + 1], (TILE, LANES))
            bj = bml_t[SM_F + h:SM_F + h + 1, :]
            ij = pre_t[SM_I + h:SM_I + h + 1, :]
            den_inter = ml_ni[p][:, LANES + e:LANES + e + 1]
            iw_blocks, den_blocks = [], []
            for rb in range(NCH):
                r0 = rb * CHUNK
                ncol = LANES * ((r0 + CHUNK + LANES - 1) // LANES)
                rr = lax.broadcasted_iota(jnp.int32, (CHUNK, ncol), 0) + r0
                kc = lax.broadcasted_iota(jnp.int32, (CHUNK, ncol), 1)
                btc = bt_all[r0:r0 + CHUNK, 0:1]
                log_d = jnp.where(rr >= kc, btc - bj[:, :ncol] + ij[:, :ncol], NEG)
                inter_log = btc + mprev[:, 0:1]
                m_row = jnp.maximum(inter_log, jnp.max(log_d, axis=1, keepdims=True))
                sc = s_pair[r0:r0 + CHUNK, e * TILE:e * TILE + ncol] * jnp.exp(log_d - m_row)
                sc_ref[p, r0:r0 + CHUNK, e * TILE:e * TILE + ncol] = sc.astype(BF16)
                if ncol < TILE:
                    sc_ref[p, r0:r0 + CHUNK, e * TILE + ncol:(e + 1) * TILE] = jnp.zeros(
                        (CHUNK, TILE - ncol), BF16)
                inter_w = jnp.exp(inter_log - m_row)
                den = jnp.sum(sc, axis=1, keepdims=True) + inter_w * den_inter[r0:r0 + CHUNK]
                den_blocks.append(jnp.maximum(jnp.abs(den), jnp.exp(-m_row)))
                iw_blocks.append(inter_w)
            iws.append(jnp.concatenate(iw_blocks, axis=0))
            dens.append(jnp.concatenate(den_blocks, axis=0))
            g_row = bt_all[TILE - 1:TILE, :]
            a_row = g_row[:, 0:1] - bj + ij
            m_new = jnp.maximum(g_row + mprev, jnp.max(a_row, axis=1, keepdims=True))
            wrows.append(jnp.exp(a_row - m_new[:, 0:1]))
            solds.append(jnp.exp(g_row + mprev - m_new))
            ml_m[h] = jnp.broadcast_to(m_new, (8, LANES))
        num = _dot(sc_ref[p], ml_vv[p]) + jnp.where(m0, iws[0], iws[1]) * ml_ni[p][:, :LANES]
        hcur = num / jnp.where(m0, dens[0], dens[1])
        y_ref[:, ATT_W + c0:ATT_W + c0 + LANES] = (
            jax.nn.sigmoid(zc[:, MO + c0:MO + c0 + LANES])
            * _pair_rms(hcur, mlg_ref[:, c0:c0 + LANES], m0)
            * _silu(zc[:, MG + c0:MG + c0 + LANES]))
        kw_t = ml_kt[p] * jnp.where(frow, wrows[0], wrows[1])
        upd = _dot(kw_t.astype(BF16), ml_vb[p])
        sold_col = jnp.where(prow_first, solds[0][:, 0:1], solds[1][:, 0:1])
        ml_c[p] = jnp.where(pair_diag, sold_col * ml_ct[p] + upd, 0.0)
        ncol = sold_col * ml_nc[p] + jnp.sum(kw_t, axis=1, keepdims=True)
        ml_n[p] = jnp.broadcast_to(ncol, (LANES, LANES))
        ip(IP_PER_ML_PAIR)

    grow = lax.broadcasted_iota(jnp.int32, (LANES, GLA_VW), 0)
    gcol = lax.broadcasted_iota(jnp.int32, (LANES, GLA_VW), 1)
    gla_diag = (grow >> 5) == (gcol >> 6)
    khead = lane >> 5
    vhead = lax.broadcasted_iota(jnp.int32, (1, GLA_VW), 1) >> 6
    arow = lax.broadcasted_iota(jnp.int32, (CHUNK, GLA_VW), 0)
    acol = lax.broadcasted_iota(jnp.int32, (CHUNK, GLA_VW), 1)
    causal4 = arow >= (acol & (CHUNK - 1))
    n_gh = GLA_KW // GLA_DK
    yc0 = ATT_W + ML_W
    g_a, g_upd, g_qi, g_vvg, g_dec = [], [], [], [], []
    for c in range(NCH):
        r0 = c * CHUNK
        bc = bgl[r0:r0 + CHUNK, :]
        bref = bc[CHUNK // 2 - 1:CHUNK // 2, :]
        btot = bc[CHUNK - 1:CHUNK, :]
        gq = zc[r0:r0 + CHUNK, GQ:GQ + GLA_KW] * (GLA_DK ** -0.5)
        gk = zc[r0:r0 + CHUNK, GK:GK + GLA_KW]
        gv = zc[r0:r0 + CHUNK, GV:GV + GLA_VW]
        qe = (gq * jnp.exp(bc - bref)).astype(BF16)
        ke = gk * jnp.exp(bref - bc)
        kek = jnp.concatenate([jnp.where(khead == h, ke, 0.0) for h in range(n_gh)],
                              axis=0).astype(BF16)
        g_vvg.append(jnp.concatenate([jnp.where(vhead == h, gv, 0.0) for h in range(n_gh)],
                                     axis=0).astype(BF16))
        g_qi.append(gq * jnp.exp(bc))
        kd = (gk * jnp.exp(btot - bc)).astype(BF16)
        g_dec.append(jnp.exp(bgl_t[:, r0 + CHUNK - 1:r0 + CHUNK]))
        g_a.append(_dot_nt(qe, kek))
        g_upd.append(_dot_tn(kd, gv.astype(BF16)))
    ip(2)
    s0 = gla_s[...]
    ltot = [bgl[c * CHUNK + CHUNK - 1:c * CHUNK + CHUNK, :] for c in range(NCH)]

    def decayed(q, chunks):
        if not chunks:
            return q
        acc = ltot[chunks[0]]
        for m in chunks[1:]:
            acc = acc + ltot[m]
        return q * jnp.exp(acc)

    lhs_rows = []
    for c in range(NCH):
        parts = [decayed(g_qi[c], list(range(c)))]
        for cp in range(NCH - 1):
            parts.append(decayed(g_qi[c], list(range(cp + 1, c))) if cp < c
                         else jnp.zeros((CHUNK, GLA_KW), F32))
        lhs_rows.append(jnp.concatenate(parts, axis=1).astype(BF16))
    u_m = [jnp.where(gla_diag, g_upd[c], 0.0) for c in range(NCH)]
    rhs = jnp.concatenate([s0] + u_m[:NCH - 1], axis=0).astype(BF16)
    o_inter = _dot(jnp.concatenate(lhs_rows, axis=0), rhs)
    s_all = s0
    for c in range(NCH):
        r0 = c * CHUNK
        a = jnp.where(causal4, g_a[c], 0.0)
        y_ref[r0:r0 + CHUNK, yc0:yc0 + GLA_VW] = (
            o_inter[r0:r0 + CHUNK, :] + _dot(a.astype(BF16), g_vvg[c]))
        s_all = g_dec[c] * s_all + u_m[c]
        ip(1)
    gla_s[...] = s_all

    for hp in range(GLA_VW // LANES):
        c0 = hp * LANES
        hcur = y_ref[:, yc0 + c0:yc0 + c0 + LANES]
        y_ref[:, yc0 + c0:yc0 + c0 + LANES] = (
            _pair_rms(hcur, glg_ref[:, c0:c0 + LANES], m0) * _silu(zc[:, GG + c0:GG + c0 + LANES]))
    ip(D_INP // IP_BLK)

    xo = xc_ref[...] + _dot(y_ref[...].astype(BF16), wout_ref[...])
    if final_norm:
        xo = xo * lax.rsqrt(jnp.sum(xo * xo, axis=1, keepdims=True) * (1.0 / D_MODEL) + EPS) * fg_ref[...]
    o_ref[...] = xo


def _project(x, ng_ref, w_cols, z_out):
    hn = x * lax.rsqrt(jnp.sum(x * x, axis=1, keepdims=True) * (1.0 / D_MODEL) + EPS) * ng_ref[...]
    hn_b = hn.astype(BF16)
    for n0 in range(0, D_INP, IP_BLK):
        z_out[:, n0:n0 + IP_BLK] = _dot(hn_b, w_cols(n0))


def _layer_kernel(xa_ref, xn_ref, ng_ref, win_ref, wtl_ref, gb_ref, conv_ref, wal_ref, bal_ref, gp_ref,
                  mlg_ref, glg_ref, wout_ref, fg_ref,
                  o_ref,
                  z_a, z_b, kb, vb, bias_ref, cbuf, ml_c, ml_n, ml_m, gla_s, y_ref, sc_ref, wrm,
                  *, final_norm, tiles_per_seq):
    g = pl.program_id(0)
    i = lax.rem(2 * g, tiles_per_seq)

    def w_rows(n0):
        if n0 < W_MAIN:
            return win_ref[n0:n0 + IP_BLK, :]
        return wtl_ref[n0 - W_MAIN:n0 - W_MAIN + IP_BLK, :]

    def w_cols(n0):
        return wrm[:, n0:n0 + IP_BLK]

    @pl.when(g == 0)
    def _first_step():
        r = lax.broadcasted_iota(jnp.int32, (TILE, KEYS), 0)
        m = lax.broadcasted_iota(jnp.int32, (TILE, KEYS), 1)
        jj = m - ((r >> 6) << 6)
        in_band = (jj >= 0) & (jj < BAND)
        for h in range(ML_HEADS):
            row = jnp.broadcast_to(gp_ref[h:h + 1, :], (TILE, ROLL_W))
            rolled = pltpu.roll(row, KEYS, 1, stride=1, stride_axis=0)
            bias_ref[h] = jnp.where(in_band, rolled[:, :KEYS] * LOG2E, NEG)
        for n0 in range(0, D_INP, IP_BLK):
            wrm[:, n0:n0 + IP_BLK] = w_rows(n0).astype(F32).T.astype(BF16)
        _project(xa_ref[0:TILE, :], ng_ref, w_cols, z_a)

    @pl.when(i == 0)
    def _reset():
        kr = lax.broadcasted_iota(jnp.int32, (2 * ATT_W, 1), 0)
        pen_row = (kr & (LANES - 1)) == jnp.where(kr < ATT_W, PEN_LANE_A, PEN_LANE_B)
        pen_init = jnp.broadcast_to(jnp.where(pen_row, NEG, 0.0), kb.shape[1:]).astype(kb.dtype)
        for slot in range(2):
            kb[slot] = pen_init
            vb[slot] = jnp.zeros(vb.shape[1:], vb.dtype)
        cbuf[0:8, :] = jnp.zeros((8, cbuf.shape[1]), F32)
        ml_c[...] = jnp.zeros(ml_c.shape, F32)
        ml_n[...] = jnp.zeros(ml_n.shape, F32)
        ml_m[...] = jnp.zeros(ml_m.shape, F32)
        gla_s[...] = jnp.zeros(gla_s.shape, F32)

    step = functools.partial(
        _step, ng_ref=ng_ref, w_cols=w_cols, gb_ref=gb_ref,
        conv_ref=conv_ref, wal_ref=wal_ref, bal_ref=bal_ref, mlg_ref=mlg_ref, glg_ref=glg_ref,
        wout_ref=wout_ref, fg_ref=fg_ref, bias_ref=bias_ref,
        cbuf=cbuf, ml_c=ml_c, ml_n=ml_n, ml_m=ml_m, gla_s=gla_s, y_ref=y_ref, sc_ref=sc_ref,
        final_norm=final_norm)

    lo = pl.ds(0, TILE)
    hi = pl.ds(TILE, TILE)
    step(i, z_a, z_b, xn_ref=xa_ref.at[hi], xc_ref=xa_ref.at[lo], o_ref=o_ref.at[lo],
         kbuf=kb.at[0], vbuf=vb.at[0], kold=kb.at[1], vold=vb.at[1])
    step(i + 1, z_b, z_a, xn_ref=xn_ref, xc_ref=xa_ref.at[hi], o_ref=o_ref.at[hi],
         kbuf=kb.at[1], vbuf=vb.at[1], kold=kb.at[0], vold=vb.at[0])


def _layer_spec(layer, arr, rows=None):
    shape = arr.shape[1:] if rows is None else (rows,) + arr.shape[2:]
    nd = len(shape)
    return pl.BlockSpec((None,) + shape, lambda g, _l=layer, _nd=nd: (_l,) + (0,) * _nd,
                        pipeline_mode=pl.Buffered(1))


def _layer_call(x2d, layer, prm, final_norm, tiles_per_seq):
    n_tiles = x2d.shape[0] // TILE
    n_steps = n_tiles // 2
    names = ("ng", "win", "wtl", "gb", "conv", "wal", "bal", "gp", "mlg", "glg", "wout")
    args = (x2d, x2d) + tuple(prm[k] for k in names) + (prm["fg"],)
    cur_spec = pl.BlockSpec((2 * TILE, D_MODEL), lambda g: (g, 0))
    nxt_spec = pl.BlockSpec((TILE, D_MODEL), lambda g: (jnp.minimum(2 * g + 2, n_tiles - 1), 0))
    in_specs = [cur_spec, nxt_spec]
    in_specs += [_layer_spec(layer, prm[k], rows=W_MAIN if k == "win" else None) for k in names]
    in_specs += [pl.BlockSpec(prm["fg"].shape, lambda g: (0, 0), pipeline_mode=pl.Buffered(1))]
    scratch = [
        pltpu.VMEM((TILE, D_INP), F32),
        pltpu.VMEM((TILE, D_INP), F32),
        pltpu.VMEM((2, 2 * ATT_W, KEYS), BF16),
        pltpu.VMEM((2, KEYS, 2 * ATT_W), BF16),
        pltpu.VMEM((ML_HEADS, TILE, KEYS), F32),
        pltpu.VMEM((TILE + 16, 2 * ML_W), F32),
        pltpu.VMEM((N_PAIRS, LANES, LANES), F32),
        pltpu.VMEM((N_PAIRS, LANES, LANES), F32),
        pltpu.VMEM((ML_HEADS, 8, LANES), F32),
        pltpu.VMEM((LANES, GLA_VW), F32),
        pltpu.VMEM((TILE, D_MODEL), F32),
        pltpu.VMEM((N_PAIRS, TILE, 2 * TILE), BF16),
        pltpu.VMEM((D_MODEL, D_INP), BF16),
    ]
    return pl.pallas_call(
        functools.partial(_layer_kernel, final_norm=final_norm, tiles_per_seq=tiles_per_seq),
        out_shape=jax.ShapeDtypeStruct(x2d.shape, x2d.dtype),
        grid=(n_steps,),
        in_specs=in_specs,
        out_specs=cur_spec,
        scratch_shapes=scratch,
        compiler_params=pltpu.CompilerParams(
            dimension_semantics=("arbitrary",),
            vmem_limit_bytes=VMEM_LIMIT),
        name="hybrid_layer_final" if final_norm else "hybrid_layer",
    )(*args)


def _prep_params(norm_g, w_in, b_gates, conv_w, w_alpha, b_alpha, rel_bias, ml_norm_g, gla_norm_g, w_out, final_g):
    depth = norm_g.shape[0]
    w_t = jnp.transpose(w_in, (0, 2, 1)).astype(BF16)
    pad = jnp.zeros((depth, D_INP - SM - GLA_RANK - 2 * ML_HEADS, D_MODEL), BF16)
    wtl = jnp.concatenate([w_t[:, 3084:3980], w_t[:, 3996:4252], w_t[:, 3980:3996],
                           w_t[:, 3072:3084], pad], axis=1)
    gb = jnp.pad(b_gates, ((0, 0), (SM_I, LANES - SM_I - 2 * ML_HEADS))).reshape(depth, 1, LANES)
    wal = jnp.pad(w_alpha, ((0, 0), (0, LANES - GLA_RANK), (0, 0))).astype(BF16)
    nh = rel_bias.shape[1]
    gp = jnp.concatenate([
        jnp.broadcast_to(rel_bias[:, :, 2 * REL_CLIP:], (depth, nh, KEYS - REL_CLIP + 1)),
        rel_bias[:, :, 2 * REL_CLIP - 1:0:-1],
        jnp.broadcast_to(rel_bias[:, :, :1], (depth, nh, ROLL_W - KEYS - REL_CLIP)),
    ], axis=2)
    return dict(ng=norm_g.reshape(depth, 1, D_MODEL), win=w_t, wtl=wtl, gb=gb, conv=conv_w, wal=wal,
                bal=b_alpha.reshape(depth, 1, GLA_KW), gp=gp, mlg=ml_norm_g.reshape(depth, 1, ML_W),
                glg=gla_norm_g.reshape(depth, 1, GLA_VW), wout=w_out.astype(BF16),
                fg=final_g.reshape(1, D_MODEL))


def kernel(x, norm_g, w_in, b_gates, conv_w, w_alpha, b_alpha, rel_bias, ml_norm_g, gla_norm_g, w_out, final_g):
    depth = norm_g.shape[0]
    bsz, seq, _ = x.shape
    prm = _prep_params(norm_g, w_in, b_gates, conv_w, w_alpha, b_alpha, rel_bias, ml_norm_g, gla_norm_g,
                       w_out, final_g)
    x2d = x.reshape(bsz * seq, D_MODEL)
    for l in range(depth):
        x2d = _layer_call(x2d, l, prm, final_norm=(l == depth - 1), tiles_per_seq=seq // TILE)
    return x2d.reshape(bsz, seq, D_MODEL)
```

```python
import functools

import jax
import jax.numpy as jnp
from jax import lax
from jax.experimental import pallas as pl
from jax.experimental.pallas import tpu as pltpu

F32 = jnp.float32
BF16 = jnp.bfloat16

D_MODEL = 1024
CHUNK = 64
HEAD_DIM = 64
ATT_W = 384
ML_W = 384
GLA_KW = 128
GLA_VW = 256
GLA_DK = 32
GLA_RANK = 16
GLA_TAU = 16.0
ML_HEADS = 6
PAST_CHUNKS = 8
REL_CLIP = 128
CONV_W = 4
EPS = 1e-6
NEG = -1e30
LOG2E = 1.4426950408889634

LANES = 128
TILE = 256
NCH = TILE // CHUNK
HIST = PAST_CHUNKS * CHUNK
KEYS = HIST + TILE
HALF_ROWS = TILE // 2
HALF_KEYS = HIST + HALF_ROWS
PEN_LANE_A = HEAD_DIM
PEN_LANE_B = 0
BAND = (PAST_CHUNKS + 1) * CHUNK
ROLL_W = 1024
N_PAIRS = ML_W // LANES

AQ, AK, AV, AG = 0, 384, 768, 1152
MQ, MK, MV, MO, MG = 1536, 1920, 2304, 2688, 3072
GQ, GK, GV, GG = 3456, 3584, 3712, 3968
SM = 4224
D_INP = 4352
IP_BLK = 256
W_MAIN = 3072
IP_PER_HEAD, IP_AFTER_ATT, IP_PER_ML_PAIR = 1, 1, 1
SM_I = GLA_RANK
SM_F = GLA_RANK + ML_HEADS

VMEM_LIMIT = 56 * 1024 * 1024


def _log_sigmoid(x):
    return jnp.minimum(x, 0.0) - jnp.log1p(jnp.exp(-jnp.abs(x)))


def _silu(x):
    return x * jax.nn.sigmoid(x)


def _dot_nt(a, b):
    return lax.dot_general(a, b, (((1,), (1,)), ((), ())), preferred_element_type=F32)


def _dot_tn(a, b):
    return lax.dot_general(a, b, (((0,), (0,)), ((), ())), preferred_element_type=F32)


def _dot(a, b):
    return jnp.dot(a, b, preferred_element_type=F32)


def _pair_rms(hcur, gain, m0):
    sq = hcur * hcur
    ms0 = jnp.sum(jnp.where(m0, sq, 0.0), axis=1, keepdims=True)
    ms1 = jnp.sum(jnp.where(m0, 0.0, sq), axis=1, keepdims=True)
    ms = jnp.where(m0, ms0, ms1) * (1.0 / HEAD_DIM)
    return hcur * lax.rsqrt(ms + EPS) * gain


def _step(i, zc, zn, xn_ref, xc_ref, ng_ref, w_cols, gb_ref, conv_ref, wal_ref, bal_ref,
          mlg_ref, glg_ref, wout_ref, fg_ref, o_ref,
          kbuf, vbuf, kold, vold, bias_ref, cbuf, ml_c, ml_n, ml_m, gla_s, y_ref, sc_ref, *, final_norm):
    lane = lax.broadcasted_iota(jnp.int32, (1, LANES), 1)
    m0 = lane < HEAD_DIM
    m1 = jnp.logical_not(m0)

    xn = xn_ref[...]
    hn = xn * lax.rsqrt(jnp.sum(xn * xn, axis=1, keepdims=True) * (1.0 / D_MODEL) + EPS) * ng_ref[...]
    hn_b = hn.astype(BF16)
    ip_next = [0]

    def ip(count):
        for _ in range(count):
            n0 = ip_next[0] * IP_BLK
            if n0 < D_INP:
                zn[:, n0:n0 + IP_BLK] = _dot(hn_b, w_cols(n0))
                ip_next[0] += 1

    kbuf[:, 0:HIST] = kold[:, TILE:KEYS]
    vbuf[0:HIST, :] = vold[TILE:KEYS, :]
    frow = lax.broadcasted_iota(jnp.int32, (LANES, 1), 0) < HEAD_DIM
    for p in range(N_PAIRS):
        c0 = p * LANES
        kn_t = zc[:, AK + c0:AK + c0 + LANES].T
        vn = zc[:, AV + c0:AV + c0 + LANES]
        kbuf[c0:c0 + LANES, HIST:KEYS] = jnp.where(frow, kn_t, 0.0).astype(BF16)
        kbuf[ATT_W + c0:ATT_W + c0 + LANES, HIST:KEYS] = jnp.where(frow, 0.0, kn_t).astype(BF16)
        vbuf[HIST:KEYS, c0:c0 + LANES] = jnp.where(m0, vn, 1.0).astype(BF16)
        vbuf[HIST:KEYS, ATT_W + c0:ATT_W + c0 + LANES] = jnp.where(m0, 1.0, vn).astype(BF16)

    def att_scores(u):
        h, half = u // 2, u % 2
        c0 = (h // 2) * LANES
        cb = (h % 2) * ATT_W + c0
        r0, k0 = half * HALF_ROWS, half * HALF_ROWS
        q2 = zc[r0:r0 + HALF_ROWS, AQ + c0:AQ + c0 + LANES] * (HEAD_DIM ** -0.5 * LOG2E)
        if h % 2 == 0:
            qm = jnp.where(m0, q2, jnp.where(lane == PEN_LANE_A, 1.0, 0.0))
        else:
            qm = jnp.where(m1, q2, jnp.where(lane == PEN_LANE_B, 1.0, 0.0))
        return (_dot(qm.astype(BF16), kbuf[cb:cb + LANES, k0:k0 + HALF_KEYS])
                + bias_ref[h, r0:r0 + HALF_ROWS, k0:k0 + HALF_KEYS])

    def att_out(u, s):
        h, half = u // 2, u % 2
        cb = (h % 2) * ATT_W + (h // 2) * LANES
        k0 = half * HALF_ROWS
        pe = jnp.exp2(s - jnp.max(s, axis=1, keepdims=True))
        return _dot(pe.astype(BF16), vbuf[k0:k0 + HALF_KEYS, cb:cb + LANES])

    n_units = 2 * ML_HEADS
    s_cur = att_scores(0)
    outs = []
    for u in range(n_units):
        s_nxt = att_scores(u + 1) if u + 1 < n_units else None
        if u % 2 == 1:
            ip(IP_PER_HEAD)
        outs.append(att_out(u, s_cur))
        s_cur = s_nxt
        if u % 4 == 3:
            c0 = (u // 4) * LANES
            r_a = jnp.concatenate(outs[u - 3:u - 1], axis=0)
            r_b = jnp.concatenate(outs[u - 1:u + 1], axis=0)
            den = pltpu.roll(jnp.where(m0, r_b, r_a), HEAD_DIM, 1)
            att = jnp.where(m0, r_a, r_b) / den
            y_ref[:, c0:c0 + LANES] = (att * _silu(zc[:, AG + c0:AG + c0 + LANES])).astype(BF16)
    ip(IP_AFTER_ATT)

    cbuf[8:8 + TILE, :] = zc[:, MQ:MQ + 2 * ML_W]
    for c0 in range(0, 2 * ML_W, LANES):
        acc = cbuf[8:8 + TILE, c0:c0 + LANES] * conv_ref[CONV_W - 1:CONV_W, c0:c0 + LANES]
        for sft in range(1, CONV_W):
            acc = acc + cbuf[8 - sft:8 - sft + TILE, c0:c0 + LANES] * \
                conv_ref[CONV_W - 1 - sft:CONV_W - sft, c0:c0 + LANES]
        act = _silu(acc)
        if c0 >= ML_W:
            act = act * (HEAD_DIM ** -0.5)
        zc[:, MQ + c0:MQ + c0 + LANES] = act
    cbuf[0:8, :] = cbuf[TILE:TILE + 8, :]

    small = zc[:, SM:SM + LANES]
    pre = small + gb_ref[...]
    lf = _log_sigmoid(pre)
    la = _log_sigmoid(_dot(small.astype(BF16), wal_ref[...]) + bal_ref[...]) * (1.0 / GLA_TAU)
    tr = lax.broadcasted_iota(jnp.int32, (TILE, TILE), 0)
    tc = lax.broadcasted_iota(jnp.int32, (TILE, TILE), 1)
    tri = jnp.where(tr >= tc, 1.0, 0.0).astype(BF16)
    terms = []
    for v in (lf, la):
        hi = v.astype(BF16)
        r1 = v - hi.astype(F32)
        mid = r1.astype(BF16)
        terms += [hi, mid, (r1 - mid.astype(F32)).astype(BF16)]
    cs = _dot(tri, jnp.concatenate(terms, axis=1))
    bml = ((cs[:, 2 * LANES:3 * LANES] + cs[:, LANES:2 * LANES]) + cs[:, 0:LANES]) * LOG2E
    bla = (cs[:, 5 * LANES:6 * LANES] + cs[:, 4 * LANES:5 * LANES]) + cs[:, 3 * LANES:4 * LANES]
    bgl = jnp.concatenate(
        [bla[0:CHUNK]] + [bla[c * CHUNK:(c + 1) * CHUNK] - bla[c * CHUNK - 1:c * CHUNK]
                          for c in range(1, NCH)], axis=0)
    pre_t = (pre * LOG2E).T
    bml_t = bml.T
    bgl_t = bgl.T
    ip(1)

    prow = lax.broadcasted_iota(jnp.int32, (LANES, LANES), 0)
    pcol = lax.broadcasted_iota(jnp.int32, (LANES, LANES), 1)
    pair_diag = (prow < HEAD_DIM) == (pcol < HEAD_DIM)
    prow_first = prow[:, 0:1] < HEAD_DIM

    ml_kt, ml_vb, ml_vv, ml_s, ml_ni, ml_ct, ml_nc = [], [], [], [], [], [], []
    for p in range(N_PAIRS):
        c0 = p * LANES
        q2 = zc[:, MQ + c0:MQ + c0 + LANES]
        k2 = zc[:, MK + c0:MK + c0 + LANES]
        v2 = zc[:, MV + c0:MV + c0 + LANES]
        q2b = q2.astype(BF16)
        k2_t = k2.T
        kk_t = jnp.concatenate([jnp.where(frow, k2_t, 0.0), jnp.where(frow, 0.0, k2_t)],
                               axis=1).astype(BF16)
        ct = ml_c[p]
        ncol = ml_n[p][:, 0:1]
        n_ext = jnp.where(lane == 0, jnp.where(frow, ncol, 0.0),
                          jnp.where(lane == 1, jnp.where(frow, 0.0, ncol), 0.0))
        ml_kt.append(k2_t)
        ml_nc.append(ncol)
        ml_vb.append(v2.astype(BF16))
        ml_vv.append(jnp.concatenate([jnp.where(m0, v2, 0.0), jnp.where(m0, 0.0, v2)],
                                     axis=0).astype(BF16))
        ml_ct.append(ct)
        ml_s.append(_dot(q2b, kk_t))
        ml_ni.append(_dot(q2b, jnp.concatenate([ct, n_ext], axis=1).astype(BF16)))
    ip(2)

    for p in range(N_PAIRS):
        c0 = p * LANES
        s_pair = ml_s[p]
        iws, dens, wrows, solds = [], [], [], []
        for e in range(2):
            h = 2 * p + e
            mprev = ml_m[h][0:1, :]
            bt_all = jnp.broadcast_to(bml[:, SM_F + h:SM_F + h + 1], (TILE, LANES))
            bj = bml_t[SM_F + h:SM_F + h + 1, :]
            ij = pre_t[SM_I + h:SM_I + h + 1, :]
            den_inter = ml_ni[p][:, LANES + e:LANES + e + 1]
            iw_blocks, den_blocks = [], []
            for rb in range(NCH):
                r0 = rb * CHUNK
                ncol = LANES * ((r0 + CHUNK + LANES - 1) // LANES)
                rr = lax.broadcasted_iota(jnp.int32, (CHUNK, ncol), 0) + r0
                kc = lax.broadcasted_iota(jnp.int32, (CHUNK, ncol), 1)
                btc = bt_all[r0:r0 + CHUNK, 0:1]
                log_d = jnp.where(rr >= kc, btc - bj[:, :ncol] + ij[:, :ncol], NEG)
                inter_log = btc + mprev[:, 0:1]
                m_row = jnp.maximum(inter_log, jnp.max(log_d, axis=1, keepdims=True))
                sc = s_pair[r0:r0 + CHUNK, e * TILE:e * TILE + ncol] * jnp.exp2(log_d - m_row)
                sc_ref[p, r0:r0 + CHUNK, e * TILE:e * TILE + ncol] = sc.astype(BF16)
                if ncol < TILE:
                    sc_ref[p, r0:r0 + CHUNK, e * TILE + ncol:(e + 1) * TILE] = jnp.zeros(
                        (CHUNK, TILE - ncol), BF16)
                inter_w = jnp.exp2(inter_log - m_row)
                den = jnp.sum(sc, axis=1, keepdims=True) + inter_w * den_inter[r0:r0 + CHUNK]
                den_blocks.append(jnp.maximum(jnp.abs(den), jnp.exp2(-m_row)))
                iw_blocks.append(inter_w)
            iws.append(jnp.concatenate(iw_blocks, axis=0))
            dens.append(jnp.concatenate(den_blocks, axis=0))
            g_row = bt_all[TILE - 1:TILE, :]
            a_row = g_row[:, 0:1] - bj + ij
            m_new = jnp.maximum(g_row + mprev, jnp.max(a_row, axis=1, keepdims=True))
            wrows.append(jnp.exp2(a_row - m_new[:, 0:1]))
            solds.append(jnp.exp2(g_row + mprev - m_new))
            ml_m[h] = jnp.broadcast_to(m_new, (8, LANES))
        num = _dot(sc_ref[p], ml_vv[p]) + jnp.where(m0, iws[0], iws[1]) * ml_ni[p][:, :LANES]
        hcur = num / jnp.where(m0, dens[0], dens[1])
        y_ref[:, ATT_W + c0:ATT_W + c0 + LANES] = (
            jax.nn.sigmoid(zc[:, MO + c0:MO + c0 + LANES])
            * _pair_rms(hcur, mlg_ref[:, c0:c0 + LANES], m0)
            * _silu(zc[:, MG + c0:MG + c0 + LANES])).astype(BF16)
        kw_t = ml_kt[p] * jnp.where(frow, wrows[0], wrows[1])
        upd = _dot(kw_t.astype(BF16), ml_vb[p])
        sold_col = jnp.where(prow_first, solds[0][:, 0:1], solds[1][:, 0:1])
        ml_c[p] = jnp.where(pair_diag, sold_col * ml_ct[p] + upd, 0.0)
        ncol = sold_col * ml_nc[p] + jnp.sum(kw_t, axis=1, keepdims=True)
        ml_n[p] = jnp.broadcast_to(ncol, (LANES, LANES))
        ip(IP_PER_ML_PAIR)

    grow = lax.broadcasted_iota(jnp.int32, (LANES, GLA_VW), 0)
    gcol = lax.broadcasted_iota(jnp.int32, (LANES, GLA_VW), 1)
    gla_diag = (grow >> 5) == (gcol >> 6)
    khead = lane >> 5
    vhead = lax.broadcasted_iota(jnp.int32, (1, GLA_VW), 1) >> 6
    arow = lax.broadcasted_iota(jnp.int32, (CHUNK, GLA_VW), 0)
    acol = lax.broadcasted_iota(jnp.int32, (CHUNK, GLA_VW), 1)
    causal4 = arow >= (acol & (CHUNK - 1))
    n_gh = GLA_KW // GLA_DK
    yc0 = ATT_W + ML_W
    g_a, g_upd, g_qi, g_vvg, g_dec = [], [], [], [], []
    for c in range(NCH):
        r0 = c * CHUNK
        bc = bgl[r0:r0 + CHUNK, :]
        bref = bc[CHUNK // 2 - 1:CHUNK // 2, :]
        btot = bc[CHUNK - 1:CHUNK, :]
        gq = zc[r0:r0 + CHUNK, GQ:GQ + GLA_KW] * (GLA_DK ** -0.5)
        gk = zc[r0:r0 + CHUNK, GK:GK + GLA_KW]
        gv = zc[r0:r0 + CHUNK, GV:GV + GLA_VW]
        qe = (gq * jnp.exp(bc - bref)).astype(BF16)
        ke = gk * jnp.exp(bref - bc)
        kek = jnp.concatenate([jnp.where(khead == h, ke, 0.0) for h in range(n_gh)],
                              axis=0).astype(BF16)
        g_vvg.append(jnp.concatenate([jnp.where(vhead == h, gv, 0.0) for h in range(n_gh)],
                                     axis=0).astype(BF16))
        g_qi.append((gq * jnp.exp(bc)).astype(BF16))
        kd = (gk * jnp.exp(btot - bc)).astype(BF16)
        g_dec.append(jnp.exp(bgl_t[:, r0 + CHUNK - 1:r0 + CHUNK]))
        g_a.append(_dot_nt(qe, kek))
        g_upd.append(_dot_tn(kd, gv.astype(BF16)))
    ip(2)
    s_all = gla_s[...]
    for c in range(NCH):
        r0 = c * CHUNK
        a = jnp.where(causal4, g_a[c], 0.0)
        o = _dot(g_qi[c], s_all.astype(BF16)) + _dot(a.astype(BF16), g_vvg[c])
        for hp in range(GLA_VW // LANES):
            c0 = hp * LANES
            y_ref[r0:r0 + CHUNK, yc0 + c0:yc0 + c0 + LANES] = (
                _pair_rms(o[:, c0:c0 + LANES], glg_ref[:, c0:c0 + LANES], m0)
                * _silu(zc[r0:r0 + CHUNK, GG + c0:GG + c0 + LANES])).astype(BF16)
        s_all = jnp.where(gla_diag, g_dec[c] * s_all + g_upd[c], 0.0)
        ip(1)
    gla_s[...] = s_all

    ip(D_INP // IP_BLK)

    xo = xc_ref[...] + _dot(y_ref[...], wout_ref[...])
    if final_norm:
        xo = xo * lax.rsqrt(jnp.sum(xo * xo, axis=1, keepdims=True) * (1.0 / D_MODEL) + EPS) * fg_ref[...]
    o_ref[...] = xo


def _project(x, ng_ref, w_cols, z_out):
    hn = x * lax.rsqrt(jnp.sum(x * x, axis=1, keepdims=True) * (1.0 / D_MODEL) + EPS) * ng_ref[...]
    hn_b = hn.astype(BF16)
    for n0 in range(0, D_INP, IP_BLK):
        z_out[:, n0:n0 + IP_BLK] = _dot(hn_b, w_cols(n0))


def _layer_kernel(xa_ref, xn_ref, ng_ref, win_ref, wtl_ref, gb_ref, conv_ref, wal_ref, bal_ref, gp_ref,
                  mlg_ref, glg_ref, wout_ref, fg_ref,
                  o_ref,
                  z_a, z_b, kb, vb, bias_ref, cbuf, ml_c, ml_n, ml_m, gla_s, y_ref, sc_ref, wrm,
                  *, final_norm, tiles_per_seq):
    g = pl.program_id(0)
    i = lax.rem(2 * g, tiles_per_seq)

    def w_rows(n0):
        if n0 < W_MAIN:
            return win_ref[n0:n0 + IP_BLK, :]
        return wtl_ref[n0 - W_MAIN:n0 - W_MAIN + IP_BLK, :]

    def w_cols(n0):
        return wrm[:, n0:n0 + IP_BLK]

    @pl.when(g == 0)
    def _first_step():
        r = lax.broadcasted_iota(jnp.int32, (TILE, KEYS), 0)
        m = lax.broadcasted_iota(jnp.int32, (TILE, KEYS), 1)
        jj = m - ((r >> 6) << 6)
        in_band = (jj >= 0) & (jj < BAND)
        for h in range(ML_HEADS):
            row = jnp.broadcast_to(gp_ref[h:h + 1, :], (TILE, ROLL_W))
            rolled = pltpu.roll(row, KEYS, 1, stride=1, stride_axis=0)
            bias_ref[h] = jnp.where(in_band, rolled[:, :KEYS] * LOG2E, NEG)
        for n0 in range(0, D_INP, IP_BLK):
            wrm[:, n0:n0 + IP_BLK] = w_rows(n0).astype(F32).T.astype(BF16)
        _project(xa_ref[0:TILE, :], ng_ref, w_cols, z_a)

    @pl.when(i == 0)
    def _reset():
        kr = lax.broadcasted_iota(jnp.int32, (2 * ATT_W, 1), 0)
        pen_row = (kr & (LANES - 1)) == jnp.where(kr < ATT_W, PEN_LANE_A, PEN_LANE_B)
        pen_init = jnp.broadcast_to(jnp.where(pen_row, NEG, 0.0), kb.shape[1:]).astype(kb.dtype)
        for slot in range(2):
            kb[slot] = pen_init
            vb[slot] = jnp.zeros(vb.shape[1:], vb.dtype)
        cbuf[0:8, :] = jnp.zeros((8, cbuf.shape[1]), F32)
        ml_c[...] = jnp.zeros(ml_c.shape, F32)
        ml_n[...] = jnp.zeros(ml_n.shape, F32)
        ml_m[...] = jnp.zeros(ml_m.shape, F32)
        gla_s[...] = jnp.zeros(gla_s.shape, F32)

    step = functools.partial(
        _step, ng_ref=ng_ref, w_cols=w_cols, gb_ref=gb_ref,
        conv_ref=conv_ref, wal_ref=wal_ref, bal_ref=bal_ref, mlg_ref=mlg_ref, glg_ref=glg_ref,
        wout_ref=wout_ref, fg_ref=fg_ref, bias_ref=bias_ref,
        cbuf=cbuf, ml_c=ml_c, ml_n=ml_n, ml_m=ml_m, gla_s=gla_s, y_ref=y_ref, sc_ref=sc_ref,
        final_norm=final_norm)

    lo = pl.ds(0, TILE)
    hi = pl.ds(TILE, TILE)
    step(i, z_a, z_b, xn_ref=xa_ref.at[hi], xc_ref=xa_ref.at[lo], o_ref=o_ref.at[lo],
         kbuf=kb.at[0], vbuf=vb.at[0], kold=kb.at[1], vold=vb.at[1])
    step(i + 1, z_b, z_a, xn_ref=xn_ref, xc_ref=xa_ref.at[hi], o_ref=o_ref.at[hi],
         kbuf=kb.at[1], vbuf=vb.at[1], kold=kb.at[0], vold=vb.at[0])


def _layer_spec(layer, arr, rows=None):
    shape = arr.shape[1:] if rows is None else (rows,) + arr.shape[2:]
    nd = len(shape)
    return pl.BlockSpec((None,) + shape, lambda g, _l=layer, _nd=nd: (_l,) + (0,) * _nd,
                        pipeline_mode=pl.Buffered(1))


def _layer_call(x2d, layer, prm, final_norm, tiles_per_seq):
    n_tiles = x2d.shape[0] // TILE
    n_steps = n_tiles // 2
    names = ("ng", "win", "wtl", "gb", "conv", "wal", "bal", "gp", "mlg", "glg", "wout")
    args = (x2d, x2d) + tuple(prm[k] for k in names) + (prm["fg"],)
    cur_spec = pl.BlockSpec((2 * TILE, D_MODEL), lambda g: (g, 0))
    nxt_spec = pl.BlockSpec((TILE, D_MODEL), lambda g: (jnp.minimum(2 * g + 2, n_tiles - 1), 0))
    in_specs = [cur_spec, nxt_spec]
    in_specs += [_layer_spec(layer, prm[k], rows=W_MAIN if k == "win" else None) for k in names]
    in_specs += [pl.BlockSpec(prm["fg"].shape, lambda g: (0, 0), pipeline_mode=pl.Buffered(1))]
    scratch = [
        pltpu.VMEM((TILE, D_INP), F32),
        pltpu.VMEM((TILE, D_INP), F32),
        pltpu.VMEM((2, 2 * ATT_W, KEYS), BF16),
        pltpu.VMEM((2, KEYS, 2 * ATT_W), BF16),
        pltpu.VMEM((ML_HEADS, TILE, KEYS), F32),
        pltpu.VMEM((TILE + 16, 2 * ML_W), F32),
        pltpu.VMEM((N_PAIRS, LANES, LANES), F32),
        pltpu.VMEM((N_PAIRS, LANES, LANES), F32),
        pltpu.VMEM((ML_HEADS, 8, LANES), F32),
        pltpu.VMEM((LANES, GLA_VW), F32),
        pltpu.VMEM((TILE, D_MODEL), BF16),
        pltpu.VMEM((N_PAIRS, TILE, 2 * TILE), BF16),
        pltpu.VMEM((D_MODEL, D_INP), BF16),
    ]
    return pl.pallas_call(
        functools.partial(_layer_kernel, final_norm=final_norm, tiles_per_seq=tiles_per_seq),
        out_shape=jax.ShapeDtypeStruct(x2d.shape, x2d.dtype),
        grid=(n_steps,),
        in_specs=in_specs,
        out_specs=cur_spec,
        scratch_shapes=scratch,
        compiler_params=pltpu.CompilerParams(
            dimension_semantics=("arbitrary",),
            vmem_limit_bytes=VMEM_LIMIT),
        name="hybrid_layer_final" if final_norm else "hybrid_layer",
    )(*args)


def _prep_params(norm_g, w_in, b_gates, conv_w, w_alpha, b_alpha, rel_bias, ml_norm_g, gla_norm_g, w_out, final_g):
    depth = norm_g.shape[0]
    w_t = jnp.transpose(w_in, (0, 2, 1)).astype(BF16)
    pad = jnp.zeros((depth, D_INP - SM - GLA_RANK - 2 * ML_HEADS, D_MODEL), BF16)
    wtl = jnp.concatenate([w_t[:, 3084:3980], w_t[:, 3996:4252], w_t[:, 3980:3996],
                           w_t[:, 3072:3084], pad], axis=1)
    gb = jnp.pad(b_gates, ((0, 0), (SM_I, LANES - SM_I - 2 * ML_HEADS))).reshape(depth, 1, LANES)
    wal = jnp.pad(w_alpha, ((0, 0), (0, LANES - GLA_RANK), (0, 0))).astype(BF16)
    nh = rel_bias.shape[1]
    gp = jnp.concatenate([
        jnp.broadcast_to(rel_bias[:, :, 2 * REL_CLIP:], (depth, nh, KEYS - REL_CLIP + 1)),
        rel_bias[:, :, 2 * REL_CLIP - 1:0:-1],
        jnp.broadcast_to(rel_bias[:, :, :1], (depth, nh, ROLL_W - KEYS - REL_CLIP)),
    ], axis=2)
    return dict(ng=norm_g.reshape(depth, 1, D_MODEL), win=w_t, wtl=wtl, gb=gb, conv=conv_w, wal=wal,
                bal=b_alpha.reshape(depth, 1, GLA_KW), gp=gp, mlg=ml_norm_g.reshape(depth, 1, ML_W),
                glg=gla_norm_g.reshape(depth, 1, GLA_VW), wout=w_out.astype(BF16),
                fg=final_g.reshape(1, D_MODEL))


def kernel(x, norm_g, w_in, b_gates, conv_w, w_alpha, b_alpha, rel_bias, ml_norm_g, gla_norm_g, w_out, final_g):
    depth = norm_g.shape[0]
    bsz, seq, _ = x.shape
    prm = _prep_params(norm_g, w_in, b_gates, conv_w, w_alpha, b_alpha, rel_bias, ml_norm_g, gla_norm_g,
                       w_out, final_g)
    x2d = x.reshape(bsz * seq, D_MODEL)
    for l in range(depth):
        x2d = _layer_call(x2d, l, prm, final_norm=(l == depth - 1), tiles_per_seq=seq // TILE)
    return x2d.reshape(bsz, seq, D_MODEL)
```

```python
import functools

import jax
import jax.numpy as jnp
from jax import lax
from jax.experimental import pallas as pl
from jax.experimental.pallas import tpu as pltpu

F32 = jnp.float32
BF16 = jnp.bfloat16

D_MODEL = 1024
CHUNK = 64
HEAD_DIM = 64
ATT_W = 384
ML_W = 384
GLA_KW = 128
GLA_VW = 256
GLA_DK = 32
GLA_RANK = 16
GLA_TAU = 16.0
ML_HEADS = 6
PAST_CHUNKS = 8
REL_CLIP = 128
CONV_W = 4
EPS = 1e-6
NEG = -1e30
LOG2E = 1.4426950408889634

LANES = 128
SUBLANES = 8
TILE = 256
NCH = TILE // CHUNK
CHUNK_BITS = CHUNK.bit_length() - 1
GLA_DK_BITS = GLA_DK.bit_length() - 1
GLA_DV_BITS = (GLA_VW * GLA_DK // GLA_KW).bit_length() - 1
HIST = PAST_CHUNKS * CHUNK
KEYS = HIST + TILE
HALF_ROWS = TILE // 2
HALF_KEYS = HIST + HALF_ROWS
PEN_LANE_A = HEAD_DIM
PEN_LANE_B = 0
BAND = (PAST_CHUNKS + 1) * CHUNK
ROLL_W = 1024
N_PAIRS = ML_W // LANES

AQ, AK, AV, AG = 0, 384, 768, 1152
MQ, MK, MV, MO, MG = 1536, 1920, 2304, 2688, 3072
GQ, GK, GV, GG = 3456, 3584, 3712, 3968
SM = 4224
D_INP = 4352
IP_BLK = 256
W_MAIN = 3072
REF_MI, REF_MG, REF_GA, REF_GG, D_IN = 3072, 3084, 3980, 3996, 4252
IP_PER_HEAD, IP_AFTER_ATT, IP_PER_ML_PAIR = 1, 1, 1
SM_I = GLA_RANK
SM_F = GLA_RANK + ML_HEADS

VMEM_LIMIT = 56 * 1024 * 1024


def _log_sigmoid(x):
    return jnp.minimum(x, 0.0) - jnp.log1p(jnp.exp(-jnp.abs(x)))


def _silu(x):
    return x * jax.nn.sigmoid(x)


def _dot_nt(a, b):
    return lax.dot_general(a, b, (((1,), (1,)), ((), ())), preferred_element_type=F32)


def _dot_tn(a, b):
    return lax.dot_general(a, b, (((0,), (0,)), ((), ())), preferred_element_type=F32)


def _dot(a, b):
    return jnp.dot(a, b, preferred_element_type=F32)


def _pair_rms(hcur, gain, m0):
    sq = hcur * hcur
    ms0 = jnp.sum(jnp.where(m0, sq, 0.0), axis=1, keepdims=True)
    ms1 = jnp.sum(jnp.where(m0, 0.0, sq), axis=1, keepdims=True)
    ms = jnp.where(m0, ms0, ms1) * (1.0 / HEAD_DIM)
    return hcur * lax.rsqrt(ms + EPS) * gain


def _step(zc, zn, xn_ref, xc_ref, ng_ref, w_cols, gb_ref, conv_ref, wal_ref, bal_ref,
          mlg_ref, glg_ref, wout_ref, fg_ref, o_ref,
          kbuf, vbuf, kold, vold, bias_ref, cbuf, ml_c, ml_n, ml_m, gla_s, y_ref, sc_ref, *, final_norm):
    lane = lax.broadcasted_iota(jnp.int32, (1, LANES), 1)
    m0 = lane < HEAD_DIM
    m1 = jnp.logical_not(m0)

    xn = xn_ref[...]
    hn = xn * lax.rsqrt(jnp.sum(xn * xn, axis=1, keepdims=True) * (1.0 / D_MODEL) + EPS) * ng_ref[...]
    hn_b = hn.astype(BF16)
    ip_next = [0]

    def ip(count):
        for _ in range(count):
            n0 = ip_next[0] * IP_BLK
            if n0 < D_INP:
                zn[:, n0:n0 + IP_BLK] = _dot(hn_b, w_cols(n0))
                ip_next[0] += 1

    kbuf[:, 0:HIST] = kold[:, TILE:KEYS]
    vbuf[0:HIST, :] = vold[TILE:KEYS, :]
    frow = lax.broadcasted_iota(jnp.int32, (LANES, 1), 0) < HEAD_DIM
    for p in range(N_PAIRS):
        c0 = p * LANES
        kn_t = zc[:, AK + c0:AK + c0 + LANES].T
        vn = zc[:, AV + c0:AV + c0 + LANES]
        kbuf[c0:c0 + LANES, HIST:KEYS] = jnp.where(frow, kn_t, 0.0).astype(BF16)
        kbuf[ATT_W + c0:ATT_W + c0 + LANES, HIST:KEYS] = jnp.where(frow, 0.0, kn_t).astype(BF16)
        vbuf[HIST:KEYS, c0:c0 + LANES] = jnp.where(m0, vn, 1.0).astype(BF16)
        vbuf[HIST:KEYS, ATT_W + c0:ATT_W + c0 + LANES] = jnp.where(m0, 1.0, vn).astype(BF16)

    def att_scores(u):
        h, half = u // 2, u % 2
        c0 = (h // 2) * LANES
        cb = (h % 2) * ATT_W + c0
        r0, k0 = half * HALF_ROWS, half * HALF_ROWS
        q2 = zc[r0:r0 + HALF_ROWS, AQ + c0:AQ + c0 + LANES] * (HEAD_DIM ** -0.5 * LOG2E)
        if h % 2 == 0:
            qm = jnp.where(m0, q2, jnp.where(lane == PEN_LANE_A, 1.0, 0.0))
        else:
            qm = jnp.where(m1, q2, jnp.where(lane == PEN_LANE_B, 1.0, 0.0))
        return (_dot(qm.astype(BF16), kbuf[cb:cb + LANES, k0:k0 + HALF_KEYS])
                + bias_ref[h, r0:r0 + HALF_ROWS, k0:k0 + HALF_KEYS])

    def att_out(u, s):
        h, half = u // 2, u % 2
        cb = (h % 2) * ATT_W + (h // 2) * LANES
        k0 = half * HALF_ROWS
        pe = jnp.exp2(s - jnp.max(s, axis=1, keepdims=True))
        return _dot(pe.astype(BF16), vbuf[k0:k0 + HALF_KEYS, cb:cb + LANES])

    n_units = 2 * ML_HEADS
    s_cur = att_scores(0)
    outs = []
    for u in range(n_units):
        s_nxt = att_scores(u + 1) if u + 1 < n_units else None
        if u % 2 == 1:
            ip(IP_PER_HEAD)
        outs.append(att_out(u, s_cur))
        s_cur = s_nxt
        if u % 4 == 3:
            c0 = (u // 4) * LANES
            r_a = jnp.concatenate(outs[u - 3:u - 1], axis=0)
            r_b = jnp.concatenate(outs[u - 1:u + 1], axis=0)
            den = pltpu.roll(jnp.where(m0, r_b, r_a), HEAD_DIM, 1)
            att = jnp.where(m0, r_a, r_b) / den
            y_ref[:, c0:c0 + LANES] = att * _silu(zc[:, AG + c0:AG + c0 + LANES])
    ip(IP_AFTER_ATT)

    cbuf[SUBLANES:SUBLANES + TILE, :] = zc[:, MQ:MQ + 2 * ML_W]
    for c0 in range(0, 2 * ML_W, LANES):
        acc = cbuf[SUBLANES:SUBLANES + TILE, c0:c0 + LANES] * conv_ref[CONV_W - 1:CONV_W, c0:c0 + LANES]
        for sft in range(1, CONV_W):
            acc = acc + cbuf[SUBLANES - sft:SUBLANES - sft + TILE, c0:c0 + LANES] * \
                conv_ref[CONV_W - 1 - sft:CONV_W - sft, c0:c0 + LANES]
        act = _silu(acc)
        if c0 >= ML_W:
            act = act * (HEAD_DIM ** -0.5)
        zc[:, MQ + c0:MQ + c0 + LANES] = act
    cbuf[0:SUBLANES, :] = cbuf[TILE:TILE + SUBLANES, :]

    small = zc[:, SM:SM + LANES]
    pre = small + gb_ref[...]
    lf = _log_sigmoid(pre)
    la = _log_sigmoid(_dot(small.astype(BF16), wal_ref[...]) + bal_ref[...]) * (1.0 / GLA_TAU)
    tr = lax.broadcasted_iota(jnp.int32, (TILE, TILE), 0)
    tc = lax.broadcasted_iota(jnp.int32, (TILE, TILE), 1)
    tri = jnp.where(tr >= tc, 1.0, 0.0).astype(BF16)
    terms = []
    for v in (lf, la):
        hi = v.astype(BF16)
        r1 = v - hi.astype(F32)
        mid = r1.astype(BF16)
        terms += [hi, mid, (r1 - mid.astype(F32)).astype(BF16)]
    cs = _dot(tri, jnp.concatenate(terms, axis=1))
    bml = (cs[:, 2 * LANES:3 * LANES] + cs[:, LANES:2 * LANES]) + cs[:, 0:LANES]
    bla = (cs[:, 5 * LANES:6 * LANES] + cs[:, 4 * LANES:5 * LANES]) + cs[:, 3 * LANES:4 * LANES]
    bgl = jnp.concatenate(
        [bla[0:CHUNK]] + [bla[c * CHUNK:(c + 1) * CHUNK] - bla[c * CHUNK - 1:c * CHUNK]
                          for c in range(1, NCH)], axis=0)
    pre_t = pre.T
    bml_t = bml.T
    bgl_t = bgl.T
    ip(1)

    prow = lax.broadcasted_iota(jnp.int32, (LANES, LANES), 0)
    pcol = lax.broadcasted_iota(jnp.int32, (LANES, LANES), 1)
    pair_diag = (prow < HEAD_DIM) == (pcol < HEAD_DIM)
    prow_first = prow[:, 0:1] < HEAD_DIM

    ml_kt, ml_vb, ml_vv, ml_s, ml_ni, ml_ct, ml_nc = [], [], [], [], [], [], []
    for p in range(N_PAIRS):
        c0 = p * LANES
        q2 = zc[:, MQ + c0:MQ + c0 + LANES]
        k2 = zc[:, MK + c0:MK + c0 + LANES]
        v2 = zc[:, MV + c0:MV + c0 + LANES]
        q2b = q2.astype(BF16)
        k2_t = k2.T
        kk_t = jnp.concatenate([jnp.where(frow, k2_t, 0.0), jnp.where(frow, 0.0, k2_t)],
                               axis=1).astype(BF16)
        ct = ml_c[p]
        ncol = ml_n[p][:, 0:1]
        n_ext = jnp.where(lane == 0, jnp.where(frow, ncol, 0.0),
                          jnp.where(lane == 1, jnp.where(frow, 0.0, ncol), 0.0))
        ml_kt.append(k2_t)
        ml_nc.append(ncol)
        ml_vb.append(v2.astype(BF16))
        ml_vv.append(jnp.concatenate([jnp.where(m0, v2, 0.0), jnp.where(m0, 0.0, v2)],
                                     axis=0).astype(BF16))
        ml_ct.append(ct)
        ml_s.append(_dot(q2b, kk_t))
        ml_ni.append(_dot(q2b, jnp.concatenate([ct, n_ext], axis=1).astype(BF16)))
    ip(2)

    for p in range(N_PAIRS):
        c0 = p * LANES
        s_pair = ml_s[p]
        iws, dens, wrows, solds = [], [], [], []
        for e in range(2):
            h = 2 * p + e
            mprev = ml_m[h][0:1, :]
            bt_all = jnp.broadcast_to(bml[:, SM_F + h:SM_F + h + 1], (TILE, LANES))
            bj = bml_t[SM_F + h:SM_F + h + 1, :]
            ij = pre_t[SM_I + h:SM_I + h + 1, :]
            den_inter = ml_ni[p][:, LANES + e:LANES + e + 1]
            iw_blocks, den_blocks = [], []
            for rb in range(NCH):
                r0 = rb * CHUNK
                ncol = LANES * ((r0 + CHUNK + LANES - 1) // LANES)
                rr = lax.broadcasted_iota(jnp.int32, (CHUNK, ncol), 0) + r0
                kc = lax.broadcasted_iota(jnp.int32, (CHUNK, ncol), 1)
                btc = bt_all[r0:r0 + CHUNK, 0:1]
                log_d = jnp.where(rr >= kc, btc - bj[:, :ncol] + ij[:, :ncol], NEG)
                inter_log = btc + mprev[:, 0:1]
                m_row = jnp.maximum(inter_log, jnp.max(log_d, axis=1, keepdims=True))
                sc = s_pair[r0:r0 + CHUNK, e * TILE:e * TILE + ncol] * jnp.exp(log_d - m_row)
                sc_ref[p, r0:r0 + CHUNK, e * TILE:e * TILE + ncol] = sc.astype(BF16)
                if ncol < TILE:
                    sc_ref[p, r0:r0 + CHUNK, e * TILE + ncol:(e + 1) * TILE] = jnp.zeros(
                        (CHUNK, TILE - ncol), BF16)
                inter_w = jnp.exp(inter_log - m_row)
                den = jnp.sum(sc, axis=1, keepdims=True) + inter_w * den_inter[r0:r0 + CHUNK]
                den_blocks.append(jnp.maximum(jnp.abs(den), jnp.exp(-m_row)))
                iw_blocks.append(inter_w)
            iws.append(jnp.concatenate(iw_blocks, axis=0))
            dens.append(jnp.concatenate(den_blocks, axis=0))
            g_row = bt_all[TILE - 1:TILE, :]
            a_row = g_row[:, 0:1] - bj + ij
            m_new = jnp.maximum(g_row + mprev, jnp.max(a_row, axis=1, keepdims=True))
            wrows.append(jnp.exp(a_row - m_new[:, 0:1]))
            solds.append(jnp.exp(g_row + mprev - m_new))
            ml_m[h] = jnp.broadcast_to(m_new, (SUBLANES, LANES))
        num = _dot(sc_ref[p], ml_vv[p]) + jnp.where(m0, iws[0], iws[1]) * ml_ni[p][:, :LANES]
        hcur = num / jnp.where(m0, dens[0], dens[1])
        y_ref[:, ATT_W + c0:ATT_W + c0 + LANES] = (
            jax.nn.sigmoid(zc[:, MO + c0:MO + c0 + LANES])
            * _pair_rms(hcur, mlg_ref[:, c0:c0 + LANES], m0)
            * _silu(zc[:, MG + c0:MG + c0 + LANES]))
        kw_t = ml_kt[p] * jnp.where(frow, wrows[0], wrows[1])
        upd = _dot(kw_t.astype(BF16), ml_vb[p])
        sold_col = jnp.where(prow_first, solds[0][:, 0:1], solds[1][:, 0:1])
        ml_c[p] = jnp.where(pair_diag, sold_col * ml_ct[p] + upd, 0.0)
        ncol = sold_col * ml_nc[p] + jnp.sum(kw_t, axis=1, keepdims=True)
        ml_n[p] = jnp.broadcast_to(ncol, (LANES, LANES))
        ip(IP_PER_ML_PAIR)

    grow = lax.broadcasted_iota(jnp.int32, (LANES, GLA_VW), 0)
    gcol = lax.broadcasted_iota(jnp.int32, (LANES, GLA_VW), 1)
    gla_diag = (grow >> GLA_DK_BITS) == (gcol >> GLA_DV_BITS)
    khead = lane >> GLA_DK_BITS
    vhead = lax.broadcasted_iota(jnp.int32, (1, GLA_VW), 1) >> GLA_DV_BITS
    arow = lax.broadcasted_iota(jnp.int32, (CHUNK, GLA_VW), 0)
    acol = lax.broadcasted_iota(jnp.int32, (CHUNK, GLA_VW), 1)
    causal4 = arow >= (acol & (CHUNK - 1))
    n_gh = GLA_KW // GLA_DK
    yc0 = ATT_W + ML_W
    g_a, g_upd, g_qi, g_vvg, g_dec = [], [], [], [], []
    for c in range(NCH):
        r0 = c * CHUNK
        bc = bgl[r0:r0 + CHUNK, :]
        bref = bc[CHUNK // 2 - 1:CHUNK // 2, :]
        btot = bc[CHUNK - 1:CHUNK, :]
        gq = zc[r0:r0 + CHUNK, GQ:GQ + GLA_KW] * (GLA_DK ** -0.5)
        gk = zc[r0:r0 + CHUNK, GK:GK + GLA_KW]
        gv = zc[r0:r0 + CHUNK, GV:GV + GLA_VW]
        qe = (gq * jnp.exp(bc - bref)).astype(BF16)
        ke = gk * jnp.exp(bref - bc)
        kek = jnp.concatenate([jnp.where(khead == h, ke, 0.0) for h in range(n_gh)],
                              axis=0).astype(BF16)
        g_vvg.append(jnp.concatenate([jnp.where(vhead == h, gv, 0.0) for h in range(n_gh)],
                                     axis=0).astype(BF16))
        g_qi.append((gq * jnp.exp(bc)).astype(BF16))
        kd = (gk * jnp.exp(btot - bc)).astype(BF16)
        g_dec.append(jnp.exp(bgl_t[:, r0 + CHUNK - 1:r0 + CHUNK]))
        g_a.append(_dot_nt(qe, kek))
        g_upd.append(_dot_tn(kd, gv.astype(BF16)))
    ip(2)
    s_all = gla_s[...]
    for c in range(NCH):
        r0 = c * CHUNK
        a = jnp.where(causal4, g_a[c], 0.0)
        o = _dot(g_qi[c], s_all.astype(BF16)) + _dot(a.astype(BF16), g_vvg[c])
        y_ref[r0:r0 + CHUNK, yc0:yc0 + GLA_VW] = o
        s_all = jnp.where(gla_diag, g_dec[c] * s_all + g_upd[c], 0.0)
        ip(1)
    gla_s[...] = s_all

    for hp in range(GLA_VW // LANES):
        c0 = hp * LANES
        hcur = y_ref[:, yc0 + c0:yc0 + c0 + LANES]
        y_ref[:, yc0 + c0:yc0 + c0 + LANES] = (
            _pair_rms(hcur, glg_ref[:, c0:c0 + LANES], m0) * _silu(zc[:, GG + c0:GG + c0 + LANES]))
    ip(D_INP // IP_BLK)

    xo = xc_ref[...] + _dot(y_ref[...].astype(BF16), wout_ref[...])
    if final_norm:
        xo = xo * lax.rsqrt(jnp.sum(xo * xo, axis=1, keepdims=True) * (1.0 / D_MODEL) + EPS) * fg_ref[...]
    o_ref[...] = xo


def _project(x, ng_ref, w_cols, z_out):
    hn = x * lax.rsqrt(jnp.sum(x * x, axis=1, keepdims=True) * (1.0 / D_MODEL) + EPS) * ng_ref[...]
    hn_b = hn.astype(BF16)
    for n0 in range(0, D_INP, IP_BLK):
        z_out[:, n0:n0 + IP_BLK] = _dot(hn_b, w_cols(n0))


def _layer_kernel(xa_ref, xn_ref, ng_ref, win_ref, wtl_ref, gb_ref, conv_ref, wal_ref, bal_ref, gp_ref,
                  mlg_ref, glg_ref, wout_ref, fg_ref,
                  o_ref,
                  z_a, z_b, kb, vb, bias_ref, cbuf, ml_c, ml_n, ml_m, gla_s, y_ref, sc_ref, wrm,
                  *, final_norm, tiles_per_seq):
    g = pl.program_id(0)
    i = lax.rem(2 * g, tiles_per_seq)

    def w_rows(n0):
        if n0 < W_MAIN:
            return win_ref[n0:n0 + IP_BLK, :]
        return wtl_ref[n0 - W_MAIN:n0 - W_MAIN + IP_BLK, :]

    def w_cols(n0):
        return wrm[:, n0:n0 + IP_BLK]

    @pl.when(g == 0)
    def _first_step():
        r = lax.broadcasted_iota(jnp.int32, (TILE, KEYS), 0)
        m = lax.broadcasted_iota(jnp.int32, (TILE, KEYS), 1)
        jj = m - ((r >> CHUNK_BITS) << CHUNK_BITS)
        in_band = (jj >= 0) & (jj < BAND)
        for h in range(ML_HEADS):
            row = jnp.broadcast_to(gp_ref[h:h + 1, :], (TILE, ROLL_W))
            rolled = pltpu.roll(row, KEYS, 1, stride=1, stride_axis=0)
            bias_ref[h] = jnp.where(in_band, rolled[:, :KEYS] * LOG2E, NEG)
        for n0 in range(0, D_INP, IP_BLK):
            wrm[:, n0:n0 + IP_BLK] = w_rows(n0).astype(F32).T.astype(BF16)
        _project(xa_ref[0:TILE, :], ng_ref, w_cols, z_a)

    @pl.when(i == 0)
    def _reset():
        kr = lax.broadcasted_iota(jnp.int32, (2 * ATT_W, 1), 0)
        pen_row = (kr & (LANES - 1)) == jnp.where(kr < ATT_W, PEN_LANE_A, PEN_LANE_B)
        pen_init = jnp.broadcast_to(jnp.where(pen_row, NEG, 0.0), kb.shape[1:]).astype(kb.dtype)
        for slot in range(2):
            kb[slot] = pen_init
            vb[slot] = jnp.zeros(vb.shape[1:], vb.dtype)
        cbuf[0:SUBLANES, :] = jnp.zeros((SUBLANES, cbuf.shape[1]), F32)
        ml_c[...] = jnp.zeros(ml_c.shape, F32)
        ml_n[...] = jnp.zeros(ml_n.shape, F32)
        ml_m[...] = jnp.zeros(ml_m.shape, F32)
        gla_s[...] = jnp.zeros(gla_s.shape, F32)

    step = functools.partial(
        _step, ng_ref=ng_ref, w_cols=w_cols, gb_ref=gb_ref,
        conv_ref=conv_ref, wal_ref=wal_ref, bal_ref=bal_ref, mlg_ref=mlg_ref, glg_ref=glg_ref,
        wout_ref=wout_ref, fg_ref=fg_ref, bias_ref=bias_ref,
        cbuf=cbuf, ml_c=ml_c, ml_n=ml_n, ml_m=ml_m, gla_s=gla_s, y_ref=y_ref, sc_ref=sc_ref,
        final_norm=final_norm)

    lo = pl.ds(0, TILE)
    hi = pl.ds(TILE, TILE)
    step(z_a, z_b, xn_ref=xa_ref.at[hi], xc_ref=xa_ref.at[lo], o_ref=o_ref.at[lo],
         kbuf=kb.at[0], vbuf=vb.at[0], kold=kb.at[1], vold=vb.at[1])
    step(z_b, z_a, xn_ref=xn_ref, xc_ref=xa_ref.at[hi], o_ref=o_ref.at[hi],
         kbuf=kb.at[1], vbuf=vb.at[1], kold=kb.at[0], vold=vb.at[0])


def _layer_spec(layer, arr, rows=None):
    shape = arr.shape[1:] if rows is None else (rows,) + arr.shape[2:]
    nd = len(shape)
    return pl.BlockSpec((None,) + shape, lambda g, _l=layer, _nd=nd: (_l,) + (0,) * _nd,
                        pipeline_mode=pl.Buffered(1))


def _layer_call(x2d, layer, prm, final_norm, tiles_per_seq):
    n_tiles = x2d.shape[0] // TILE
    n_steps = n_tiles // 2
    names = ("ng", "win", "wtl", "gb", "conv", "wal", "bal", "gp", "mlg", "glg", "wout")
    args = (x2d, x2d) + tuple(prm[k] for k in names) + (prm["fg"],)
    cur_spec = pl.BlockSpec((2 * TILE, D_MODEL), lambda g: (g, 0))
    nxt_spec = pl.BlockSpec((TILE, D_MODEL), lambda g: (jnp.minimum(2 * g + 2, n_tiles - 1), 0))
    in_specs = [cur_spec, nxt_spec]
    in_specs += [_layer_spec(layer, prm[k], rows=W_MAIN if k == "win" else None) for k in names]
    in_specs += [pl.BlockSpec(prm["fg"].shape, lambda g: (0, 0), pipeline_mode=pl.Buffered(1))]
    scratch = [
        pltpu.VMEM((TILE, D_INP), F32),
        pltpu.VMEM((TILE, D_INP), F32),
        pltpu.VMEM((2, 2 * ATT_W, KEYS), BF16),
        pltpu.VMEM((2, KEYS, 2 * ATT_W), BF16),
        pltpu.VMEM((ML_HEADS, TILE, KEYS), F32),
        pltpu.VMEM((TILE + 2 * SUBLANES, 2 * ML_W), F32),
        pltpu.VMEM((N_PAIRS, LANES, LANES), F32),
        pltpu.VMEM((N_PAIRS, LANES, LANES), F32),
        pltpu.VMEM((ML_HEADS, SUBLANES, LANES), F32),
        pltpu.VMEM((LANES, GLA_VW), F32),
        pltpu.VMEM((TILE, D_MODEL), F32),
        pltpu.VMEM((N_PAIRS, TILE, 2 * TILE), BF16),
        pltpu.VMEM((D_MODEL, D_INP), BF16),
    ]
    return pl.pallas_call(
        functools.partial(_layer_kernel, final_norm=final_norm, tiles_per_seq=tiles_per_seq),
        out_shape=jax.ShapeDtypeStruct(x2d.shape, x2d.dtype),
        grid=(n_steps,),
        in_specs=in_specs,
        out_specs=cur_spec,
        scratch_shapes=scratch,
        compiler_params=pltpu.CompilerParams(
            dimension_semantics=("arbitrary",),
            vmem_limit_bytes=VMEM_LIMIT),
        name="hybrid_layer_final" if final_norm else "hybrid_layer",
    )(*args)


def _prep_params(norm_g, w_in, b_gates, conv_w, w_alpha, b_alpha, rel_bias, ml_norm_g, gla_norm_g, w_out, final_g):
    depth = norm_g.shape[0]
    w_t = jnp.transpose(w_in, (0, 2, 1)).astype(BF16)
    pad = jnp.zeros((depth, D_INP - SM - GLA_RANK - 2 * ML_HEADS, D_MODEL), BF16)
    wtl = jnp.concatenate([w_t[:, REF_MG:REF_GA], w_t[:, REF_GG:D_IN], w_t[:, REF_GA:REF_GG],
                           w_t[:, REF_MI:REF_MG], pad], axis=1)
    gb = jnp.pad(b_gates, ((0, 0), (SM_I, LANES - SM_I - 2 * ML_HEADS))).reshape(depth, 1, LANES)
    wal = jnp.pad(w_alpha, ((0, 0), (0, LANES - GLA_RANK), (0, 0))).astype(BF16)
    nh = rel_bias.shape[1]
    gp = jnp.concatenate([
        jnp.broadcast_to(rel_bias[:, :, 2 * REL_CLIP:], (depth, nh, KEYS - REL_CLIP + 1)),
        rel_bias[:, :, 2 * REL_CLIP - 1:0:-1],
        jnp.broadcast_to(rel_bias[:, :, :1], (depth, nh, ROLL_W - KEYS - REL_CLIP)),
    ], axis=2)
    return dict(ng=norm_g.reshape(depth, 1, D_MODEL), win=w_t, wtl=wtl, gb=gb, conv=conv_w, wal=wal,
                bal=b_alpha.reshape(depth, 1, GLA_KW), gp=gp, mlg=ml_norm_g.reshape(depth, 1, ML_W),
                glg=gla_norm_g.reshape(depth, 1, GLA_VW), wout=w_out.astype(BF16),
                fg=final_g.reshape(1, D_MODEL))


def kernel(x, norm_g, w_in, b_gates, conv_w, w_alpha, b_alpha, rel_bias, ml_norm_g, gla_norm_g, w_out, final_g):
    depth = norm_g.shape[0]
    bsz, seq, _ = x.shape
    prm = _prep_params(norm_g, w_in, b_gates, conv_w, w_alpha, b_alpha, rel_bias, ml_norm_g, gla_norm_g,
                       w_out, final_g)
    x2d = x.reshape(bsz * seq, D_MODEL)
    for l in range(depth):
        x2d = _layer_call(x2d, l, prm, final_norm=(l == depth - 1), tiles_per_seq=seq // TILE)
    return x2d.reshape(bsz, seq, D_MODEL)
```

```python
import functools

import jax
import jax.numpy as jnp
from jax import lax
from jax.experimental import pallas as pl
from jax.experimental.pallas import tpu as pltpu

F32 = jnp.float32
BF16 = jnp.bfloat16

D_MODEL = 1024
CHUNK = 64
HEAD_DIM = 64
ATT_W = 384
ML_W = 384
GLA_KW = 128
GLA_VW = 256
GLA_DK = 32
GLA_RANK = 16
GLA_TAU = 16.0
ML_HEADS = 6
PAST_CHUNKS = 8
REL_CLIP = 128
CONV_W = 4
EPS = 1e-6
NEG = -1e30
LOG2E = 1.4426950408889634

LANES = 128
SUBLANES = 8
TILE = 256
NCH = TILE // CHUNK
CHUNK_BITS = CHUNK.bit_length() - 1
GLA_DK_BITS = GLA_DK.bit_length() - 1
GLA_DV_BITS = (GLA_VW * GLA_DK // GLA_KW).bit_length() - 1
HIST = PAST_CHUNKS * CHUNK
KEYS = HIST + TILE
N_HALVES = 1
HALF_ROWS = TILE // N_HALVES
HALF_KEYS = HIST + HALF_ROWS
PEN_LANE_A = HEAD_DIM
PEN_LANE_B = 0
BAND = (PAST_CHUNKS + 1) * CHUNK
ROLL_W = 1024
N_PAIRS = ML_W // LANES

AQ, AK, AV, AG = 0, 384, 768, 1152
MQ, MK, MV, MO, MG = 1536, 1920, 2304, 2688, 3072
GQ, GK, GV, GG = 3456, 3584, 3712, 3968
SM = 4224
D_INP = 4352
IP_BLK = 256
W_MAIN = 3072
REF_MI, REF_MG, REF_GA, REF_GG, D_IN = 3072, 3084, 3980, 3996, 4252
IP_PER_HEAD, IP_AFTER_ATT, IP_PER_ML_PAIR = 1, 1, 1
SM_I = GLA_RANK
SM_F = GLA_RANK + ML_HEADS

VMEM_LIMIT = 56 * 1024 * 1024


def _log_sigmoid(x):
    return jnp.minimum(x, 0.0) - jnp.log1p(jnp.exp(-jnp.abs(x)))


def _silu(x):
    return x * jax.nn.sigmoid(x)


def _dot_nt(a, b):
    return lax.dot_general(a, b, (((1,), (1,)), ((), ())), preferred_element_type=F32)


def _dot_tn(a, b):
    return lax.dot_general(a, b, (((0,), (0,)), ((), ())), preferred_element_type=F32)


def _dot(a, b):
    return jnp.dot(a, b, preferred_element_type=F32)


def _pair_rms(hcur, gain, m0):
    sq = hcur * hcur
    ms0 = jnp.sum(jnp.where(m0, sq, 0.0), axis=1, keepdims=True)
    ms1 = jnp.sum(jnp.where(m0, 0.0, sq), axis=1, keepdims=True)
    ms = jnp.where(m0, ms0, ms1) * (1.0 / HEAD_DIM)
    return hcur * lax.rsqrt(ms + EPS) * gain


def _step(zc, zn, xn_ref, xc_ref, ng_ref, w_cols, gb_ref, conv_ref, wal_ref, bal_ref,
          mlg_ref, glg_ref, wout_ref, fg_ref, o_ref,
          kbuf, vbuf, kold, vold, bias_ref, cbuf, ml_c, ml_n, ml_m, gla_s, y_ref, sc_ref, *, final_norm):
    lane = lax.broadcasted_iota(jnp.int32, (1, LANES), 1)
    m0 = lane < HEAD_DIM
    m1 = jnp.logical_not(m0)

    xn = xn_ref[...]
    hn = xn * lax.rsqrt(jnp.sum(xn * xn, axis=1, keepdims=True) * (1.0 / D_MODEL) + EPS) * ng_ref[...]
    hn_b = hn.astype(BF16)
    ip_next = [0]

    def ip(count):
        for _ in range(count):
            n0 = ip_next[0] * IP_BLK
            if n0 < D_INP:
                zn[:, n0:n0 + IP_BLK] = _dot(hn_b, w_cols(n0))
                ip_next[0] += 1

    kbuf[:, 0:HIST] = kold[:, TILE:KEYS]
    vbuf[0:HIST, :] = vold[TILE:KEYS, :]
    frow = lax.broadcasted_iota(jnp.int32, (LANES, 1), 0) < HEAD_DIM
    for p in range(N_PAIRS):
        c0 = p * LANES
        kn_t = zc[:, AK + c0:AK + c0 + LANES].T
        vn = zc[:, AV + c0:AV + c0 + LANES]
        kbuf[c0:c0 + LANES, HIST:KEYS] = jnp.where(frow, kn_t, 0.0).astype(BF16)
        kbuf[ATT_W + c0:ATT_W + c0 + LANES, HIST:KEYS] = jnp.where(frow, 0.0, kn_t).astype(BF16)
        vbuf[HIST:KEYS, c0:c0 + LANES] = jnp.where(m0, vn, 1.0).astype(BF16)
        vbuf[HIST:KEYS, ATT_W + c0:ATT_W + c0 + LANES] = jnp.where(m0, 1.0, vn).astype(BF16)

    def att_scores(u):
        h, half = u // N_HALVES, u % N_HALVES
        c0 = (h // 2) * LANES
        cb = (h % 2) * ATT_W + c0
        r0, k0 = half * HALF_ROWS, half * HALF_ROWS
        q2 = zc[r0:r0 + HALF_ROWS, AQ + c0:AQ + c0 + LANES] * (HEAD_DIM ** -0.5 * LOG2E)
        if h % 2 == 0:
            qm = jnp.where(m0, q2, jnp.where(lane == PEN_LANE_A, 1.0, 0.0))
        else:
            qm = jnp.where(m1, q2, jnp.where(lane == PEN_LANE_B, 1.0, 0.0))
        return (_dot(qm.astype(BF16), kbuf[cb:cb + LANES, k0:k0 + HALF_KEYS])
                + bias_ref[h, r0:r0 + HALF_ROWS, k0:k0 + HALF_KEYS])

    def att_out(u, s):
        h, half = u // N_HALVES, u % N_HALVES
        cb = (h % 2) * ATT_W + (h // 2) * LANES
        k0 = half * HALF_ROWS
        pe = jnp.exp2(s - jnp.max(s, axis=1, keepdims=True))
        return _dot(pe.astype(BF16), vbuf[k0:k0 + HALF_KEYS, cb:cb + LANES])

    n_units = N_HALVES * ML_HEADS
    s_cur = att_scores(0)
    outs = []
    for u in range(n_units):
        s_nxt = att_scores(u + 1) if u + 1 < n_units else None
        if u % N_HALVES == N_HALVES - 1:
            ip(IP_PER_HEAD)
        outs.append(att_out(u, s_cur))
        s_cur = s_nxt
        if u % (2 * N_HALVES) == 2 * N_HALVES - 1:
            c0 = (u // (2 * N_HALVES)) * LANES
            r_a = jnp.concatenate(outs[u - 2 * N_HALVES + 1:u - N_HALVES + 1], axis=0)
            r_b = jnp.concatenate(outs[u - N_HALVES + 1:u + 1], axis=0)
            den = pltpu.roll(jnp.where(m0, r_b, r_a), HEAD_DIM, 1)
            att = jnp.where(m0, r_a, r_b) / den
            y_ref[:, c0:c0 + LANES] = att * _silu(zc[:, AG + c0:AG + c0 + LANES])
    ip(IP_AFTER_ATT)

    cbuf[SUBLANES:SUBLANES + TILE, :] = zc[:, MQ:MQ + 2 * ML_W]
    for c0 in range(0, 2 * ML_W, LANES):
        acc = cbuf[SUBLANES:SUBLANES + TILE, c0:c0 + LANES] * conv_ref[CONV_W - 1:CONV_W, c0:c0 + LANES]
        for sft in range(1, CONV_W):
            acc = acc + cbuf[SUBLANES - sft:SUBLANES - sft + TILE, c0:c0 + LANES] * \
                conv_ref[CONV_W - 1 - sft:CONV_W - sft, c0:c0 + LANES]
        act = _silu(acc)
        if c0 >= ML_W:
            act = act * (HEAD_DIM ** -0.5)
        zc[:, MQ + c0:MQ + c0 + LANES] = act
    cbuf[0:SUBLANES, :] = cbuf[TILE:TILE + SUBLANES, :]

    small = zc[:, SM:SM + LANES]
    pre = small + gb_ref[...]
    lf = _log_sigmoid(pre)
    la = _log_sigmoid(_dot(small.astype(BF16), wal_ref[...]) + bal_ref[...]) * (1.0 / GLA_TAU)
    tr = lax.broadcasted_iota(jnp.int32, (TILE, TILE), 0)
    tc = lax.broadcasted_iota(jnp.int32, (TILE, TILE), 1)
    tri = jnp.where(tr >= tc, 1.0, 0.0).astype(BF16)
    terms = []
    for v in (lf, la):
        hi = v.astype(BF16)
        r1 = v - hi.astype(F32)
        mid = r1.astype(BF16)
        terms += [hi, mid, (r1 - mid.astype(F32)).astype(BF16)]
    cs = _dot(tri, jnp.concatenate(terms, axis=1))
    bml = (cs[:, 2 * LANES:3 * LANES] + cs[:, LANES:2 * LANES]) + cs[:, 0:LANES]
    bla = (cs[:, 5 * LANES:6 * LANES] + cs[:, 4 * LANES:5 * LANES]) + cs[:, 3 * LANES:4 * LANES]
    bgl = jnp.concatenate(
        [bla[0:CHUNK]] + [bla[c * CHUNK:(c + 1) * CHUNK] - bla[c * CHUNK - 1:c * CHUNK]
                          for c in range(1, NCH)], axis=0)
    pre_t = pre.T
    bml_t = bml.T
    bgl_t = bgl.T
    ip(1)

    prow = lax.broadcasted_iota(jnp.int32, (LANES, LANES), 0)
    pcol = lax.broadcasted_iota(jnp.int32, (LANES, LANES), 1)
    pair_diag = (prow < HEAD_DIM) == (pcol < HEAD_DIM)
    prow_first = prow[:, 0:1] < HEAD_DIM

    ml_kt, ml_vb, ml_vv, ml_s, ml_ni, ml_ct, ml_nc = [], [], [], [], [], [], []
    for p in range(N_PAIRS):
        c0 = p * LANES
        q2 = zc[:, MQ + c0:MQ + c0 + LANES]
        k2 = zc[:, MK + c0:MK + c0 + LANES]
        v2 = zc[:, MV + c0:MV + c0 + LANES]
        q2b = q2.astype(BF16)
        k2_t = k2.T
        kk_t = jnp.concatenate([jnp.where(frow, k2_t, 0.0), jnp.where(frow, 0.0, k2_t)],
                               axis=1).astype(BF16)
        ct = ml_c[p]
        ncol = ml_n[p][:, 0:1]
        n_ext = jnp.where(lane == 0, jnp.where(frow, ncol, 0.0),
                          jnp.where(lane == 1, jnp.where(frow, 0.0, ncol), 0.0))
        ml_kt.append(k2_t)
        ml_nc.append(ncol)
        ml_vb.append(v2.astype(BF16))
        ml_vv.append(jnp.concatenate([jnp.where(m0, v2, 0.0), jnp.where(m0, 0.0, v2)],
                                     axis=0).astype(BF16))
        ml_ct.append(ct)
        ml_s.append(_dot(q2b, kk_t))
        ml_ni.append(_dot(q2b, jnp.concatenate([ct, n_ext], axis=1).astype(BF16)))
    ip(2)

    for p in range(N_PAIRS):
        c0 = p * LANES
        s_pair = ml_s[p]
        iws, dens, wrows, solds = [], [], [], []
        for e in range(2):
            h = 2 * p + e
            mprev = ml_m[h][0:1, :]
            bt_all = jnp.broadcast_to(bml[:, SM_F + h:SM_F + h + 1], (TILE, LANES))
            bj = bml_t[SM_F + h:SM_F + h + 1, :]
            ij = pre_t[SM_I + h:SM_I + h + 1, :]
            den_inter = ml_ni[p][:, LANES + e:LANES + e + 1]
            iw_blocks, den_blocks = [], []
            for rb in range(NCH):
                r0 = rb * CHUNK
                ncol = LANES * ((r0 + CHUNK + LANES - 1) // LANES)
                rr = lax.broadcasted_iota(jnp.int32, (CHUNK, ncol), 0) + r0
                kc = lax.broadcasted_iota(jnp.int32, (CHUNK, ncol), 1)
                btc = bt_all[r0:r0 + CHUNK, 0:1]
                log_d = jnp.where(rr >= kc, btc - bj[:, :ncol] + ij[:, :ncol], NEG)
                inter_log = btc + mprev[:, 0:1]
                m_row = jnp.maximum(inter_log, jnp.max(log_d, axis=1, keepdims=True))
                sc = s_pair[r0:r0 + CHUNK, e * TILE:e * TILE + ncol] * jnp.exp(log_d - m_row)
                sc_ref[p, r0:r0 + CHUNK, e * TILE:e * TILE + ncol] = sc.astype(BF16)
                if ncol < TILE:
                    sc_ref[p, r0:r0 + CHUNK, e * TILE + ncol:(e + 1) * TILE] = jnp.zeros(
                        (CHUNK, TILE - ncol), BF16)
                inter_w = jnp.exp(inter_log - m_row)
                den = jnp.sum(sc, axis=1, keepdims=True) + inter_w * den_inter[r0:r0 + CHUNK]
                den_blocks.append(jnp.maximum(jnp.abs(den), jnp.exp(-m_row)))
                iw_blocks.append(inter_w)
            iws.append(jnp.concatenate(iw_blocks, axis=0))
            dens.append(jnp.concatenate(den_blocks, axis=0))
            g_row = bt_all[TILE - 1:TILE, :]
            a_row = g_row[:, 0:1] - bj + ij
            m_new = jnp.maximum(g_row + mprev, jnp.max(a_row, axis=1, keepdims=True))
            wrows.append(jnp.exp(a_row - m_new[:, 0:1]))
            solds.append(jnp.exp(g_row + mprev - m_new))
            ml_m[h] = jnp.broadcast_to(m_new, (SUBLANES, LANES))
        num = _dot(sc_ref[p], ml_vv[p]) + jnp.where(m0, iws[0], iws[1]) * ml_ni[p][:, :LANES]
        hcur = num / jnp.where(m0, dens[0], dens[1])
        y_ref[:, ATT_W + c0:ATT_W + c0 + LANES] = (
            jax.nn.sigmoid(zc[:, MO + c0:MO + c0 + LANES])
            * _pair_rms(hcur, mlg_ref[:, c0:c0 + LANES], m0)
            * _silu(zc[:, MG + c0:MG + c0 + LANES]))
        kw_t = ml_kt[p] * jnp.where(frow, wrows[0], wrows[1])
        upd = _dot(kw_t.astype(BF16), ml_vb[p])
        sold_col = jnp.where(prow_first, solds[0][:, 0:1], solds[1][:, 0:1])
        ml_c[p] = jnp.where(pair_diag, sold_col * ml_ct[p] + upd, 0.0)
        ncol = sold_col * ml_nc[p] + jnp.sum(kw_t, axis=1, keepdims=True)
        ml_n[p] = jnp.broadcast_to(ncol, (LANES, LANES))
        ip(IP_PER_ML_PAIR)

    grow = lax.broadcasted_iota(jnp.int32, (LANES, GLA_VW), 0)
    gcol = lax.broadcasted_iota(jnp.int32, (LANES, GLA_VW), 1)
    gla_diag = (grow >> GLA_DK_BITS) == (gcol >> GLA_DV_BITS)
    khead = lane >> GLA_DK_BITS
    vhead = lax.broadcasted_iota(jnp.int32, (1, GLA_VW), 1) >> GLA_DV_BITS
    arow = lax.broadcasted_iota(jnp.int32, (CHUNK, GLA_VW), 0)
    acol = lax.broadcasted_iota(jnp.int32, (CHUNK, GLA_VW), 1)
    causal4 = arow >= (acol & (CHUNK - 1))
    n_gh = GLA_KW // GLA_DK
    yc0 = ATT_W + ML_W
    g_a, g_upd, g_qi, g_vvg, g_dec = [], [], [], [], []
    for c in range(NCH):
        r0 = c * CHUNK
        bc = bgl[r0:r0 + CHUNK, :]
        bref = bc[CHUNK // 2 - 1:CHUNK // 2, :]
        btot = bc[CHUNK - 1:CHUNK, :]
        gq = zc[r0:r0 + CHUNK, GQ:GQ + GLA_KW] * (GLA_DK ** -0.5)
        gk = zc[r0:r0 + CHUNK, GK:GK + GLA_KW]
        gv = zc[r0:r0 + CHUNK, GV:GV + GLA_VW]
        qe = (gq * jnp.exp(bc - bref)).astype(BF16)
        ke = gk * jnp.exp(bref - bc)
        kek = jnp.concatenate([jnp.where(khead == h, ke, 0.0) for h in range(n_gh)],
                              axis=0).astype(BF16)
        g_vvg.append(jnp.concatenate([jnp.where(vhead == h, gv, 0.0) for h in range(n_gh)],
                                     axis=0).astype(BF16))
        g_qi.append((gq * jnp.exp(bc)).astype(BF16))
        kd = (gk * jnp.exp(btot - bc)).astype(BF16)
        g_dec.append(jnp.exp(bgl_t[:, r0 + CHUNK - 1:r0 + CHUNK]))
        g_a.append(_dot_nt(qe, kek))
        g_upd.append(_dot_tn(kd, gv.astype(BF16)))
    ip(2)
    s_all = gla_s[...]
    for c in range(NCH):
        r0 = c * CHUNK
        a = jnp.where(causal4, g_a[c], 0.0)
        o = _dot(g_qi[c], s_all.astype(BF16)) + _dot(a.astype(BF16), g_vvg[c])
        y_ref[r0:r0 + CHUNK, yc0:yc0 + GLA_VW] = o
        s_all = jnp.where(gla_diag, g_dec[c] * s_all + g_upd[c], 0.0)
        ip(1)
    gla_s[...] = s_all

    for hp in range(GLA_VW // LANES):
        c0 = hp * LANES
        hcur = y_ref[:, yc0 + c0:yc0 + c0 + LANES]
        y_ref[:, yc0 + c0:yc0 + c0 + LANES] = (
            _pair_rms(hcur, glg_ref[:, c0:c0 + LANES], m0) * _silu(zc[:, GG + c0:GG + c0 + LANES]))
    ip(D_INP // IP_BLK)

    xo = xc_ref[...] + _dot(y_ref[...].astype(BF16), wout_ref[...])
    if final_norm:
        xo = xo * lax.rsqrt(jnp.sum(xo * xo, axis=1, keepdims=True) * (1.0 / D_MODEL) + EPS) * fg_ref[...]
    o_ref[...] = xo


def _project(x, ng_ref, w_cols, z_out):
    hn = x * lax.rsqrt(jnp.sum(x * x, axis=1, keepdims=True) * (1.0 / D_MODEL) + EPS) * ng_ref[...]
    hn_b = hn.astype(BF16)
    for n0 in range(0, D_INP, IP_BLK):
        z_out[:, n0:n0 + IP_BLK] = _dot(hn_b, w_cols(n0))


def _layer_kernel(xa_ref, xn_ref, ng_ref, win_ref, wtl_ref, gb_ref, conv_ref, wal_ref, bal_ref, gp_ref,
                  mlg_ref, glg_ref, wout_ref, fg_ref,
                  o_ref,
                  z_a, z_b, kb, vb, bias_ref, cbuf, ml_c, ml_n, ml_m, gla_s, y_ref, sc_ref, wrm,
                  *, final_norm, tiles_per_seq):
    g = pl.program_id(0)
    i = lax.rem(2 * g, tiles_per_seq)

    def w_rows(n0):
        if n0 < W_MAIN:
            return win_ref[n0:n0 + IP_BLK, :]
        return wtl_ref[n0 - W_MAIN:n0 - W_MAIN + IP_BLK, :]

    def w_cols(n0):
        return wrm[:, n0:n0 + IP_BLK]

    @pl.when(g == 0)
    def _first_step():
        r = lax.broadcasted_iota(jnp.int32, (TILE, KEYS), 0)
        m = lax.broadcasted_iota(jnp.int32, (TILE, KEYS), 1)
        jj = m - ((r >> CHUNK_BITS) << CHUNK_BITS)
        in_band = (jj >= 0) & (jj < BAND)
        for h in range(ML_HEADS):
            row = jnp.broadcast_to(gp_ref[h:h + 1, :], (TILE, ROLL_W))
            rolled = pltpu.roll(row, KEYS, 1, stride=1, stride_axis=0)
            bias_ref[h] = jnp.where(in_band, rolled[:, :KEYS] * LOG2E, NEG)
        for n0 in range(0, D_INP, IP_BLK):
            wrm[:, n0:n0 + IP_BLK] = w_rows(n0).astype(F32).T.astype(BF16)
        _project(xa_ref[0:TILE, :], ng_ref, w_cols, z_a)

    @pl.when(i == 0)
    def _reset():
        kr = lax.broadcasted_iota(jnp.int32, (2 * ATT_W, 1), 0)
        pen_row = (kr & (LANES - 1)) == jnp.where(kr < ATT_W, PEN_LANE_A, PEN_LANE_B)
        pen_init = jnp.broadcast_to(jnp.where(pen_row, NEG, 0.0), kb.shape[1:]).astype(kb.dtype)
        for slot in range(2):
            kb[slot] = pen_init
            vb[slot] = jnp.zeros(vb.shape[1:], vb.dtype)
        cbuf[0:SUBLANES, :] = jnp.zeros((SUBLANES, cbuf.shape[1]), F32)
        ml_c[...] = jnp.zeros(ml_c.shape, F32)
        ml_n[...] = jnp.zeros(ml_n.shape, F32)
        ml_m[...] = jnp.zeros(ml_m.shape, F32)
        gla_s[...] = jnp.zeros(gla_s.shape, F32)

    step = functools.partial(
        _step, ng_ref=ng_ref, w_cols=w_cols, gb_ref=gb_ref,
        conv_ref=conv_ref, wal_ref=wal_ref, bal_ref=bal_ref, mlg_ref=mlg_ref, glg_ref=glg_ref,
        wout_ref=wout_ref, fg_ref=fg_ref, bias_ref=bias_ref,
        cbuf=cbuf, ml_c=ml_c, ml_n=ml_n, ml_m=ml_m, gla_s=gla_s, y_ref=y_ref, sc_ref=sc_ref,
        final_norm=final_norm)

    lo = pl.ds(0, TILE)
    hi = pl.ds(TILE, TILE)
    step(z_a, z_b, xn_ref=xa_ref.at[hi], xc_ref=xa_ref.at[lo], o_ref=o_ref.at[lo],
         kbuf=kb.at[0], vbuf=vb.at[0], kold=kb.at[1], vold=vb.at[1])
    step(z_b, z_a, xn_ref=xn_ref, xc_ref=xa_ref.at[hi], o_ref=o_ref.at[hi],
         kbuf=kb.at[1], vbuf=vb.at[1], kold=kb.at[0], vold=vb.at[0])


def _layer_spec(layer, arr, rows=None):
    shape = arr.shape[1:] if rows is None else (rows,) + arr.shape[2:]
    nd = len(shape)
    return pl.BlockSpec((None,) + shape, lambda g, _l=layer, _nd=nd: (_l,) + (0,) * _nd,
                        pipeline_mode=pl.Buffered(1))


def _layer_call(x2d, layer, prm, final_norm, tiles_per_seq):
    n_tiles = x2d.shape[0] // TILE
    n_steps = n_tiles // 2
    names = ("ng", "win", "wtl", "gb", "conv", "wal", "bal", "gp", "mlg", "glg", "wout")
    args = (x2d, x2d) + tuple(prm[k] for k in names) + (prm["fg"],)
    cur_spec = pl.BlockSpec((2 * TILE, D_MODEL), lambda g: (g, 0))
    nxt_spec = pl.BlockSpec((TILE, D_MODEL), lambda g: (jnp.minimum(2 * g + 2, n_tiles - 1), 0))
    in_specs = [cur_spec, nxt_spec]
    in_specs += [_layer_spec(layer, prm[k], rows=W_MAIN if k == "win" else None) for k in names]
    in_specs += [pl.BlockSpec(prm["fg"].shape, lambda g: (0, 0), pipeline_mode=pl.Buffered(1))]
    scratch = [
        pltpu.VMEM((TILE, D_INP), F32),
        pltpu.VMEM((TILE, D_INP), F32),
        pltpu.VMEM((2, 2 * ATT_W, KEYS), BF16),
        pltpu.VMEM((2, KEYS, 2 * ATT_W), BF16),
        pltpu.VMEM((ML_HEADS, TILE, KEYS), F32),
        pltpu.VMEM((TILE + 2 * SUBLANES, 2 * ML_W), F32),
        pltpu.VMEM((N_PAIRS, LANES, LANES), F32),
        pltpu.VMEM((N_PAIRS, LANES, LANES), F32),
        pltpu.VMEM((ML_HEADS, SUBLANES, LANES), F32),
        pltpu.VMEM((LANES, GLA_VW), F32),
        pltpu.VMEM((TILE, D_MODEL), F32),
        pltpu.VMEM((N_PAIRS, TILE, 2 * TILE), BF16),
        pltpu.VMEM((D_MODEL, D_INP), BF16),
    ]
    return pl.pallas_call(
        functools.partial(_layer_kernel, final_norm=final_norm, tiles_per_seq=tiles_per_seq),
        out_shape=jax.ShapeDtypeStruct(x2d.shape, x2d.dtype),
        grid=(n_steps,),
        in_specs=in_specs,
        out_specs=cur_spec,
        scratch_shapes=scratch,
        compiler_params=pltpu.CompilerParams(
            dimension_semantics=("arbitrary",),
            vmem_limit_bytes=VMEM_LIMIT),
        name="hybrid_layer_final" if final_norm else "hybrid_layer",
    )(*args)


def _prep_params(norm_g, w_in, b_gates, conv_w, w_alpha, b_alpha, rel_bias, ml_norm_g, gla_norm_g, w_out, final_g):
    depth = norm_g.shape[0]
    w_t = jnp.transpose(w_in, (0, 2, 1)).astype(BF16)
    pad = jnp.zeros((depth, D_INP - SM - GLA_RANK - 2 * ML_HEADS, D_MODEL), BF16)
    wtl = jnp.concatenate([w_t[:, REF_MG:REF_GA], w_t[:, REF_GG:D_IN], w_t[:, REF_GA:REF_GG],
                           w_t[:, REF_MI:REF_MG], pad], axis=1)
    gb = jnp.pad(b_gates, ((0, 0), (SM_I, LANES - SM_I - 2 * ML_HEADS))).reshape(depth, 1, LANES)
    wal = jnp.pad(w_alpha, ((0, 0), (0, LANES - GLA_RANK), (0, 0))).astype(BF16)
    nh = rel_bias.shape[1]
    gp = jnp.concatenate([
        jnp.broadcast_to(rel_bias[:, :, 2 * REL_CLIP:], (depth, nh, KEYS - REL_CLIP + 1)),
        rel_bias[:, :, 2 * REL_CLIP - 1:0:-1],
        jnp.broadcast_to(rel_bias[:, :, :1], (depth, nh, ROLL_W - KEYS - REL_CLIP)),
    ], axis=2)
    return dict(ng=norm_g.reshape(depth, 1, D_MODEL), win=w_t, wtl=wtl, gb=gb, conv=conv_w, wal=wal,
                bal=b_alpha.reshape(depth, 1, GLA_KW), gp=gp, mlg=ml_norm_g.reshape(depth, 1, ML_W),
                glg=gla_norm_g.reshape(depth, 1, GLA_VW), wout=w_out.astype(BF16),
                fg=final_g.reshape(1, D_MODEL))


def kernel(x, norm_g, w_in, b_gates, conv_w, w_alpha, b_alpha, rel_bias, ml_norm_g, gla_norm_g, w_out, final_g):
    depth = norm_g.shape[0]
    bsz, seq, _ = x.shape
    prm = _prep_params(norm_g, w_in, b_gates, conv_w, w_alpha, b_alpha, rel_bias, ml_norm_g, gla_norm_g,
                       w_out, final_g)
    x2d = x.reshape(bsz * seq, D_MODEL)
    for l in range(depth):
        x2d = _layer_call(x2d, l, prm, final_norm=(l == depth - 1), tiles_per_seq=seq // TILE)
    return x2d.reshape(bsz, seq, D_MODEL)
```

```python
import functools

import jax
import jax.numpy as jnp
from jax import lax
from jax.experimental import pallas as pl
from jax.experimental.pallas import tpu as pltpu

F32 = jnp.float32
BF16 = jnp.bfloat16

D_MODEL = 1024
CHUNK = 64
HEAD_DIM = 64
ATT_W = 384
ML_W = 384
GLA_KW = 128
GLA_VW = 256
GLA_DK = 32
GLA_RANK = 16
GLA_TAU = 16.0
ML_HEADS = 6
PAST_CHUNKS = 8
REL_CLIP = 128
CONV_W = 4
EPS = 1e-6
NEG = -1e30
LOG2E = 1.4426950408889634

LANES = 128
SUBLANES = 8
TILE = 256
NCH = TILE // CHUNK
CHUNK_BITS = CHUNK.bit_length() - 1
GLA_DK_BITS = GLA_DK.bit_length() - 1
GLA_DV_BITS = (GLA_VW * GLA_DK // GLA_KW).bit_length() - 1
HIST = PAST_CHUNKS * CHUNK
KEYS = HIST + TILE
HALF_ROWS = TILE // 2
HALF_KEYS = HIST + HALF_ROWS
NEAR0 = HIST - REL_CLIP
PEN_LANE_A = HEAD_DIM
PEN_LANE_B = 0
BAND = (PAST_CHUNKS + 1) * CHUNK
ROLL_W = 1024
N_PAIRS = ML_W // LANES

AQ, AK, AV, AG = 0, 384, 768, 1152
MQ, MK, MV, MO, MG = 1536, 1920, 2304, 2688, 3072
GQ, GK, GV, GG = 3456, 3584, 3712, 3968
SM = 4224
D_INP = 4352
IP_BLK = 256
W_MAIN = 3072
REF_MI, REF_MG, REF_GA, REF_GG, D_IN = 3072, 3084, 3980, 3996, 4252
IP_PER_HEAD, IP_AFTER_ATT, IP_PER_ML_PAIR = 1, 1, 1
SM_I = GLA_RANK
SM_F = GLA_RANK + ML_HEADS

VMEM_LIMIT = 56 * 1024 * 1024


def _log_sigmoid(x):
    return jnp.minimum(x, 0.0) - jnp.log1p(jnp.exp(-jnp.abs(x)))


def _silu(x):
    return x * jax.nn.sigmoid(x)


def _dot_nt(a, b):
    return lax.dot_general(a, b, (((1,), (1,)), ((), ())), preferred_element_type=F32)


def _dot_tn(a, b):
    return lax.dot_general(a, b, (((0,), (0,)), ((), ())), preferred_element_type=F32)


def _dot(a, b):
    return jnp.dot(a, b, preferred_element_type=F32)


def _pair_rms(hcur, gain, m0):
    sq = hcur * hcur
    ms0 = jnp.sum(jnp.where(m0, sq, 0.0), axis=1, keepdims=True)
    ms1 = jnp.sum(jnp.where(m0, 0.0, sq), axis=1, keepdims=True)
    ms = jnp.where(m0, ms0, ms1) * (1.0 / HEAD_DIM)
    return hcur * lax.rsqrt(ms + EPS) * gain


def _step(zc, zn, xn_ref, xc_ref, ng_ref, w_cols, gb_ref, conv_ref, wal_ref, bal_ref,
          mlg_ref, glg_ref, wout_ref, fg_ref, o_ref,
          kbuf, vbuf, kold, vold, bias_ref, cbuf, ml_c, ml_n, ml_m, gla_s, y_ref, sc_ref, *, final_norm):
    lane = lax.broadcasted_iota(jnp.int32, (1, LANES), 1)
    m0 = lane < HEAD_DIM
    m1 = jnp.logical_not(m0)

    xn = xn_ref[...]
    hn = xn * lax.rsqrt(jnp.sum(xn * xn, axis=1, keepdims=True) * (1.0 / D_MODEL) + EPS) * ng_ref[...]
    hn_b = hn.astype(BF16)
    ip_next = [0]

    def ip(count):
        for _ in range(count):
            n0 = ip_next[0] * IP_BLK
            if n0 < D_INP:
                zn[:, n0:n0 + IP_BLK] = _dot(hn_b, w_cols(n0))
                ip_next[0] += 1

    kbuf[:, 0:HIST] = kold[:, TILE:KEYS]
    vbuf[0:HIST, :] = vold[TILE:KEYS, :]
    frow = lax.broadcasted_iota(jnp.int32, (LANES, 1), 0) < HEAD_DIM
    for p in range(N_PAIRS):
        c0 = p * LANES
        kn_t = zc[:, AK + c0:AK + c0 + LANES].T
        vn = zc[:, AV + c0:AV + c0 + LANES]
        kbuf[c0:c0 + LANES, HIST:KEYS] = jnp.where(frow, kn_t, 0.0).astype(BF16)
        kbuf[ATT_W + c0:ATT_W + c0 + LANES, HIST:KEYS] = jnp.where(frow, 0.0, kn_t).astype(BF16)
        vbuf[HIST:KEYS, c0:c0 + LANES] = jnp.where(m0, vn, 1.0).astype(BF16)
        vbuf[HIST:KEYS, ATT_W + c0:ATT_W + c0 + LANES] = jnp.where(m0, 1.0, vn).astype(BF16)

    def att_scores(u):
        h, half = u // 2, u % 2
        c0 = (h // 2) * LANES
        cb = (h % 2) * ATT_W + c0
        r0, k0 = half * HALF_ROWS, half * HALF_ROWS
        q2 = zc[r0:r0 + HALF_ROWS, AQ + c0:AQ + c0 + LANES] * (HEAD_DIM ** -0.5 * LOG2E)
        if h % 2 == 0:
            qm = jnp.where(m0, q2, jnp.where(lane == PEN_LANE_A, 1.0, 0.0))
        else:
            qm = jnp.where(m1, q2, jnp.where(lane == PEN_LANE_B, 1.0, 0.0))
        s = _dot(qm.astype(BF16), kbuf[cb:cb + LANES, k0:k0 + HALF_KEYS])
        def bias(c_lo, c_hi):
            return bias_ref[h, r0:r0 + HALF_ROWS, k0 + c_lo:k0 + c_hi]
        return jnp.concatenate([s[:, :LANES] + bias(0, LANES), s[:, LANES:NEAR0],
                                s[:, NEAR0:] + bias(NEAR0, HALF_KEYS)], axis=1)

    def att_out(u, s):
        h, half = u // 2, u % 2
        cb = (h % 2) * ATT_W + (h // 2) * LANES
        k0 = half * HALF_ROWS
        pe = jnp.exp2(s - jnp.max(s, axis=1, keepdims=True))
        return _dot(pe.astype(BF16), vbuf[k0:k0 + HALF_KEYS, cb:cb + LANES])

    n_units = 2 * ML_HEADS
    s_cur = att_scores(0)
    outs = []
    for u in range(n_units):
        s_nxt = att_scores(u + 1) if u + 1 < n_units else None
        if u % 2 == 1:
            ip(IP_PER_HEAD)
        outs.append(att_out(u, s_cur))
        s_cur = s_nxt
        if u % 4 == 3:
            c0 = (u // 4) * LANES
            r_a = jnp.concatenate(outs[u - 3:u - 1], axis=0)
            r_b = jnp.concatenate(outs[u - 1:u + 1], axis=0)
            den = pltpu.roll(jnp.where(m0, r_b, r_a), HEAD_DIM, 1)
            att = jnp.where(m0, r_a, r_b) / den
            y_ref[:, c0:c0 + LANES] = att * _silu(zc[:, AG + c0:AG + c0 + LANES])
    ip(IP_AFTER_ATT)

    cbuf[SUBLANES:SUBLANES + TILE, :] = zc[:, MQ:MQ + 2 * ML_W]
    for c0 in range(0, 2 * ML_W, LANES):
        acc = cbuf[SUBLANES:SUBLANES + TILE, c0:c0 + LANES] * conv_ref[CONV_W - 1:CONV_W, c0:c0 + LANES]
        for sft in range(1, CONV_W):
            acc = acc + cbuf[SUBLANES - sft:SUBLANES - sft + TILE, c0:c0 + LANES] * \
                conv_ref[CONV_W - 1 - sft:CONV_W - sft, c0:c0 + LANES]
        act = _silu(acc)
        if c0 >= ML_W:
            act = act * (HEAD_DIM ** -0.5)
        zc[:, MQ + c0:MQ + c0 + LANES] = act
    cbuf[0:SUBLANES, :] = cbuf[TILE:TILE + SUBLANES, :]

    small = zc[:, SM:SM + LANES]
    pre = small + gb_ref[...]
    lf = _log_sigmoid(pre)
    la = _log_sigmoid(_dot(small.astype(BF16), wal_ref[...]) + bal_ref[...]) * (1.0 / GLA_TAU)
    tr = lax.broadcasted_iota(jnp.int32, (TILE, TILE), 0)
    tc = lax.broadcasted_iota(jnp.int32, (TILE, TILE), 1)
    tri = jnp.where(tr >= tc, 1.0, 0.0).astype(BF16)
    terms = []
    for v in (lf, la):
        hi = v.astype(BF16)
        r1 = v - hi.astype(F32)
        mid = r1.astype(BF16)
        terms += [hi, mid, (r1 - mid.astype(F32)).astype(BF16)]
    cs = _dot(tri, jnp.concatenate(terms, axis=1))
    bml = (cs[:, 2 * LANES:3 * LANES] + cs[:, LANES:2 * LANES]) + cs[:, 0:LANES]
    bla = (cs[:, 5 * LANES:6 * LANES] + cs[:, 4 * LANES:5 * LANES]) + cs[:, 3 * LANES:4 * LANES]
    bgl = jnp.concatenate(
        [bla[0:CHUNK]] + [bla[c * CHUNK:(c + 1) * CHUNK] - bla[c * CHUNK - 1:c * CHUNK]
                          for c in range(1, NCH)], axis=0)
    pre_t = pre.T
    bml_t = bml.T
    bgl_t = bgl.T
    ip(1)

    prow = lax.broadcasted_iota(jnp.int32, (LANES, LANES), 0)
    pcol = lax.broadcasted_iota(jnp.int32, (LANES, LANES), 1)
    pair_diag = (prow < HEAD_DIM) == (pcol < HEAD_DIM)
    prow_first = prow[:, 0:1] < HEAD_DIM

    ml_kt, ml_vb, ml_vv, ml_s, ml_ni, ml_ct, ml_nc = [], [], [], [], [], [], []
    for p in range(N_PAIRS):
        c0 = p * LANES
        q2 = zc[:, MQ + c0:MQ + c0 + LANES]
        k2 = zc[:, MK + c0:MK + c0 + LANES]
        v2 = zc[:, MV + c0:MV + c0 + LANES]
        q2b = q2.astype(BF16)
        k2_t = k2.T
        kk_t = jnp.concatenate([jnp.where(frow, k2_t, 0.0), jnp.where(frow, 0.0, k2_t)],
                               axis=1).astype(BF16)
        ct = ml_c[p]
        ncol = ml_n[p][:, 0:1]
        n_ext = jnp.where(lane == 0, jnp.where(frow, ncol, 0.0),
                          jnp.where(lane == 1, jnp.where(frow, 0.0, ncol), 0.0))
        ml_kt.append(k2_t)
        ml_nc.append(ncol)
        ml_vb.append(v2.astype(BF16))
        ml_vv.append(jnp.concatenate([jnp.where(m0, v2, 0.0), jnp.where(m0, 0.0, v2)],
                                     axis=0).astype(BF16))
        ml_ct.append(ct)
        ml_s.append(_dot(q2b, kk_t))
        ml_ni.append(_dot(q2b, jnp.concatenate([ct, n_ext], axis=1).astype(BF16)))
    ip(2)

    for p in range(N_PAIRS):
        c0 = p * LANES
        s_pair = ml_s[p]
        iws, dens, wrows, solds = [], [], [], []
        for e in range(2):
            h = 2 * p + e
            mprev = ml_m[h][0:1, :]
            bt_all = jnp.broadcast_to(bml[:, SM_F + h:SM_F + h + 1], (TILE, LANES))
            bj = bml_t[SM_F + h:SM_F + h + 1, :]
            ij = pre_t[SM_I + h:SM_I + h + 1, :]
            den_inter = ml_ni[p][:, LANES + e:LANES + e + 1]
            dj = ij - bj
            iw_blocks, den_blocks = [], []
            for rb in range(NCH):
                r0 = rb * CHUNK
                ncol = LANES * ((r0 + CHUNK + LANES - 1) // LANES)
                btc = bt_all[r0:r0 + CHUNK, 0:1]
                log_d = btc + dj[:, :ncol]
                dt = ncol - LANES
                rr = lax.broadcasted_iota(jnp.int32, (CHUNK, LANES), 0) + r0
                kc = lax.broadcasted_iota(jnp.int32, (CHUNK, LANES), 1) + dt
                edge = jnp.where(rr >= kc, log_d[:, dt:], NEG)
                log_d = edge if dt == 0 else jnp.concatenate([log_d[:, :dt], edge], axis=1)
                inter_log = btc + mprev[:, 0:1]
                m_row = jnp.maximum(inter_log, jnp.max(log_d, axis=1, keepdims=True))
                sc = s_pair[r0:r0 + CHUNK, e * TILE:e * TILE + ncol] * jnp.exp(log_d - m_row)
                sc_ref[p, r0:r0 + CHUNK, e * TILE:e * TILE + ncol] = sc.astype(BF16)
                if ncol < TILE:
                    sc_ref[p, r0:r0 + CHUNK, e * TILE + ncol:(e + 1) * TILE] = jnp.zeros(
                        (CHUNK, TILE - ncol), BF16)
                inter_w = jnp.exp(inter_log - m_row)
                den = jnp.sum(sc, axis=1, keepdims=True) + inter_w * den_inter[r0:r0 + CHUNK]
                den_blocks.append(jnp.maximum(jnp.abs(den), jnp.exp(-m_row)))
                iw_blocks.append(inter_w)
            iws.append(jnp.concatenate(iw_blocks, axis=0))
            dens.append(jnp.concatenate(den_blocks, axis=0))
            g_row = bt_all[TILE - 1:TILE, :]
            a_row = g_row[:, 0:1] + dj
            m_new = jnp.maximum(g_row + mprev, jnp.max(a_row, axis=1, keepdims=True))
            wrows.append(jnp.exp(a_row - m_new[:, 0:1]))
            solds.append(jnp.exp(g_row + mprev - m_new))
            ml_m[h] = jnp.broadcast_to(m_new, (SUBLANES, LANES))
        num = _dot(sc_ref[p], ml_vv[p]) + jnp.where(m0, iws[0], iws[1]) * ml_ni[p][:, :LANES]
        hcur = num / jnp.where(m0, dens[0], dens[1])
        y_ref[:, ATT_W + c0:ATT_W + c0 + LANES] = (
            jax.nn.sigmoid(zc[:, MO + c0:MO + c0 + LANES])
            * _pair_rms(hcur, mlg_ref[:, c0:c0 + LANES], m0)
            * _silu(zc[:, MG + c0:MG + c0 + LANES]))
        kw_t = ml_kt[p] * jnp.where(frow, wrows[0], wrows[1])
        upd = _dot(kw_t.astype(BF16), ml_vb[p])
        sold_col = jnp.where(prow_first, solds[0][:, 0:1], solds[1][:, 0:1])
        ml_c[p] = jnp.where(pair_diag, sold_col * ml_ct[p] + upd, 0.0)
        ncol = sold_col * ml_nc[p] + jnp.sum(kw_t, axis=1, keepdims=True)
        ml_n[p] = jnp.broadcast_to(ncol, (LANES, LANES))
        ip(IP_PER_ML_PAIR)

    grow = lax.broadcasted_iota(jnp.int32, (LANES, GLA_VW), 0)
    gcol = lax.broadcasted_iota(jnp.int32, (LANES, GLA_VW), 1)
    gla_diag = (grow >> GLA_DK_BITS) == (gcol >> GLA_DV_BITS)
    khead = lane >> GLA_DK_BITS
    vhead = lax.broadcasted_iota(jnp.int32, (1, GLA_VW), 1) >> GLA_DV_BITS
    arow = lax.broadcasted_iota(jnp.int32, (CHUNK, GLA_VW), 0)
    acol = lax.broadcasted_iota(jnp.int32, (CHUNK, GLA_VW), 1)
    causal4 = arow >= (acol & (CHUNK - 1))
    n_gh = GLA_KW // GLA_DK
    yc0 = ATT_W + ML_W
    g_a, g_upd, g_qi, g_vvg, g_dec = [], [], [], [], []
    for c in range(NCH):
        r0 = c * CHUNK
        bc = bgl[r0:r0 + CHUNK, :]
        bref = bc[CHUNK // 2 - 1:CHUNK // 2, :]
        btot = bc[CHUNK - 1:CHUNK, :]
        gq = zc[r0:r0 + CHUNK, GQ:GQ + GLA_KW] * (GLA_DK ** -0.5)
        gk = zc[r0:r0 + CHUNK, GK:GK + GLA_KW]
        gv = zc[r0:r0 + CHUNK, GV:GV + GLA_VW]
        qe = (gq * jnp.exp(bc - bref)).astype(BF16)
        ke = gk * jnp.exp(bref - bc)
        kek = jnp.concatenate([jnp.where(khead == h, ke, 0.0) for h in range(n_gh)],
                              axis=0).astype(BF16)
        g_vvg.append(jnp.concatenate([jnp.where(vhead == h, gv, 0.0) for h in range(n_gh)],
                                     axis=0).astype(BF16))
        g_qi.append((gq * jnp.exp(bc)).astype(BF16))
        kd = (gk * jnp.exp(btot - bc)).astype(BF16)
        g_dec.append(jnp.exp(bgl_t[:, r0 + CHUNK - 1:r0 + CHUNK]))
        g_a.append(_dot_nt(qe, kek))
        g_upd.append(_dot_tn(kd, gv.astype(BF16)))
    ip(2)
    s_all = gla_s[...]
    for c in range(NCH):
        r0 = c * CHUNK
        a = jnp.where(causal4, g_a[c], 0.0)
        o = _dot(g_qi[c], s_all.astype(BF16)) + _dot(a.astype(BF16), g_vvg[c])
        y_ref[r0:r0 + CHUNK, yc0:yc0 + GLA_VW] = o
        s_all = jnp.where(gla_diag, g_dec[c] * s_all + g_upd[c], 0.0)
        ip(1)
    gla_s[...] = s_all

    for hp in range(GLA_VW // LANES):
        c0 = hp * LANES
        hcur = y_ref[:, yc0 + c0:yc0 + c0 + LANES]
        y_ref[:, yc0 + c0:yc0 + c0 + LANES] = (
            _pair_rms(hcur, glg_ref[:, c0:c0 + LANES], m0) * _silu(zc[:, GG + c0:GG + c0 + LANES]))
    ip(D_INP // IP_BLK)

    xo = xc_ref[...] + _dot(y_ref[...].astype(BF16), wout_ref[...])
    if final_norm:
        xo = xo * lax.rsqrt(jnp.sum(xo * xo, axis=1, keepdims=True) * (1.0 / D_MODEL) + EPS) * fg_ref[...]
    o_ref[...] = xo


def _project(x, ng_ref, w_cols, z_out):
    hn = x * lax.rsqrt(jnp.sum(x * x, axis=1, keepdims=True) * (1.0 / D_MODEL) + EPS) * ng_ref[...]
    hn_b = hn.astype(BF16)
    for n0 in range(0, D_INP, IP_BLK):
        z_out[:, n0:n0 + IP_BLK] = _dot(hn_b, w_cols(n0))


def _layer_kernel(xa_ref, xn_ref, ng_ref, win_ref, wtl_ref, gb_ref, conv_ref, wal_ref, bal_ref, gp_ref,
                  mlg_ref, glg_ref, wout_ref, fg_ref,
                  o_ref,
                  z_a, z_b, kb, vb, bias_ref, cbuf, ml_c, ml_n, ml_m, gla_s, y_ref, sc_ref, wrm,
                  *, final_norm, tiles_per_seq):
    g = pl.program_id(0)
    i = lax.rem(2 * g, tiles_per_seq)

    def w_rows(n0):
        if n0 < W_MAIN:
            return win_ref[n0:n0 + IP_BLK, :]
        return wtl_ref[n0 - W_MAIN:n0 - W_MAIN + IP_BLK, :]

    def w_cols(n0):
        return wrm[:, n0:n0 + IP_BLK]

    @pl.when(g == 0)
    def _first_step():
        r = lax.broadcasted_iota(jnp.int32, (TILE, KEYS), 0)
        m = lax.broadcasted_iota(jnp.int32, (TILE, KEYS), 1)
        jj = m - ((r >> CHUNK_BITS) << CHUNK_BITS)
        in_band = (jj >= 0) & (jj < BAND)
        for h in range(ML_HEADS):
            row = jnp.broadcast_to(gp_ref[h:h + 1, :], (TILE, ROLL_W))
            rolled = pltpu.roll(row, KEYS, 1, stride=1, stride_axis=0)
            far = gp_ref[h:h + 1, 0:1]
            bias_ref[h] = jnp.where(in_band, (rolled[:, :KEYS] - far) * LOG2E, NEG)
        for n0 in range(0, D_INP, IP_BLK):
            wrm[:, n0:n0 + IP_BLK] = w_rows(n0).astype(F32).T.astype(BF16)
        _project(xa_ref[0:TILE, :], ng_ref, w_cols, z_a)

    @pl.when(i == 0)
    def _reset():
        kr = lax.broadcasted_iota(jnp.int32, (2 * ATT_W, 1), 0)
        pen_row = (kr & (LANES - 1)) == jnp.where(kr < ATT_W, PEN_LANE_A, PEN_LANE_B)
        pen_init = jnp.broadcast_to(jnp.where(pen_row, NEG, 0.0), kb.shape[1:]).astype(kb.dtype)
        for slot in range(2):
            kb[slot] = pen_init
            vb[slot] = jnp.zeros(vb.shape[1:], vb.dtype)
        cbuf[0:SUBLANES, :] = jnp.zeros((SUBLANES, cbuf.shape[1]), F32)
        ml_c[...] = jnp.zeros(ml_c.shape, F32)
        ml_n[...] = jnp.zeros(ml_n.shape, F32)
        ml_m[...] = jnp.zeros(ml_m.shape, F32)
        gla_s[...] = jnp.zeros(gla_s.shape, F32)

    step = functools.partial(
        _step, ng_ref=ng_ref, w_cols=w_cols, gb_ref=gb_ref,
        conv_ref=conv_ref, wal_ref=wal_ref, bal_ref=bal_ref, mlg_ref=mlg_ref, glg_ref=glg_ref,
        wout_ref=wout_ref, fg_ref=fg_ref, bias_ref=bias_ref,
        cbuf=cbuf, ml_c=ml_c, ml_n=ml_n, ml_m=ml_m, gla_s=gla_s, y_ref=y_ref, sc_ref=sc_ref,
        final_norm=final_norm)

    lo = pl.ds(0, TILE)
    hi = pl.ds(TILE, TILE)
    step(z_a, z_b, xn_ref=xa_ref.at[hi], xc_ref=xa_ref.at[lo], o_ref=o_ref.at[lo],
         kbuf=kb.at[0], vbuf=vb.at[0], kold=kb.at[1], vold=vb.at[1])
    step(z_b, z_a, xn_ref=xn_ref, xc_ref=xa_ref.at[hi], o_ref=o_ref.at[hi],
         kbuf=kb.at[1], vbuf=vb.at[1], kold=kb.at[0], vold=vb.at[0])


def _layer_spec(layer, arr, rows=None):
    shape = arr.shape[1:] if rows is None else (rows,) + arr.shape[2:]
    nd = len(shape)
    return pl.BlockSpec((None,) + shape, lambda g, _l=layer, _nd=nd: (_l,) + (0,) * _nd,
                        pipeline_mode=pl.Buffered(1))


def _layer_call(x2d, layer, prm, final_norm, tiles_per_seq):
    n_tiles = x2d.shape[0] // TILE
    n_steps = n_tiles // 2
    names = ("ng", "win", "wtl", "gb", "conv", "wal", "bal", "gp", "mlg", "glg", "wout")
    args = (x2d, x2d) + tuple(prm[k] for k in names) + (prm["fg"],)
    cur_spec = pl.BlockSpec((2 * TILE, D_MODEL), lambda g: (g, 0))
    nxt_spec = pl.BlockSpec((TILE, D_MODEL), lambda g: (jnp.minimum(2 * g + 2, n_tiles - 1), 0))
    in_specs = [cur_spec, nxt_spec]
    in_specs += [_layer_spec(layer, prm[k], rows=W_MAIN if k == "win" else None) for k in names]
    in_specs += [pl.BlockSpec(prm["fg"].shape, lambda g: (0, 0), pipeline_mode=pl.Buffered(1))]
    scratch = [
        pltpu.VMEM((TILE, D_INP), F32),
        pltpu.VMEM((TILE, D_INP), F32),
        pltpu.VMEM((2, 2 * ATT_W, KEYS), BF16),
        pltpu.VMEM((2, KEYS, 2 * ATT_W), BF16),
        pltpu.VMEM((ML_HEADS, TILE, KEYS), F32),
        pltpu.VMEM((TILE + 2 * SUBLANES, 2 * ML_W), F32),
        pltpu.VMEM((N_PAIRS, LANES, LANES), F32),
        pltpu.VMEM((N_PAIRS, LANES, LANES), F32),
        pltpu.VMEM((ML_HEADS, SUBLANES, LANES), F32),
        pltpu.VMEM((LANES, GLA_VW), F32),
        pltpu.VMEM((TILE, D_MODEL), F32),
        pltpu.VMEM((N_PAIRS, TILE, 2 * TILE), BF16),
        pltpu.VMEM((D_MODEL, D_INP), BF16),
    ]
    return pl.pallas_call(
        functools.partial(_layer_kernel, final_norm=final_norm, tiles_per_seq=tiles_per_seq),
        out_shape=jax.ShapeDtypeStruct(x2d.shape, x2d.dtype),
        grid=(n_steps,),
        in_specs=in_specs,
        out_specs=cur_spec,
        scratch_shapes=scratch,
        compiler_params=pltpu.CompilerParams(
            dimension_semantics=("arbitrary",),
            vmem_limit_bytes=VMEM_LIMIT),
        name="hybrid_layer_final" if final_norm else "hybrid_layer",
    )(*args)


def _prep_params(norm_g, w_in, b_gates, conv_w, w_alpha, b_alpha, rel_bias, ml_norm_g, gla_norm_g, w_out, final_g):
    depth = norm_g.shape[0]
    w_t = jnp.transpose(w_in, (0, 2, 1)).astype(BF16)
    pad = jnp.zeros((depth, D_INP - SM - GLA_RANK - 2 * ML_HEADS, D_MODEL), BF16)
    wtl = jnp.concatenate([w_t[:, REF_MG:REF_GA], w_t[:, REF_GG:D_IN], w_t[:, REF_GA:REF_GG],
                           w_t[:, REF_MI:REF_MG], pad], axis=1)
    gb = jnp.pad(b_gates, ((0, 0), (SM_I, LANES - SM_I - 2 * ML_HEADS))).reshape(depth, 1, LANES)
    wal = jnp.pad(w_alpha, ((0, 0), (0, LANES - GLA_RANK), (0, 0))).astype(BF16)
    nh = rel_bias.shape[1]
    gp = jnp.concatenate([
        jnp.broadcast_to(rel_bias[:, :, 2 * REL_CLIP:], (depth, nh, KEYS - REL_CLIP + 1)),
        rel_bias[:, :, 2 * REL_CLIP - 1:0:-1],
        jnp.broadcast_to(rel_bias[:, :, :1], (depth, nh, ROLL_W - KEYS - REL_CLIP)),
    ], axis=2)
    return dict(ng=norm_g.reshape(depth, 1, D_MODEL), win=w_t, wtl=wtl, gb=gb, conv=conv_w, wal=wal,
                bal=b_alpha.reshape(depth, 1, GLA_KW), gp=gp, mlg=ml_norm_g.reshape(depth, 1, ML_W),
                glg=gla_norm_g.reshape(depth, 1, GLA_VW), wout=w_out.astype(BF16),
                fg=final_g.reshape(1, D_MODEL))


def kernel(x, norm_g, w_in, b_gates, conv_w, w_alpha, b_alpha, rel_bias, ml_norm_g, gla_norm_g, w_out, final_g):
    depth = norm_g.shape[0]
    bsz, seq, _ = x.shape
    prm = _prep_params(norm_g, w_in, b_gates, conv_w, w_alpha, b_alpha, rel_bias, ml_norm_g, gla_norm_g,
                       w_out, final_g)
    x2d = x.reshape(bsz * seq, D_MODEL)
    for l in range(depth):
        x2d = _layer_call(x2d, l, prm, final_norm=(l == depth - 1), tiles_per_seq=seq // TILE)
    return x2d.reshape(bsz, seq, D_MODEL)
```
